```python
import jax
import jax.numpy as jnp
from jax import lax
import numpy as np

D_MODEL = 2048
BATCH = 8
SEQ = 2048
DEPTH = 2

GRID_W = 64
CTX_LEN = 256
N_HEADS = 16
N_KV_HEADS = 4
HEAD_DIM = D_MODEL // N_HEADS
KV_GROUP = N_HEADS // N_KV_HEADS
AXIS_ROT_DIM = HEAD_DIM // 2
ROPE_THETA = 10000.0
WINDOW = 128
BLOCK = 128
CONV_WIDTH = 3
N_EXPERTS = 16
EC_CAPACITY_FACTOR = 2
D_EXPERT = D_MODEL // 2
NORM_EPS = 1e-6
MASK_VALUE = -1e30
MIXER_ORDER = ('conv', 'attn')
N_MIXERS = 2
N_CONV_LAYERS = (DEPTH + 1) // 2
N_ATTN_LAYERS = DEPTH // 2

kernel_name = 'hybrid_shortconv_swa_ec_moe_dit'


def rms_norm(x, gain):
    xf = x.astype(jnp.float32)
    xf = xf * lax.rsqrt(jnp.mean(xf * xf, axis=-1, keepdims=True) + NORM_EPS)
    return (xf * gain.astype(jnp.float32)).astype(x.dtype)


def _row(v):
    return v[..., None, :]


def modulate(x, gain, shift, scale):
    return rms_norm(x, gain) * (1 + _row(scale)) + _row(shift)


def adaln(cond, w_ada, b_ada):
    return jnp.split(jax.nn.silu(cond) @ w_ada + b_ada, 6, axis=-1)


def axial_rope_angles(rows):
    row = jnp.repeat(jnp.arange(rows, dtype=jnp.float32), GRID_W)
    col = jnp.tile(jnp.arange(GRID_W, dtype=jnp.float32), rows)
    inv_freq = ROPE_THETA ** (-jnp.arange(AXIS_ROT_DIM // 2, dtype=jnp.float32) * 2.0 / AXIS_ROT_DIM)
    return row[:, None] * inv_freq, col[:, None] * inv_freq


def rope_rotate(x, ang):
    d2 = x.shape[-1] // 2
    cos = jnp.cos(ang)[:, None, :]
    sin = jnp.sin(ang)[:, None, :]
    x1, x2 = x[..., :d2], x[..., d2:]
    return jnp.concatenate([x1 * cos - x2 * sin, x2 * cos + x1 * sin], axis=-1)


def axial_rope(x, ang_row, ang_col):
    xf = x.astype(jnp.float32)
    out = jnp.concatenate([rope_rotate(xf[..., :AXIS_ROT_DIM], ang_row),
                           rope_rotate(xf[..., AXIS_ROT_DIM:], ang_col)], axis=-1)
    return out.astype(x.dtype)


def short_gated_conv(h, w_in, conv_w, w_out):
    s = h.shape[1]
    gate_b, gate_c, xv = jnp.split(h @ w_in, 3, axis=-1)
    u = gate_c * xv
    pad = CONV_WIDTH // 2
    up = jnp.pad(u, ((0, 0), (pad, pad), (0, 0)))
    y = conv_w[0] * up[:, 0:s]
    for j in range(1, CONV_WIDTH):
        y = y + conv_w[j] * up[:, j:j + s]
    return (gate_b * y) @ w_out


def windowed_gqa_sink(a_lat, a_ctx, w_qkv, q_gain, k_gain, sink, w_o, ang_row, ang_col, update_ctx):
    b, s, _ = a_lat.shape
    n_ctx = a_ctx.shape[1]
    n_blk = s // BLOCK
    dq = N_HEADS * HEAD_DIM
    dkv = N_KV_HEADS * HEAD_DIM
    scale = HEAD_DIM ** -0.5

    def heads(t, n_heads, gain):
        t = t.reshape(t.shape[0], t.shape[1], n_heads, HEAD_DIM)
        return t if gain is None else rms_norm(t, gain)

    qkv = a_lat @ w_qkv
    q = axial_rope(heads(qkv[..., :dq], N_HEADS, q_gain), ang_row, ang_col)
    k = axial_rope(heads(qkv[..., dq:dq + dkv], N_KV_HEADS, k_gain), ang_row, ang_col)
    v = heads(qkv[..., dq + dkv:], N_KV_HEADS, None)
    kv_ctx = a_ctx @ w_qkv[:, dq:]
    k_ctx = heads(kv_ctx[..., :dkv], N_KV_HEADS, k_gain)
    v_ctx = heads(kv_ctx[..., dkv:], N_KV_HEADS, None)
    sink_f = sink.astype(jnp.float32).reshape(N_KV_HEADS, KV_GROUP)

    qb = q.reshape(b, n_blk, BLOCK, N_KV_HEADS, KV_GROUP, HEAD_DIM)

    def band(t):
        tp = jnp.pad(t, ((0, 0), (BLOCK, BLOCK), (0, 0), (0, 0)))
        tb = tp.reshape(b, n_blk + 2, BLOCK, N_KV_HEADS, HEAD_DIM)
        return jnp.concatenate([tb[:, :-2], tb[:, 1:-1], tb[:, 2:]], axis=2)

    kb, vb = band(k), band(v)
    q_pos = jnp.arange(s).reshape(n_blk, BLOCK)
    k_pos = (jnp.arange(n_blk)[:, None] - 1) * BLOCK + jnp.arange(3 * BLOCK)[None, :]
    valid = ((k_pos[:, None, :] >= 0) & (k_pos[:, None, :] < s)
             & (jnp.abs(q_pos[:, :, None] - k_pos[:, None, :]) <= WINDOW))

    s_loc = jnp.einsum('bnqhgd,bnkhd->bhgnqk', qb, kb, preferred_element_type=jnp.float32) * scale
    s_loc = jnp.where(valid[None, None, None], s_loc, MASK_VALUE)
    s_ctx = jnp.einsum('bnqhgd,bchd->bhgnqc', qb, k_ctx, preferred_element_type=jnp.float32) * scale
    s_sink = jnp.broadcast_to(sink_f[None, :, :, None, None, None], s_loc.shape[:-1] + (1,))
    p = jax.nn.softmax(jnp.concatenate([s_loc, s_ctx, s_sink], axis=-1), axis=-1)
    p_loc = p[..., :3 * BLOCK].astype(v.dtype)
    p_ctx = p[..., 3 * BLOCK:3 * BLOCK + n_ctx].astype(v.dtype)
    o = (jnp.einsum('bhgnqk,bnkhd->bnqhgd', p_loc, vb)
         + jnp.einsum('bhgnqc,bchd->bnqhgd', p_ctx, v_ctx))
    y_lat = o.reshape(b, s, dq) @ w_o

    if not update_ctx:
        return y_lat, None
    q_ctx = heads(a_ctx @ w_qkv[:, :dq], N_HEADS, q_gain).reshape(b, n_ctx, N_KV_HEADS, KV_GROUP, HEAD_DIM)
    sc = jnp.einsum('bqhgd,bchd->bhgqc', q_ctx, k_ctx, preferred_element_type=jnp.float32) * scale
    sc_sink = jnp.broadcast_to(sink_f[None, :, :, None, None], sc.shape[:-1] + (1,))
    pc = jax.nn.softmax(jnp.concatenate([sc, sc_sink], axis=-1), axis=-1)[..., :n_ctx].astype(v_ctx.dtype)
    oc = jnp.einsum('bhgqc,bchd->bqhgd', pc, v_ctx).reshape(b, n_ctx, dq)
    return y_lat, oc @ w_o


def expert_choice_swiglu(h, w_router, w_gate, w_up, w_down):
    b, n, _ = h.shape
    cap = EC_CAPACITY_FACTOR * n // N_EXPERTS
    logits = jnp.einsum('bnd,de->bne', h, w_router, preferred_element_type=jnp.float32)
    affinity = jax.nn.softmax(logits, axis=-1)
    gates, idx = lax.top_k(jnp.swapaxes(affinity, 1, 2), cap)
    b_idx = jnp.arange(b)[:, None, None]
    xs = h[b_idx, idx]
    a = jnp.einsum('becd,edf->becf', xs, w_gate)
    u = jnp.einsum('becd,edf->becf', xs, w_up)
    y = jnp.einsum('becf,efd->becd', jax.nn.silu(a) * u, w_down)
    y = y * gates[..., None].astype(y.dtype)
    return jnp.zeros_like(h).at[b_idx, idx].add(y)


def setup_inputs(seed: int = 0) -> dict:
    key = jax.random.key(seed)
    ks = jax.random.split(key, 24)
    d, f = D_MODEL, D_EXPERT
    dqkv = (N_HEADS + 2 * N_KV_HEADS) * HEAD_DIM
    nrm = jax.random.normal
    return {
        'x': nrm(ks[0], (BATCH, SEQ, d), jnp.float32),
        'c': nrm(ks[1], (BATCH, d), jnp.float32),
        'ctx': nrm(ks[2], (BATCH, CTX_LEN, d), jnp.float32),
        'c_ctx': nrm(ks[3], (d,), jnp.float32),
        'ada_w': nrm(ks[4], (DEPTH, d, 6 * d), jnp.float32) * (0.5 * d ** -0.5),
        'ada_b': nrm(ks[5], (DEPTH, 6 * d), jnp.float32) * 0.02,
        'norm_mix_g': 1.0 + 0.02 * nrm(ks[6], (DEPTH, d), jnp.float32),
        'norm_ffn_g': 1.0 + 0.02 * nrm(ks[7], (DEPTH, d), jnp.float32),
        'conv_w_in': nrm(ks[8], (N_CONV_LAYERS, d, 3 * d), jnp.float32) * d ** -0.5,
        'conv_w': nrm(ks[9], (N_CONV_LAYERS, CONV_WIDTH, d), jnp.float32) * CONV_WIDTH ** -0.5,
        'conv_w_out': nrm(ks[10], (N_CONV_LAYERS, d, d), jnp.float32) * d ** -0.5,
        'attn_w_qkv': nrm(ks[11], (N_ATTN_LAYERS, d, dqkv), jnp.float32) * d ** -0.5,
        'attn_q_gain': 1.0 + 0.02 * nrm(ks[12], (N_ATTN_LAYERS, HEAD_DIM), jnp.float32),
        'attn_k_gain': 1.0 + 0.02 * nrm(ks[13], (N_ATTN_LAYERS, HEAD_DIM), jnp.float32),
        'attn_sink': nrm(ks[14], (N_ATTN_LAYERS, N_HEADS), jnp.float32),
        'attn_w_o': nrm(ks[15], (N_ATTN_LAYERS, N_HEADS * HEAD_DIM, d), jnp.float32) * d ** -0.5,
        'router_w': nrm(ks[16], (DEPTH, d, N_EXPERTS), jnp.float32) * d ** -0.5,
        'expert_w_gate': nrm(ks[17], (DEPTH, N_EXPERTS, d, f), jnp.float32) * d ** -0.5,
        'expert_w_up': nrm(ks[18], (DEPTH, N_EXPERTS, d, f), jnp.float32) * d ** -0.5,
        'expert_w_down': nrm(ks[19], (DEPTH, N_EXPERTS, f, d), jnp.float32) * f ** -0.5,
    }


def reference(x, c, ctx, c_ctx, ada_w, ada_b, norm_mix_g, norm_ffn_g, conv_w_in, conv_w, conv_w_out,
              attn_w_qkv, attn_q_gain, attn_k_gain, attn_sink, attn_w_o, router_w,
              expert_w_gate, expert_w_up, expert_w_down):
    s = x.shape[1]
    rows = s // GRID_W
    ang_row, ang_col = axial_rope_angles(rows)
    h_lat, h_ctx = x, ctx
    i_conv = 0
    i_attn = 0
    for i in range(DEPTH):
        kind = MIXER_ORDER[i % N_MIXERS]
        ctx_live = any(MIXER_ORDER[j % N_MIXERS] == 'attn' for j in range(i + 1, DEPTH))
        sh1, sc1, g1, sh2, sc2, g2 = adaln(c, ada_w[i], ada_b[i])
        csh1, csc1, cg1, csh2, csc2, cg2 = adaln(c_ctx, ada_w[i], ada_b[i])
        a_lat = modulate(h_lat, norm_mix_g[i], sh1, sc1)
        if kind == 'conv':
            y_lat = short_gated_conv(a_lat, conv_w_in[i_conv], conv_w[i_conv], conv_w_out[i_conv])
            y_ctx = None
            if ctx_live:
                a_ctx = modulate(h_ctx, norm_mix_g[i], csh1, csc1)
                y_ctx = short_gated_conv(a_ctx, conv_w_in[i_conv], conv_w[i_conv], conv_w_out[i_conv])
            i_conv += 1
        else:
            a_ctx = modulate(h_ctx, norm_mix_g[i], csh1, csc1)
            y_lat, y_ctx = windowed_gqa_sink(a_lat, a_ctx, attn_w_qkv[i_attn], attn_q_gain[i_attn],
                                             attn_k_gain[i_attn], attn_sink[i_attn], attn_w_o[i_attn],
                                             ang_row, ang_col, ctx_live)
            i_attn += 1
        h_lat = h_lat + _row(g1) * y_lat
        h_lat = h_lat + _row(g2) * expert_choice_swiglu(modulate(h_lat, norm_ffn_g[i], sh2, sc2), router_w[i],
                                                        expert_w_gate[i], expert_w_up[i], expert_w_down[i])
        if ctx_live:
            h_ctx = h_ctx + _row(cg1) * y_ctx
            h_ctx = h_ctx + _row(cg2) * expert_choice_swiglu(modulate(h_ctx, norm_ffn_g[i], csh2, csc2),
                                                             router_w[i], expert_w_gate[i], expert_w_up[i],
                                                             expert_w_down[i])
    return h_lat
```

```python
import dataclasses
import functools

import jax
import jax.numpy as jnp
from jax import lax
from jax.experimental import pallas as pl
from jax.experimental.pallas import tpu as pltpu

F32 = jnp.float32
BF16 = jnp.bfloat16
I32 = jnp.int32

NORM_EPS = 1e-6
MASK_VALUE = -1e30
ROPE_THETA = 10000.0
HEAD_DIM = 128
ATTN_BLOCK = 128
N_ADA = 6
ADA_ROWS = 16
MAX_LEVELS = 16
VMEM_LIMIT = 56 * 1024 * 1024


@dataclasses.dataclass(frozen=True)
class Dims:
    d: int
    batch: int
    seq: int
    grid_w: int
    ctx: int
    heads: int
    kv_heads: int
    experts: int
    cap_factor: int
    d_expert: int

    @property
    def rows_lat(self):
        return self.batch * self.seq

    @property
    def rows_ctx(self):
        return self.batch * self.ctx

    @property
    def rows(self):
        return self.rows_lat + self.rows_ctx

    @property
    def dq(self):
        return self.heads * HEAD_DIM

    @property
    def dkv(self):
        return self.kv_heads * HEAD_DIM

    @property
    def cap_lat(self):
        return self.cap_factor * self.seq // self.experts

    @property
    def cap_ctx(self):
        return self.cap_factor * self.ctx // self.experts

    @property
    def tm(self):
        t = 1024
        while self.seq % t or self.rows_ctx % t:
            t //= 2
        return t


def _params(*sem):
    return pltpu.CompilerParams(dimension_semantics=sem, vmem_limit_bytes=VMEM_LIMIT)


def _modulate(h, gain, shift, scale):
    ms = jnp.mean(h * h, axis=-1, keepdims=True)
    xn = h * lax.rsqrt(ms + NORM_EPS)
    return (xn * gain) * (1.0 + scale) + shift


def _mod_row(dims, tm):
    per = dims.seq // tm
    return lambda m: jnp.minimum(m // per, dims.batch)


def _mod_spec(dims, tm, which, width=None, col=False):
    row = _mod_row(dims, tm)
    w = dims.d if width is None else width
    if col:
        return pl.BlockSpec((1, 1, w), lambda m, n: (row(m) * N_ADA + which, 0, n))
    return pl.BlockSpec((1, 1, w), lambda m, n: (row(m) * N_ADA + which, 0, 0))


def _adaln_body(c_ref, w_ref, b_ref, o_ref):
    c = c_ref[...]
    s = c * jax.nn.sigmoid(c)
    o_ref[0] = jnp.dot(s.astype(BF16), w_ref[0].astype(BF16), preferred_element_type=F32) + b_ref[0]


def _adaln(dims, cvec, ada_w, ada_b):
    depth, d, n = ada_w.shape
    tn = min(1024, n)
    return pl.pallas_call(
        _adaln_body,
        grid=(depth, n // tn),
        in_specs=[
            pl.BlockSpec((ADA_ROWS, d), lambda l, j: (0, 0)),
            pl.BlockSpec((1, d, tn), lambda l, j: (l, 0, j)),
            pl.BlockSpec((1, 1, tn), lambda l, j: (l, 0, j)),
        ],
        out_specs=pl.BlockSpec((1, ADA_ROWS, tn), lambda l, j: (l, 0, j)),
        out_shape=jax.ShapeDtypeStruct((depth, ADA_ROWS, n), F32),
        compiler_params=_params("arbitrary", "arbitrary"),
        name="adaln",
    )(cvec, ada_w, ada_b.reshape(depth, 1, n))


def _inproj_body(h_ref, g_ref, sh_ref, sc_ref, wb_ref, wc_ref, wx_ref, gb_ref, u_ref, a_scr):
    @pl.when(pl.program_id(1) == 0)
    def _():
        a_scr[...] = _modulate(h_ref[...], g_ref[...], sh_ref[0], sc_ref[0]).astype(BF16)

    a = a_scr[...]
    gb = jnp.dot(a, wb_ref[...].astype(BF16), preferred_element_type=F32)
    gc = jnp.dot(a, wc_ref[...].astype(BF16), preferred_element_type=F32)
    xv = jnp.dot(a, wx_ref[...].astype(BF16), preferred_element_type=F32)
    gb_ref[...] = gb.astype(BF16)
    u_ref[...] = (gc * xv).astype(BF16)


def _inproj(dims, h, mods, gain, w_in):
    rows, d = h.shape
    tm, tn = dims.tm, 256
    nt = d // tn
    return pl.pallas_call(
        _inproj_body,
        grid=(rows // tm, nt),
        in_specs=[
            pl.BlockSpec((tm, d), lambda m, n: (m, 0)),
            pl.BlockSpec((1, d), lambda m, n: (0, 0)),
            _mod_spec(dims, tm, 0),
            _mod_spec(dims, tm, 1),
            pl.BlockSpec((d, tn), lambda m, n: (0, n)),
            pl.BlockSpec((d, tn), lambda m, n: (0, nt + n)),
            pl.BlockSpec((d, tn), lambda m, n: (0, 2 * nt + n)),
        ],
        out_specs=[pl.BlockSpec((tm, tn), lambda m, n: (m, n))] * 2,
        out_shape=[jax.ShapeDtypeStruct((rows, d), BF16)] * 2,
        scratch_shapes=[pltpu.VMEM((tm, d), BF16)],
        compiler_params=_params("arbitrary", "arbitrary"),
        name="conv_inproj",
    )(h, gain.reshape(1, d), mods, mods, w_in, w_in, w_in)


def _convout_body(gb_ref, u_ref, up_ref, un_ref, cw_ref, w_ref, h_ref, g1_ref, o_ref, v_scr, *, dims, tm):
    @pl.when(pl.program_id(1) == 0)
    def _():
        u = u_ref[...].astype(F32)
        halo = up_ref.shape[0]
        row = lax.broadcasted_iota(I32, (tm, 1), 0)
        grow = pl.program_id(0) * tm + row
        seq_len = jnp.where(grow < dims.rows_lat, dims.seq, dims.ctx)
        pos = grow & (seq_len - 1)
        u_dn = pltpu.roll(u, 1, 0)
        u_dn = jnp.where(row == 0, up_ref[halo - 1:halo, :].astype(F32), u_dn)
        u_dn = jnp.where(pos == 0, 0.0, u_dn)
        u_up = pltpu.roll(u, tm - 1, 0)
        u_up = jnp.where(row == tm - 1, un_ref[0:1, :].astype(F32), u_up)
        u_up = jnp.where(pos == seq_len - 1, 0.0, u_up)
        y = cw_ref[0:1, :] * u_dn + cw_ref[1:2, :] * u + cw_ref[2:3, :] * u_up
        v_scr[...] = (gb_ref[...].astype(F32) * y).astype(BF16)

    out = jnp.dot(v_scr[...], w_ref[...].astype(BF16), preferred_element_type=F32)
    o_ref[...] = h_ref[...] + g1_ref[0] * out


def _convout(dims, gb, u, conv_w, w_out, h, mods):
    rows, d = h.shape
    tm, tn, halo = dims.tm, 512, 16
    assert dims.seq & (dims.seq - 1) == 0 and dims.ctx & (dims.ctx - 1) == 0
    per = tm // halo
    last = rows // halo - 1
    return pl.pallas_call(
        functools.partial(_convout_body, dims=dims, tm=tm),
        grid=(rows // tm, d // tn),
        in_specs=[
            pl.BlockSpec((tm, d), lambda m, n: (m, 0)),
            pl.BlockSpec((tm, d), lambda m, n: (m, 0)),
            pl.BlockSpec((halo, d), lambda m, n: (jnp.maximum(m * per - 1, 0), 0)),
            pl.BlockSpec((halo, d), lambda m, n: (jnp.minimum((m + 1) * per, last), 0)),
            pl.BlockSpec((3, d), lambda m, n: (0, 0)),
            pl.BlockSpec((d, tn), lambda m, n: (0, n)),
            pl.BlockSpec((tm, tn), lambda m, n: (m, n)),
            _mod_spec(dims, tm, 2, width=tn, col=True),
        ],
        out_specs=pl.BlockSpec((tm, tn), lambda m, n: (m, n)),
        out_shape=jax.ShapeDtypeStruct((rows, d), F32),
        scratch_shapes=[pltpu.VMEM((tm, d), BF16)],
        compiler_params=_params("arbitrary", "arbitrary"),
        name="conv_outproj",
    )(gb, u, u, u, conv_w, w_out, h, mods)


def _qkv_body(h_ref, g_ref, sh_ref, sc_ref, w_ref, qg_ref, kg_ref, cos_ref, sin_ref, o_ref, a_scr, *, n_q, tn):
    n = pl.program_id(1)

    @pl.when(n == 0)
    def _():
        a_scr[...] = _modulate(h_ref[...], g_ref[...], sh_ref[0], sc_ref[0]).astype(BF16)

    acc = jnp.dot(a_scr[...], w_ref[...].astype(BF16), preferred_element_type=F32)

    def norm_rope(gain, scale):
        cos = cos_ref[...]
        sin = sin_ref[...]
        lane = lax.broadcasted_iota(I32, (1, HEAD_DIM), 1)
        first = (lane % (HEAD_DIM // 2)) < (HEAD_DIM // 4)
        outs = []
        for hd in range(tn // HEAD_DIM):
            x = acc[:, hd * HEAD_DIM:(hd + 1) * HEAD_DIM]
            ms = jnp.mean(x * x, axis=-1, keepdims=True)
            xn = x * lax.rsqrt(ms + NORM_EPS) * gain
            rot = jnp.where(first, pltpu.roll(xn, HEAD_DIM - HEAD_DIM // 4, 1), pltpu.roll(xn, HEAD_DIM // 4, 1))
            outs.append(((xn * cos + rot * sin) * scale).astype(BF16))
        return outs[0] if len(outs) == 1 else jnp.concatenate(outs, axis=1)

    @pl.when(n < n_q)
    def _():
        o_ref[...] = norm_rope(qg_ref[...], HEAD_DIM ** -0.5)

    @pl.when(n == n_q)
    def _():
        o_ref[...] = norm_rope(kg_ref[...], 1.0)

    @pl.when(n > n_q)
    def _():
        o_ref[...] = acc.astype(BF16)


def _rope_tables(dims, tm):
    half = HEAD_DIM // 4
    inv_freq = ROPE_THETA ** (-jnp.arange(half, dtype=F32) * 2.0 / (HEAD_DIM // 2))
    t = jnp.arange(dims.seq)
    ang_row = (t // dims.grid_w).astype(F32)[:, None] * inv_freq
    ang_col = (t % dims.grid_w).astype(F32)[:, None] * inv_freq
    cos = jnp.concatenate([jnp.cos(ang_row)] * 2 + [jnp.cos(ang_col)] * 2, axis=-1)
    sin = jnp.concatenate([-jnp.sin(ang_row), jnp.sin(ang_row), -jnp.sin(ang_col), jnp.sin(ang_col)], axis=-1)
    cos = jnp.concatenate([cos, jnp.ones((tm, HEAD_DIM), F32)], axis=0)
    sin = jnp.concatenate([sin, jnp.zeros((tm, HEAD_DIM), F32)], axis=0)
    return cos, sin


def _qkvproj(dims, h, mods, gain, w_qkv, q_gain, k_gain):
    rows, d = h.shape
    tm, tn = dims.tm, dims.dkv
    nq = dims.dq // tn
    n_total = (dims.dq + 2 * dims.dkv) // tn
    cos, sin = _rope_tables(dims, tm)
    per = dims.seq // tm
    lat_tiles = dims.rows_lat // tm
    pos_tile = lambda m, n: (jnp.where(m < lat_tiles, m % per, per), 0)
    return pl.pallas_call(
        functools.partial(_qkv_body, n_q=nq, tn=tn),
        grid=(rows // tm, n_total),
        in_specs=[
            pl.BlockSpec((tm, d), lambda m, n: (m, 0)),
            pl.BlockSpec((1, d), lambda m, n: (0, 0)),
            _mod_spec(dims, tm, 0),
            _mod_spec(dims, tm, 1),
            pl.BlockSpec((d, tn), lambda m, n: (0, n)),
            pl.BlockSpec((1, HEAD_DIM), lambda m, n: (0, 0)),
            pl.BlockSpec((1, HEAD_DIM), lambda m, n: (0, 0)),
            pl.BlockSpec((tm, HEAD_DIM), pos_tile),
            pl.BlockSpec((tm, HEAD_DIM), pos_tile),
        ],
        out_specs=pl.BlockSpec((tm, tn), lambda m, n: (m, n)),
        out_shape=jax.ShapeDtypeStruct((rows, n_total * tn), BF16),
        scratch_shapes=[pltpu.VMEM((tm, d), BF16)],
        compiler_params=_params("arbitrary", "arbitrary"),
        name="attn_qkv",
    )(h, gain.reshape(1, d), mods, mods, w_qkv, q_gain.reshape(1, HEAD_DIM), k_gain.reshape(1, HEAD_DIM), cos, sin)


def _attn_body(sink_ref, q_ref, k_ref, v_ref, kc_ref, vc_ref, o_ref, *, dims):
    group = dims.heads // dims.kv_heads
    blk = ATTN_BLOCK
    band = 3 * blk
    n_ctx = dims.ctx
    head0 = pl.program_id(1) * group
    kc = kc_ref[...]
    vc = vc_ref[...]

    def block(n, carry):
        q0 = pl.multiple_of(n * blk, blk)
        start = pl.multiple_of(jnp.clip((n - 1) * blk, 0, dims.seq - band), blk)
        kcat = jnp.concatenate([k_ref[pl.ds(start, band), :], kc], axis=0)
        vcat = jnp.concatenate([v_ref[pl.ds(start, band), :], vc], axis=0)
        q_pos = q0 + lax.broadcasted_iota(I32, (blk, 1), 0)
        col = lax.broadcasted_iota(I32, (1, band + n_ctx), 1)
        valid = (col >= band) | (jnp.abs(q_pos - (start + col)) <= blk)
        for g in range(group):
            q = q_ref[pl.ds(q0, blk), g * HEAD_DIM:(g + 1) * HEAD_DIM]
            s = lax.dot_general(q, kcat, (((1,), (1,)), ((), ())), preferred_element_type=F32)
            s = jnp.where(valid, s, MASK_VALUE)
            sink = sink_ref[head0 + g]
            m = jnp.maximum(jnp.max(s, axis=-1, keepdims=True), sink)
            p = jnp.exp(s - m)
            den = jnp.sum(p, axis=-1, keepdims=True) + jnp.exp(sink - m)
            o = jnp.dot(p.astype(BF16), vcat, preferred_element_type=F32) / den
            o_ref[pl.ds(q0, blk), g * HEAD_DIM:(g + 1) * HEAD_DIM] = o.astype(BF16)
        return carry

    lax.fori_loop(0, dims.seq // blk, block, 0)


def _attention(dims, qkv, sink):
    group = dims.heads // dims.kv_heads
    gw = group * HEAD_DIM
    k_col = dims.dq // HEAD_DIM
    v_col = (dims.dq + dims.dkv) // HEAD_DIM
    ctx_blk0 = dims.rows_lat // dims.ctx
    return pl.pallas_call(
        functools.partial(_attn_body, dims=dims),
        grid=(dims.batch, dims.kv_heads),
        in_specs=[
            pl.BlockSpec(memory_space=pltpu.SMEM),
            pl.BlockSpec((dims.seq, gw), lambda b, h: (b, h)),
            pl.BlockSpec((dims.seq, HEAD_DIM), lambda b, h: (b, k_col + h)),
            pl.BlockSpec((dims.seq, HEAD_DIM), lambda b, h: (b, v_col + h)),
            pl.BlockSpec((dims.ctx, HEAD_DIM), lambda b, h: (ctx_blk0 + b, k_col + h)),
            pl.BlockSpec((dims.ctx, HEAD_DIM), lambda b, h: (ctx_blk0 + b, v_col + h)),
        ],
        out_specs=pl.BlockSpec((dims.seq, gw), lambda b, h: (b, h)),
        out_shape=jax.ShapeDtypeStruct((dims.rows_lat, dims.dq), BF16),
        compiler_params=_params("arbitrary", "arbitrary"),
        name="attn_core",
    )(sink, qkv, qkv, qkv, qkv, qkv)


def _oproj_body(o_ref, w_ref, h_ref, g1_ref, out_ref):
    y = jnp.dot(o_ref[...], w_ref[...].astype(BF16), preferred_element_type=F32)
    out_ref[...] = h_ref[...] + g1_ref[0] * y


def _oproj(dims, o, w_o, h, mods):
    rows, dq = o.shape
    d = w_o.shape[1]
    tm, tn = dims.tm, 512
    return pl.pallas_call(
        _oproj_body,
        grid=(rows // tm, d // tn),
        in_specs=[
            pl.BlockSpec((tm, dq), lambda m, n: (m, 0)),
            pl.BlockSpec((dq, tn), lambda m, n: (0, n)),
            pl.BlockSpec((tm, tn), lambda m, n: (m, n)),
            _mod_spec(dims, tm, 2, width=tn, col=True),
        ],
        out_specs=pl.BlockSpec((tm, tn), lambda m, n: (m, n)),
        out_shape=jax.ShapeDtypeStruct((rows, d), F32),
        compiler_params=_params("arbitrary", "arbitrary"),
        name="attn_oproj",
    )(o, w_o, h, mods)


def _router_body(h_ref, g_ref, sh_ref, sc_ref, wr_ref, h2_ref, aff_ref, *, experts):
    a = _modulate(h_ref[...], g_ref[...], sh_ref[0], sc_ref[0])
    h2_ref[...] = a
    logits = jnp.dot(a.astype(BF16), wr_ref[...], preferred_element_type=F32)
    lt = logits.T[0:experts, :]
    ex = jnp.exp(lt - jnp.max(lt, axis=0, keepdims=True))
    aff_ref[...] = ex / jnp.sum(ex, axis=0, keepdims=True)


def _router(dims, h, mods, gain, w_router):
    rows, d = h.shape
    tm = min(dims.tm, 512)
    wr = jnp.zeros((d, 128), BF16).at[:, :dims.experts].set(w_router.astype(BF16))
    return pl.pallas_call(
        functools.partial(_router_body, experts=dims.experts),
        grid=(rows // tm, 1),
        in_specs=[
            pl.BlockSpec((tm, d), lambda m, n: (m, 0)),
            pl.BlockSpec((1, d), lambda m, n: (0, 0)),
            _mod_spec(dims, tm, 3),
            _mod_spec(dims, tm, 4),
            pl.BlockSpec((d, 128), lambda m, n: (0, 0)),
        ],
        out_specs=[
            pl.BlockSpec((tm, d), lambda m, n: (m, 0)),
            pl.BlockSpec((dims.experts, tm), lambda m, n: (0, m)),
        ],
        out_shape=[
            jax.ShapeDtypeStruct((rows, d), F32),
            jax.ShapeDtypeStruct((dims.experts, rows), F32),
        ],
        compiler_params=_params("arbitrary", "arbitrary"),
        name="moe_router",
    )(h, gain.reshape(1, d), mods, mods, wr)


def _topk_body(aff_ref, idx_ref, gate_ref, slot_ref, cnt_ref, pos_scr, sel_scr, *,
               n_tok, cap, row0, slot0, slots_per_expert, experts):
    grp = pl.program_id(0)
    a = aff_ref[...]
    bits = pltpu.bitcast(a, I32)
    lane = lax.broadcasted_iota(I32, (experts, n_tok), 1)

    def count(mask):
        return jnp.sum(mask.astype(I32), axis=1, keepdims=True)

    def thr_step(i, thr):
        cand = thr | jnp.left_shift(jnp.int32(1), 30 - i)
        return jnp.where(count(bits >= cand) >= cap, cand, thr)

    thr = lax.fori_loop(0, 31, thr_step, jnp.zeros((experts, 1), I32))
    above = bits > thr
    tie = bits == thr
    need = cap - count(above)

    def tie_step(i, lim):
        cand = lim | jnp.left_shift(jnp.int32(1), (n_tok.bit_length() - 2) - i)
        return jnp.where(count(tie & (lane < cand)) < need, cand, lim)

    lim = lax.fori_loop(0, n_tok.bit_length() - 1, tie_step, jnp.zeros((experts, 1), I32))
    sel = above | (tie & (lane <= lim))
    self32 = sel.astype(F32)

    r = lax.broadcasted_iota(I32, (128, 128), 0)
    c = lax.broadcasted_iota(I32, (128, 128), 1)
    tri = (r <= c).astype(F32).astype(BF16)
    off = jnp.zeros((experts, 1), F32)
    chunks = []
    for ch in range(n_tok // 128):
        x = self32[:, ch * 128:(ch + 1) * 128]
        incl = jnp.dot(x.astype(BF16), tri, preferred_element_type=F32)
        chunks.append(incl - x + off)
        off = off + incl[:, 127:128]
    pos = jnp.concatenate(chunks, axis=1).astype(I32) if len(chunks) > 1 else chunks[0].astype(I32)
    sel_i = sel.astype(I32)
    pos_scr[...] = pos
    sel_scr[...] = sel_i

    sc = min(cap, 64)
    tok = lax.broadcasted_iota(I32, (1, n_tok), 1)
    slot_iota = lax.broadcasted_iota(I32, (sc, 1), 0)

    def per_expert(e, carry):
        pe = pos_scr[pl.ds(e, 1), :]
        se = sel_scr[pl.ds(e, 1), :] > 0
        ae = aff_ref[pl.ds(e, 1), :]
        for s0 in range(0, cap, sc):
            hit = (pe == (slot_iota + s0)) & se
            idx_ref[0, e, s0:s0 + sc, :] = (jnp.sum(jnp.where(hit, tok, 0), axis=1, keepdims=True)
                                             + (row0 + grp * n_tok))
            gate_ref[0, e, s0:s0 + sc, :] = jnp.sum(jnp.where(hit, ae, 0.0), axis=1, keepdims=True)
        return carry

    lax.fori_loop(0, experts, per_expert, 0)

    level_iota = lax.broadcasted_iota(I32, (MAX_LEVELS, 1), 0)
    level = jnp.zeros((1, n_tok), I32)
    slots = jnp.zeros((MAX_LEVELS, n_tok), I32)
    for e in range(experts):
        se = sel_i[e:e + 1, :]
        flat = pos[e:e + 1, :] + (e * slots_per_expert + slot0 + grp * cap)
        slots = jnp.where((se > 0) & (level == level_iota), flat, slots)
        level = level + se
    slot_ref[...] = slots
    cnt_ref[...] = level


def _topk(dims, aff, n_tok, cap, row0, slot0, slots_per_expert, n_groups, col0):
    e = dims.experts
    return pl.pallas_call(
        functools.partial(_topk_body, n_tok=n_tok, cap=cap, row0=row0, slot0=slot0,
                          slots_per_expert=slots_per_expert, experts=e),
        grid=(n_groups,),
        in_specs=[pl.BlockSpec((e, n_tok), lambda g: (0, col0 + g))],
        out_specs=[
            pl.BlockSpec((1, e, cap, 1), lambda g: (g, 0, 0, 0)),
            pl.BlockSpec((1, e, cap, 1), lambda g: (g, 0, 0, 0)),
            pl.BlockSpec((MAX_LEVELS, n_tok), lambda g: (0, g)),
            pl.BlockSpec((1, n_tok), lambda g: (0, g)),
        ],
        out_shape=[
            jax.ShapeDtypeStruct((n_groups, e, cap, 1), I32),
            jax.ShapeDtypeStruct((n_groups, e, cap, 1), F32),
            jax.ShapeDtypeStruct((MAX_LEVELS, n_groups * n_tok), I32),
            jax.ShapeDtypeStruct((1, n_groups * n_tok), I32),
        ],
        scratch_shapes=[pltpu.VMEM((e, n_tok), I32), pltpu.VMEM((e, n_tok), I32)],
        compiler_params=_params("arbitrary"),
        name="moe_topk",
    )(aff)


def _expert_body(idx_ref, h2_hbm, gate_ref, wg_ref, wu_ref, wd_ref, o_ref, x32_scr, x_scr, mid_scr, sem, *,
                 tmx, n_f, tf):
    s = pl.program_id(2)

    @pl.when(s == 0)
    def _():
        def row_copy(r, row):
            return pltpu.make_async_copy(h2_hbm.at[pl.ds(row, 1), :], x32_scr.at[pl.ds(r, 1), :], sem)

        def issue(r, carry):
            row_copy(r, idx_ref[0, 0, r]).start()
            return carry

        lax.fori_loop(0, tmx, issue, 0)

        def drain(r, carry):
            row_copy(r, 0).wait()
            return carry

        lax.fori_loop(0, tmx, drain, 0)
        x_scr[...] = x32_scr[...].astype(BF16)

    @pl.when(s < n_f)
    def _():
        x = x_scr[...]
        a = jnp.dot(x, wg_ref[0].astype(BF16), preferred_element_type=F32)
        u = jnp.dot(x, wu_ref[0].astype(BF16), preferred_element_type=F32)
        mid_scr[s] = (a * jax.nn.sigmoid(a) * u).astype(BF16)

    @pl.when(s >= n_f)
    def _():
        y = jnp.dot(mid_scr[0], wd_ref[0, 0:tf, :].astype(BF16), preferred_element_type=F32)
        for f in range(1, n_f):
            y += jnp.dot(mid_scr[f], wd_ref[0, f * tf:(f + 1) * tf, :].astype(BF16), preferred_element_type=F32)
        o_ref[0] = y * gate_ref[0]


def _experts(dims, idx, gate, h2, w_gate, w_up, w_down, tmx):
    e, slots = idx.shape
    d, fdim = w_gate.shape[1], w_gate.shape[2]
    tf, tn = 256, 512
    n_m, n_f, n_n = slots // tmx, fdim // tf, d // tn
    idx3 = idx.reshape(e * n_m, 1, tmx)
    up_chunk = lambda i, m, s: (i, 0, jnp.minimum(s, n_f - 1))
    down_chunk = lambda s: jnp.maximum(s - n_f, 0)
    return pl.pallas_call(
        functools.partial(_expert_body, tmx=tmx, n_f=n_f, tf=tf),
        grid=(e, n_m, n_f + n_n),
        in_specs=[
            pl.BlockSpec((1, 1, tmx), lambda i, m, s: (i * n_m + m, 0, 0), memory_space=pltpu.SMEM),
            pl.BlockSpec(memory_space=pl.ANY),
            pl.BlockSpec((1, tmx, 1), lambda i, m, s: (i, m, 0)),
            pl.BlockSpec((1, d, tf), up_chunk),
            pl.BlockSpec((1, d, tf), up_chunk),
            pl.BlockSpec((1, fdim, tn), lambda i, m, s: (i, 0, down_chunk(s))),
        ],
        out_specs=pl.BlockSpec((1, tmx, tn), lambda i, m, s: (i, m, down_chunk(s))),
        out_shape=jax.ShapeDtypeStruct((e, slots, d), F32),
        scratch_shapes=[
            pltpu.VMEM((tmx, d), F32),
            pltpu.VMEM((tmx, d), BF16),
            pltpu.VMEM((n_f, tmx, tf), BF16),
            pltpu.SemaphoreType.DMA(()),
        ],
        compiler_params=_params("arbitrary", "arbitrary", "arbitrary"),
        name="moe_experts",
    )(idx3, h2, gate, w_gate, w_up, w_down)


def _combine_body(slot_ref, cnt_ref, cntv_ref, y_hbm, h_ref, g2_ref, o_ref, buf, sem, *, tt):
    @pl.when(pl.program_id(0) == 0)
    def _():
        buf[...] = jnp.zeros_like(buf)

    def copy(t, j):
        return pltpu.make_async_copy(y_hbm.at[pl.ds(slot_ref[0, j, t], 1), :], buf.at[j, pl.ds(t, 1), :], sem)

    def issue(t, top):
        n = cnt_ref[0, 0, t]
        lax.fori_loop(0, n, lambda j, c: (copy(t, j).start(), c)[1], 0)
        return jnp.maximum(top, n)

    top = lax.fori_loop(0, tt, issue, 0)

    def drain(t, carry):
        lax.fori_loop(0, cnt_ref[0, 0, t], lambda j, c: (copy(t, j).wait(), c)[1], 0)
        return carry

    lax.fori_loop(0, tt, drain, 0)

    cnt = cntv_ref[...]
    o_ref[...] = h_ref[...]
    for j in range(MAX_LEVELS):
        @pl.when(j < top)
        def _():
            o_ref[...] += g2_ref[0] * jnp.where(cnt > j, buf[j], 0.0)


def _combine(dims, slots, cnt, y, h, mods):
    rows, d = h.shape
    tt = 64
    n_t = rows // tt
    per = dims.seq // tt
    mod_row = lambda m: jnp.minimum(m // per, dims.batch)
    slot3 = slots.reshape(MAX_LEVELS, n_t, tt).transpose(1, 0, 2)
    cnt3 = cnt.reshape(n_t, 1, tt)
    return pl.pallas_call(
        functools.partial(_combine_body, tt=tt),
        grid=(n_t,),
        in_specs=[
            pl.BlockSpec((1, MAX_LEVELS, tt), lambda m: (m, 0, 0), memory_space=pltpu.SMEM),
            pl.BlockSpec((1, 1, tt), lambda m: (m, 0, 0), memory_space=pltpu.SMEM),
            pl.BlockSpec((tt, 1), lambda m: (m, 0)),
            pl.BlockSpec(memory_space=pl.ANY),
            pl.BlockSpec((tt, d), lambda m: (m, 0)),
            pl.BlockSpec((1, 1, d), lambda m: (mod_row(m) * N_ADA + 5, 0, 0)),
        ],
        out_specs=pl.BlockSpec((tt, d), lambda m: (m, 0)),
        out_shape=jax.ShapeDtypeStruct((rows, d), F32),
        scratch_shapes=[pltpu.VMEM((MAX_LEVELS, tt, d), F32), pltpu.SemaphoreType.DMA(())],
        compiler_params=_params("arbitrary"),
        name="moe_combine",
    )(slot3, cnt3, cnt.reshape(rows, 1), y.reshape(-1, d), h, mods)


def _moe(dims, h, mods, gain, w_router, w_gate, w_up, w_down, with_ctx):
    rows, d = h.shape
    e = dims.experts
    h2, aff = _router(dims, h, mods, gain, w_router)
    slots_per_expert = dims.batch * dims.cap_lat + (dims.batch * dims.cap_ctx if with_ctx else 0)
    parts = [_topk(dims, aff, dims.seq, dims.cap_lat, 0, 0, slots_per_expert, dims.batch, 0)]
    if with_ctx:
        parts.append(_topk(dims, aff, dims.ctx, dims.cap_ctx, dims.rows_lat, dims.batch * dims.cap_lat,
                           slots_per_expert, dims.batch, dims.rows_lat // dims.ctx))
    flat = lambda t: t[..., 0].transpose(1, 0, 2).reshape(e, -1)
    idx = jnp.concatenate([flat(p[0]) for p in parts], axis=1)
    gate = jnp.concatenate([flat(p[1]) for p in parts], axis=1)[..., None]
    slots = jnp.concatenate([p[2] for p in parts], axis=1)
    cnt = jnp.concatenate([p[3] for p in parts], axis=1)
    tmx = slots_per_expert // 2
    y = _experts(dims, idx, gate, h2, w_gate, w_up, w_down, tmx)
    return _combine(dims, slots, cnt, y, h, mods)


def _forward(dims, x, c, ctx, c_ctx, ada_w, ada_b, norm_mix_g, norm_ffn_g, conv_w_in, conv_w, conv_w_out,
             attn_w_qkv, attn_q_gain, attn_k_gain, attn_sink, attn_w_o, router_w,
             expert_w_gate, expert_w_up, expert_w_down):
    d = dims.d
    cvec = jnp.zeros((ADA_ROWS, d), F32).at[:dims.batch].set(c).at[dims.batch].set(c_ctx)
    mods = _adaln(dims, cvec, ada_w, ada_b)
    mods = mods.reshape(mods.shape[0], ADA_ROWS * N_ADA, 1, d)
    h = jnp.concatenate([x.reshape(dims.rows_lat, d), ctx.reshape(dims.rows_ctx, d)], axis=0)

    gb, u = _inproj(dims, h, mods[0], norm_mix_g[0], conv_w_in[0])
    h = _convout(dims, gb, u, conv_w[0], conv_w_out[0], h, mods[0])
    h = _moe(dims, h, mods[0], norm_ffn_g[0], router_w[0], expert_w_gate[0], expert_w_up[0], expert_w_down[0], True)

    qkv = _qkvproj(dims, h, mods[1], norm_mix_g[1], attn_w_qkv[0], attn_q_gain[0], attn_k_gain[0])
    o = _attention(dims, qkv, attn_sink[0])
    h = _oproj(dims, o, attn_w_o[0], h, mods[1])
    h = _moe(dims, h, mods[1], norm_ffn_g[1], router_w[1], expert_w_gate[1], expert_w_up[1], expert_w_down[1], False)
    return h.reshape(dims.batch, dims.seq, d)


def kernel(x, c, ctx, c_ctx, ada_w, ada_b, norm_mix_g, norm_ffn_g, conv_w_in, conv_w, conv_w_out, attn_w_qkv, attn_q_gain, attn_k_gain, attn_sink, attn_w_o, router_w, expert_w_gate, expert_w_up, expert_w_down):
    batch, seq, d = x.shape
    dims = Dims(d=d, batch=batch, seq=seq, grid_w=64, ctx=ctx.shape[1], heads=d // HEAD_DIM,
                kv_heads=(attn_w_qkv.shape[2] // HEAD_DIM - d // HEAD_DIM) // 2,
                experts=router_w.shape[2], cap_factor=2, d_expert=expert_w_gate.shape[3])
    return _forward(dims, x, c, ctx, c_ctx, ada_w, ada_b, norm_mix_g, norm_ffn_g, conv_w_in, conv_w, conv_w_out,
                    attn_w_qkv, attn_q_gain, attn_k_gain, attn_sink, attn_w_o, router_w,
                    expert_w_gate, expert_w_up, expert_w_down)
```

```python
import dataclasses
import functools

import jax
import jax.numpy as jnp
from jax import lax
from jax.experimental import pallas as pl
from jax.experimental.pallas import tpu as pltpu

F32 = jnp.float32
BF16 = jnp.bfloat16
I32 = jnp.int32

NORM_EPS = 1e-6
MASK_VALUE = -1e30
ROPE_THETA = 10000.0
GRID_W = 64
CAP_FACTOR = 2
HEAD_DIM = 128
ATTN_BLOCK = 128
N_ADA = 6
ADA_ROWS = 16
HI_MASK = -65536
PAIR_BLOCK = 256
VMEM_LIMIT = 58 * 1024 * 1024


@dataclasses.dataclass(frozen=True)
class Dims:
    d: int
    batch: int
    seq: int
    grid_w: int
    ctx: int
    heads: int
    kv_heads: int
    experts: int
    d_expert: int

    @property
    def dq(self):
        return self.heads * HEAD_DIM

    @property
    def dkv(self):
        return self.kv_heads * HEAD_DIM


@dataclasses.dataclass(frozen=True)
class Stream:
    batch: int
    seq: int
    shared_row: int = -1

    @property
    def rows(self):
        return self.batch * self.seq

    def tm(self, cap=1024):
        unit = self.rows if self.shared_row >= 0 else self.seq
        t = cap
        while unit % t:
            t //= 2
        return t

    def ada_row(self, tm):
        if self.shared_row >= 0:
            return lambda m: self.shared_row
        per = self.seq // tm
        assert per >= 1
        return lambda m: m // per


def _params(*sem):
    return pltpu.CompilerParams(dimension_semantics=sem, vmem_limit_bytes=VMEM_LIMIT)


def _modulate(h, gain, shift, scale):
    ms = jnp.mean(h * h, axis=-1, keepdims=True)
    xn = h * lax.rsqrt(ms + NORM_EPS)
    return (xn * gain) * (1.0 + scale) + shift


def _mod_spec(st, tm, which, d, width=None):
    row = st.ada_row(tm)
    if width is None:
        return pl.BlockSpec((1, 1, d), lambda m, n: (row(m) * N_ADA + which, 0, 0))
    return pl.BlockSpec((1, 1, width), lambda m, n: (row(m) * N_ADA + which, 0, n))


def _pack_bf16_pair(lo, hi):
    lo_bits = pltpu.bitcast(lo.astype(BF16).astype(F32), I32)
    hi_bits = pltpu.bitcast(hi.astype(BF16).astype(F32), I32)
    return (hi_bits & HI_MASK) | lax.shift_right_logical(lo_bits, 16)


def _unpack_bf16_pair(w):
    lo = pltpu.bitcast(lax.shift_left(w, 16), F32).astype(BF16)
    hi = pltpu.bitcast(w & HI_MASK, F32).astype(BF16)
    return lo, hi


def _adaln_body(c_ref, w_ref, b_ref, o_ref):
    c = c_ref[...]
    s = c * jax.nn.sigmoid(c)
    o_ref[0] = jnp.dot(s.astype(BF16), w_ref[0].astype(BF16), preferred_element_type=F32) + b_ref[0]


def _adaln(cvec, ada_w, ada_b):
    depth, d, n = ada_w.shape
    tn = min(1024, n)
    return pl.pallas_call(
        _adaln_body,
        grid=(depth, n // tn),
        in_specs=[
            pl.BlockSpec((ADA_ROWS, d), lambda l, j: (0, 0)),
            pl.BlockSpec((1, d, tn), lambda l, j: (l, 0, j)),
            pl.BlockSpec((1, 1, tn), lambda l, j: (l, 0, j)),
        ],
        out_specs=pl.BlockSpec((1, ADA_ROWS, tn), lambda l, j: (l, 0, j)),
        out_shape=jax.ShapeDtypeStruct((depth, ADA_ROWS, n), F32),
        compiler_params=_params("arbitrary", "arbitrary"),
        name="adaln",
    )(cvec, ada_w, ada_b.reshape(depth, 1, n))


def _inproj_body(h_ref, g_ref, sh_ref, sc_ref, wb_ref, wc_ref, wx_ref, gb_ref, u_ref, a_scr):
    @pl.when(pl.program_id(1) == 0)
    def _():
        a_scr[...] = _modulate(h_ref[...], g_ref[...], sh_ref[0], sc_ref[0]).astype(BF16)

    a = a_scr[...]
    gb = jnp.dot(a, wb_ref[0].astype(BF16), preferred_element_type=F32)
    gc = jnp.dot(a, wc_ref[0].astype(BF16), preferred_element_type=F32)
    xv = jnp.dot(a, wx_ref[0].astype(BF16), preferred_element_type=F32)
    gb_ref[...] = gb.astype(BF16)
    u_ref[...] = (gc * xv).astype(BF16)


def _inproj(st, h, mods, gain, w_in, layer):
    rows, d = h.shape
    tm, tn = st.tm(), 256
    nt = d // tn
    return pl.pallas_call(
        _inproj_body,
        grid=(rows // tm, nt),
        in_specs=[
            pl.BlockSpec((tm, d), lambda m, n: (m, 0)),
            pl.BlockSpec((1, d), lambda m, n: (0, 0)),
            _mod_spec(st, tm, 0, d),
            _mod_spec(st, tm, 1, d),
            pl.BlockSpec((1, d, tn), lambda m, n: (layer, 0, n)),
            pl.BlockSpec((1, d, tn), lambda m, n: (layer, 0, nt + n)),
            pl.BlockSpec((1, d, tn), lambda m, n: (layer, 0, 2 * nt + n)),
        ],
        out_specs=[pl.BlockSpec((tm, tn), lambda m, n: (m, n))] * 2,
        out_shape=[jax.ShapeDtypeStruct((rows, d), BF16)] * 2,
        scratch_shapes=[pltpu.VMEM((tm, d), BF16)],
        compiler_params=_params("arbitrary", "arbitrary"),
        name="conv_inproj",
    )(h, gain.reshape(1, d), mods, mods, w_in, w_in, w_in)


def _convout_body(gb_ref, u_ref, up_ref, un_ref, cw_ref, w_ref, h_ref, g1_ref, o_ref, v_scr, *, seq, tm):
    @pl.when(pl.program_id(1) == 0)
    def _():
        u = u_ref[...].astype(F32)
        halo = up_ref.shape[0]
        row = lax.broadcasted_iota(I32, (tm, 1), 0)
        pos = (pl.program_id(0) * tm + row) & (seq - 1)
        u_dn = pltpu.roll(u, 1, 0)
        u_dn = jnp.where(row == 0, up_ref[halo - 1:halo, :].astype(F32), u_dn)
        u_dn = jnp.where(pos == 0, 0.0, u_dn)
        u_up = pltpu.roll(u, tm - 1, 0)
        u_up = jnp.where(row == tm - 1, un_ref[0:1, :].astype(F32), u_up)
        u_up = jnp.where(pos == seq - 1, 0.0, u_up)
        y = cw_ref[0, 0:1, :] * u_dn + cw_ref[0, 1:2, :] * u + cw_ref[0, 2:3, :] * u_up
        v_scr[...] = (gb_ref[...].astype(F32) * y).astype(BF16)

    out = jnp.dot(v_scr[...], w_ref[0].astype(BF16), preferred_element_type=F32)
    o_ref[...] = h_ref[...] + g1_ref[0] * out


def _convout(st, gb, u, conv_w, w_out, h, mods, layer):
    rows, d = h.shape
    tm, tn, halo = st.tm(), 512, 16
    assert st.seq & (st.seq - 1) == 0
    per = tm // halo
    last = rows // halo - 1
    return pl.pallas_call(
        functools.partial(_convout_body, seq=st.seq, tm=tm),
        grid=(rows // tm, d // tn),
        in_specs=[
            pl.BlockSpec((tm, d), lambda m, n: (m, 0)),
            pl.BlockSpec((tm, d), lambda m, n: (m, 0)),
            pl.BlockSpec((halo, d), lambda m, n: (jnp.maximum(m * per - 1, 0), 0)),
            pl.BlockSpec((halo, d), lambda m, n: (jnp.minimum((m + 1) * per, last), 0)),
            pl.BlockSpec((1, 3, d), lambda m, n: (layer, 0, 0)),
            pl.BlockSpec((1, d, tn), lambda m, n: (layer, 0, n)),
            pl.BlockSpec((tm, tn), lambda m, n: (m, n)),
            _mod_spec(st, tm, 2, d, width=tn),
        ],
        out_specs=pl.BlockSpec((tm, tn), lambda m, n: (m, n)),
        out_shape=jax.ShapeDtypeStruct((rows, d), F32),
        scratch_shapes=[pltpu.VMEM((tm, d), BF16)],
        compiler_params=_params("arbitrary", "arbitrary"),
        name="conv_outproj",
    )(gb, u, u, u, conv_w, w_out, h, mods)


def _qkv_body(h_ref, g_ref, sh_ref, sc_ref, w_ref, qg_ref, kg_ref, cos_ref, sin_ref, o_ref, a_scr, *, n_q, tn):
    n = pl.program_id(1)

    @pl.when(n == 0)
    def _():
        a_scr[...] = _modulate(h_ref[...], g_ref[...], sh_ref[0], sc_ref[0]).astype(BF16)

    acc = jnp.dot(a_scr[...], w_ref[0].astype(BF16), preferred_element_type=F32)

    def norm_rope(gain, scale):
        cos = cos_ref[...]
        sin = sin_ref[...]
        lane = lax.broadcasted_iota(I32, (1, HEAD_DIM), 1)
        first = (lane % (HEAD_DIM // 2)) < (HEAD_DIM // 4)
        outs = []
        for hd in range(tn // HEAD_DIM):
            x = acc[:, hd * HEAD_DIM:(hd + 1) * HEAD_DIM]
            ms = jnp.mean(x * x, axis=-1, keepdims=True)
            xn = x * lax.rsqrt(ms + NORM_EPS) * gain
            rot = jnp.where(first, pltpu.roll(xn, HEAD_DIM - HEAD_DIM // 4, 1), pltpu.roll(xn, HEAD_DIM // 4, 1))
            outs.append(((xn * cos + rot * sin) * scale).astype(BF16))
        return outs[0] if len(outs) == 1 else jnp.concatenate(outs, axis=1)

    @pl.when(n < n_q)
    def _():
        o_ref[...] = norm_rope(qg_ref[...], HEAD_DIM ** -0.5)

    @pl.when(n == n_q)
    def _():
        o_ref[...] = norm_rope(kg_ref[...], 1.0)

    @pl.when(n > n_q)
    def _():
        o_ref[...] = acc.astype(BF16)


def _rope_tables(dims, tm):
    half = HEAD_DIM // 4
    inv_freq = ROPE_THETA ** (-jnp.arange(half, dtype=F32) * 2.0 / (HEAD_DIM // 2))
    t = jnp.arange(dims.seq)
    ang_row = (t // dims.grid_w).astype(F32)[:, None] * inv_freq
    ang_col = (t % dims.grid_w).astype(F32)[:, None] * inv_freq
    cos = jnp.concatenate([jnp.cos(ang_row)] * 2 + [jnp.cos(ang_col)] * 2, axis=-1)
    sin = jnp.concatenate([-jnp.sin(ang_row), jnp.sin(ang_row), -jnp.sin(ang_col), jnp.sin(ang_col)], axis=-1)
    cos = jnp.concatenate([cos, jnp.ones((tm, HEAD_DIM), F32)], axis=0)
    sin = jnp.concatenate([sin, jnp.zeros((tm, HEAD_DIM), F32)], axis=0)
    return cos, sin


def _qkvproj(dims, st, h, mods, gain, w_qkv, q_gain, k_gain, layer, with_q):
    rows, d = h.shape
    tm, tn = st.tm(), dims.dkv
    n_q = dims.dq // tn if with_q else 0
    col0 = 0 if with_q else dims.dq // tn
    n_total = n_q + 2
    cos, sin = _rope_tables(dims, tm)
    per = dims.seq // tm
    pos_tile = (lambda m, n: (m % per, 0)) if with_q else (lambda m, n: (per, 0))
    return pl.pallas_call(
        functools.partial(_qkv_body, n_q=n_q, tn=tn),
        grid=(rows // tm, n_total),
        in_specs=[
            pl.BlockSpec((tm, d), lambda m, n: (m, 0)),
            pl.BlockSpec((1, d), lambda m, n: (0, 0)),
            _mod_spec(st, tm, 0, d),
            _mod_spec(st, tm, 1, d),
            pl.BlockSpec((1, d, tn), lambda m, n: (layer, 0, col0 + n)),
            pl.BlockSpec((1, HEAD_DIM), lambda m, n: (0, 0)),
            pl.BlockSpec((1, HEAD_DIM), lambda m, n: (0, 0)),
            pl.BlockSpec((tm, HEAD_DIM), pos_tile),
            pl.BlockSpec((tm, HEAD_DIM), pos_tile),
        ],
        out_specs=pl.BlockSpec((tm, tn), lambda m, n: (m, n)),
        out_shape=jax.ShapeDtypeStruct((rows, n_total * tn), BF16),
        scratch_shapes=[pltpu.VMEM((tm, d), BF16)],
        compiler_params=_params("arbitrary", "arbitrary"),
        name="attn_qkv",
    )(h, gain.reshape(1, d), mods, mods, w_qkv, q_gain.reshape(1, HEAD_DIM), k_gain.reshape(1, HEAD_DIM), cos, sin)


def _attn_body(sink_ref, q_ref, k_ref, v_ref, kc_ref, vc_ref, o_ref, *, dims):
    group = dims.heads // dims.kv_heads
    blk = ATTN_BLOCK
    band = 3 * blk
    n_ctx = dims.ctx
    head0 = pl.program_id(1) * group
    kc = kc_ref[...]
    vc = vc_ref[...]

    def block(n, carry):
        q0 = pl.multiple_of(n * blk, blk)
        start = pl.multiple_of(jnp.clip((n - 1) * blk, 0, dims.seq - band), blk)
        kcat = jnp.concatenate([k_ref[pl.ds(start, band), :], kc], axis=0)
        vcat = jnp.concatenate([v_ref[pl.ds(start, band), :], vc], axis=0)
        q_pos = q0 + lax.broadcasted_iota(I32, (blk, 1), 0)
        col = lax.broadcasted_iota(I32, (1, band + n_ctx), 1)
        valid = (col >= band) | (jnp.abs(q_pos - (start + col)) <= blk)
        for g in range(group):
            q = q_ref[pl.ds(q0, blk), g * HEAD_DIM:(g + 1) * HEAD_DIM]
            s = lax.dot_general(q, kcat, (((1,), (1,)), ((), ())), preferred_element_type=F32)
            s = jnp.where(valid, s, MASK_VALUE)
            sink = sink_ref[head0 + g]
            m = jnp.maximum(jnp.max(s, axis=-1, keepdims=True), sink)
            p = jnp.exp(s - m)
            den = jnp.sum(p, axis=-1, keepdims=True) + jnp.exp(sink - m)
            o = jnp.dot(p.astype(BF16), vcat, preferred_element_type=F32) / den
            o_ref[pl.ds(q0, blk), g * HEAD_DIM:(g + 1) * HEAD_DIM] = o.astype(BF16)
        return carry

    lax.fori_loop(0, dims.seq // blk, block, 0)


def _attention(dims, qkv, kv_ctx, sink):
    group = dims.heads // dims.kv_heads
    gw = group * HEAD_DIM
    k_col = dims.dq // HEAD_DIM
    v_col = (dims.dq + dims.dkv) // HEAD_DIM
    return pl.pallas_call(
        functools.partial(_attn_body, dims=dims),
        grid=(dims.batch, dims.kv_heads),
        in_specs=[
            pl.BlockSpec(memory_space=pltpu.SMEM),
            pl.BlockSpec((dims.seq, gw), lambda b, h: (b, h)),
            pl.BlockSpec((dims.seq, HEAD_DIM), lambda b, h: (b, k_col + h)),
            pl.BlockSpec((dims.seq, HEAD_DIM), lambda b, h: (b, v_col + h)),
            pl.BlockSpec((dims.ctx, HEAD_DIM), lambda b, h: (b, h)),
            pl.BlockSpec((dims.ctx, HEAD_DIM), lambda b, h: (b, dims.kv_heads + h)),
        ],
        out_specs=pl.BlockSpec((dims.seq, gw), lambda b, h: (b, h)),
        out_shape=jax.ShapeDtypeStruct((dims.batch * dims.seq, dims.dq), BF16),
        compiler_params=_params("arbitrary", "arbitrary"),
        name="attn_core",
    )(sink, qkv, qkv, qkv, kv_ctx, kv_ctx)


def _oproj_body(o_ref, w_ref, h_ref, g1_ref, out_ref):
    y = jnp.dot(o_ref[...], w_ref[0].astype(BF16), preferred_element_type=F32)
    out_ref[...] = h_ref[...] + g1_ref[0] * y


def _oproj(st, o, w_o, h, mods, layer):
    rows, dq = o.shape
    d = w_o.shape[2]
    tm, tn = st.tm(), 512
    return pl.pallas_call(
        _oproj_body,
        grid=(rows // tm, d // tn),
        in_specs=[
            pl.BlockSpec((tm, dq), lambda m, n: (m, 0)),
            pl.BlockSpec((1, dq, tn), lambda m, n: (layer, 0, n)),
            pl.BlockSpec((tm, tn), lambda m, n: (m, n)),
            _mod_spec(st, tm, 2, d, width=tn),
        ],
        out_specs=pl.BlockSpec((tm, tn), lambda m, n: (m, n)),
        out_shape=jax.ShapeDtypeStruct((rows, d), F32),
        compiler_params=_params("arbitrary", "arbitrary"),
        name="attn_oproj",
    )(o, w_o, h, mods)


def _router_body(h_ref, g_ref, sh_ref, sc_ref, wr_ref, h2_ref, aff_ref, *, experts):
    a = _modulate(h_ref[...], g_ref[...], sh_ref[0], sc_ref[0])
    half = a.shape[1] // 2
    h2_ref[...] = _pack_bf16_pair(a[:, :half], a[:, half:])
    logits = jnp.dot(a.astype(BF16), wr_ref[...], preferred_element_type=F32)
    lt = logits.T[0:experts, :]
    ex = jnp.exp(lt - jnp.max(lt, axis=0, keepdims=True))
    aff_ref[...] = ex / jnp.sum(ex, axis=0, keepdims=True)


def _router(st, h, mods, gain, w_router):
    rows, d = h.shape
    experts = w_router.shape[1]
    tm = st.tm(512)
    wr = jnp.zeros((d, 128), BF16).at[:, :experts].set(w_router.astype(BF16))
    return pl.pallas_call(
        functools.partial(_router_body, experts=experts),
        grid=(rows // tm, 1),
        in_specs=[
            pl.BlockSpec((tm, d), lambda m, n: (m, 0)),
            pl.BlockSpec((1, d), lambda m, n: (0, 0)),
            _mod_spec(st, tm, 3, d),
            _mod_spec(st, tm, 4, d),
            pl.BlockSpec((d, 128), lambda m, n: (0, 0)),
        ],
        out_specs=[
            pl.BlockSpec((tm, d // 2), lambda m, n: (m, 0)),
            pl.BlockSpec((experts, tm), lambda m, n: (0, m)),
        ],
        out_shape=[
            jax.ShapeDtypeStruct((rows, d // 2), I32),
            jax.ShapeDtypeStruct((experts, rows), F32),
        ],
        compiler_params=_params("arbitrary", "arbitrary"),
        name="moe_router",
    )(h, gain.reshape(1, d), mods, mods, wr)


def _lane_prefix(x, tri):
    n = x.shape[1]
    off = jnp.zeros((x.shape[0], 1), F32)
    chunks = []
    for ch in range(n // 128):
        xc = x[:, ch * 128:(ch + 1) * 128]
        incl = jnp.dot(xc.astype(BF16), tri, preferred_element_type=F32)
        chunks.append(incl - xc + off)
        off = off + incl[:, 127:128]
    return chunks[0] if len(chunks) == 1 else jnp.concatenate(chunks, axis=1)


def _topk_body(aff_ref, idx_ref, dst_ref, gate_ref, off_ref, cnt_ref, pos_scr, sel_scr, dstv_scr, *,
               n_tok, cap, pair0, experts):
    grp = pl.program_id(0)
    a = aff_ref[...]
    bits = pltpu.bitcast(a, I32)
    lane = lax.broadcasted_iota(I32, (experts, n_tok), 1)

    def count(mask):
        return jnp.sum(mask.astype(I32), axis=1, keepdims=True)

    def thr_step(i, thr):
        cand = thr | jnp.left_shift(jnp.int32(1), 30 - i)
        return jnp.where(count(bits >= cand) >= cap, cand, thr)

    thr = lax.fori_loop(0, 31, thr_step, jnp.zeros((experts, 1), I32))
    above = bits > thr
    tie = bits == thr
    need = cap - count(above)

    top_bit = n_tok.bit_length() - 2

    def tie_step(i, lim):
        cand = lim | jnp.left_shift(jnp.int32(1), top_bit - i)
        return jnp.where(count(tie & (lane < cand)) < need, cand, lim)

    lim = lax.fori_loop(0, top_bit + 1, tie_step, jnp.zeros((experts, 1), I32))
    sel = above | (tie & (lane <= lim))
    sel_f = sel.astype(F32)
    sel_i = sel.astype(I32)

    r = lax.broadcasted_iota(I32, (128, 128), 0)
    c = lax.broadcasted_iota(I32, (128, 128), 1)
    tri = (r <= c).astype(F32).astype(BF16)
    pos_scr[...] = _lane_prefix(sel_f, tri).astype(I32)
    sel_scr[...] = sel_i

    level = jnp.zeros((1, n_tok), I32)
    levels = []
    for e in range(experts):
        levels.append(level)
        level = level + sel_i[e:e + 1, :]
    cnt = level
    off = _lane_prefix(cnt.astype(F32), tri).astype(I32) + (pair0 + grp * (experts * cap))
    off_ref[...] = off
    cnt_ref[...] = cnt
    for e in range(experts):
        dstv_scr[e:e + 1, :] = off + levels[e]

    sc = min(cap, 64)
    tok = lax.broadcasted_iota(I32, (1, n_tok), 1)
    slot_iota = lax.broadcasted_iota(I32, (sc, 1), 0)

    def per_expert(e, carry):
        pe = pos_scr[pl.ds(e, 1), :]
        se = sel_scr[pl.ds(e, 1), :] > 0
        ae = aff_ref[pl.ds(e, 1), :]
        de = dstv_scr[pl.ds(e, 1), :]
        for s0 in range(0, cap, sc):
            hit = (pe == (slot_iota + s0)) & se
            idx_ref[0, e, s0:s0 + sc, :] = (jnp.sum(jnp.where(hit, tok, 0), axis=1, keepdims=True)
                                             + grp * n_tok)
            dst_ref[0, e, s0:s0 + sc, :] = jnp.sum(jnp.where(hit, de, 0), axis=1, keepdims=True)
            gate_ref[0, e, s0:s0 + sc, :] = jnp.sum(jnp.where(hit, ae, 0.0), axis=1, keepdims=True)
        return carry

    lax.fori_loop(0, experts, per_expert, 0)


def _topk(st, aff, cap, pair0):
    e = aff.shape[0]
    n_tok, n_groups = st.seq, st.batch
    slot_shape = jax.ShapeDtypeStruct((n_groups, e, cap, 1), I32)
    slot_spec = pl.BlockSpec((1, e, cap, 1), lambda g: (g, 0, 0, 0))
    return pl.pallas_call(
        functools.partial(_topk_body, n_tok=n_tok, cap=cap, pair0=pair0, experts=e),
        grid=(n_groups,),
        in_specs=[pl.BlockSpec((e, n_tok), lambda g: (0, g))],
        out_specs=[slot_spec, slot_spec, slot_spec,
                   pl.BlockSpec((1, n_tok), lambda g: (0, g)),
                   pl.BlockSpec((1, n_tok), lambda g: (0, g))],
        out_shape=[slot_shape, slot_shape, jax.ShapeDtypeStruct((n_groups, e, cap, 1), F32),
                   jax.ShapeDtypeStruct((1, n_groups * n_tok), I32),
                   jax.ShapeDtypeStruct((1, n_groups * n_tok), I32)],
        scratch_shapes=[pltpu.VMEM((e, n_tok), I32)] * 3,
        compiler_params=_params("arbitrary"),
        name="moe_topk",
    )(aff)


def _expert_body(*refs, n_src, src_slots, slots, rc, n_f, n_n, tf, n_e):
    idx_ref, idx_nxt_ref, dst_ref, dst_prv_ref = refs[0:4]
    srcs = refs[4:4 + n_src]
    gate_ref, wg_ref, wu_ref, wd_ref, pairs_hbm, x_scr, y_scr, mid_scr, gsem, ssem = refs[4 + n_src:]
    e = pl.program_id(0)
    s = pl.program_id(1)
    half = x_scr.shape[1]
    tnw = y_scr.shape[1] // n_n

    def gather(ref, lo, n):
        def one(i, carry):
            r = lo + i
            for (s0, s1), src in zip(src_slots, srcs):
                @pl.when((r >= s0) & (r < s1))
                def _():
                    pltpu.make_async_copy(src.at[pl.ds(ref[0, 0, r], 1), :], x_scr.at[pl.ds(r, 1), :], gsem).start()
            return carry
        lax.fori_loop(0, n, one, 0, unroll=8)

    def scatter(ref, lo, n):
        def one(i, carry):
            r = lo + i
            pltpu.make_async_copy(y_scr.at[pl.ds(r, 1), :], pairs_hbm.at[pl.ds(ref[0, 0, r], 1), :], ssem).start()
            return carry
        lax.fori_loop(0, n, one, 0, unroll=8)

    def wait_gather():
        pltpu.make_async_copy(srcs[0].at[pl.ds(0, slots), :], x_scr, gsem).wait()

    def wait_scatter():
        pltpu.make_async_copy(y_scr, pairs_hbm.at[pl.ds(0, slots), :], ssem).wait()

    @pl.when((e == 0) & (s == 0))
    def _():
        y_scr[...] = jnp.zeros_like(y_scr)
        gather(idx_ref, 0, slots)

    @pl.when(s == 0)
    def _():
        wait_gather()

    @pl.when(s < n_f)
    def _():
        wg = wg_ref[0, 0].astype(BF16)
        wu = wu_ref[0, 0].astype(BF16)
        for r0 in range(0, slots, rc):
            lo, hi = _unpack_bf16_pair(x_scr[r0:r0 + rc, :])
            a = (jnp.dot(lo, wg[:half], preferred_element_type=F32)
                 + jnp.dot(hi, wg[half:], preferred_element_type=F32))
            u = (jnp.dot(lo, wu[:half], preferred_element_type=F32)
                 + jnp.dot(hi, wu[half:], preferred_element_type=F32))
            mid_scr[s, r0:r0 + rc, :] = (a * jax.nn.sigmoid(a) * u).astype(BF16)
        scatter(dst_prv_ref, s * (slots // n_f), slots // n_f)

    @pl.when(s == n_f)
    def _():
        wait_scatter()

    for j in range(n_n):
        @pl.when(s == n_f + j)
        def _():
            wd = wd_ref[0, 0].astype(BF16)
            for r0 in range(0, slots, rc):
                y = jnp.dot(mid_scr[0, r0:r0 + rc, :], wd[0:tf], preferred_element_type=F32)
                for f in range(1, n_f):
                    y += jnp.dot(mid_scr[f, r0:r0 + rc, :], wd[f * tf:(f + 1) * tf], preferred_element_type=F32)
                y = y * gate_ref[0, r0:r0 + rc, :]
                y_scr[r0:r0 + rc, j * tnw:(j + 1) * tnw] = _pack_bf16_pair(y[:, :tnw], y[:, tnw:])
            gather(idx_nxt_ref, j * (slots // n_n), slots // n_n)

    @pl.when((e == n_e - 1) & (s == n_f + n_n - 1))
    def _():
        wait_gather()
        scatter(dst_ref, 0, slots)
        wait_scatter()


def _experts(idx, dst, gate, sources, src_slots, w_gate, w_up, w_down, layer, n_pairs):
    n_e, slots = idx.shape
    d, fdim = w_gate.shape[2], w_gate.shape[3]
    tf, tn = 256, 512
    n_f, n_n = fdim // tf, d // tn
    rc = slots // 4
    assert slots % (8 * n_f) == 0 and slots % (8 * n_n) == 0 and rc % 16 == 0
    idx3 = idx.reshape(n_e, 1, slots)
    dst3 = dst.reshape(n_e, 1, slots)
    smem = lambda f: pl.BlockSpec((1, 1, slots), f, memory_space=pltpu.SMEM)
    up_chunk = lambda i, s: (layer, i, 0, jnp.minimum(s, n_f - 1))
    return pl.pallas_call(
        functools.partial(_expert_body, n_src=len(sources), src_slots=src_slots, slots=slots, rc=rc,
                          n_f=n_f, n_n=n_n, tf=tf, n_e=n_e),
        grid=(n_e, n_f + n_n),
        in_specs=[
            smem(lambda i, s: (i, 0, 0)),
            smem(lambda i, s: (jnp.minimum(i + 1, n_e - 1), 0, 0)),
            smem(lambda i, s: (i, 0, 0)),
            smem(lambda i, s: (jnp.maximum(i - 1, 0), 0, 0)),
        ] + [pl.BlockSpec(memory_space=pl.ANY)] * len(sources) + [
            pl.BlockSpec((1, slots, 1), lambda i, s: (i, 0, 0)),
            pl.BlockSpec((1, 1, d, tf), up_chunk),
            pl.BlockSpec((1, 1, d, tf), up_chunk),
            pl.BlockSpec((1, 1, fdim, tn), lambda i, s: (layer, i, 0, jnp.maximum(s - n_f, 0))),
        ],
        out_specs=pl.BlockSpec(memory_space=pl.ANY),
        out_shape=jax.ShapeDtypeStruct((n_pairs, d // 2), I32),
        scratch_shapes=[
            pltpu.VMEM((slots, d // 2), I32),
            pltpu.VMEM((slots, d // 2), I32),
            pltpu.VMEM((n_f, slots, tf), BF16),
            pltpu.SemaphoreType.DMA(()),
            pltpu.SemaphoreType.DMA(()),
        ],
        compiler_params=_params("arbitrary", "arbitrary"),
        name="moe_experts",
    )(idx3, idx3, dst3, dst3, *sources, gate, w_gate, w_up, w_down)


def _combine_body(lo_ref, hi_ref, off_ref, cnt_ref, p_ref, h_ref, g2_ref, o_ref, *, tt, n_tiles, pair0, pairs_per_group):
    base = pair0 + pl.program_id(0) * pairs_per_group
    tnw = p_ref.shape[1]
    lane = lax.broadcasted_iota(I32, (1, PAIR_BLOCK), 1)
    for i in range(n_tiles):
        rows = slice(i * tt, (i + 1) * tt)
        first = off_ref[rows, :] - base
        last = first + cnt_ref[rows, :]
        k0 = lax.shift_right_logical(lo_ref[0, 0, i] - base, PAIR_BLOCK.bit_length() - 1)
        k1 = lax.shift_right_logical(hi_ref[0, 0, i] - base + (PAIR_BLOCK - 1), PAIR_BLOCK.bit_length() - 1)

        def step(k, acc):
            p0 = pl.multiple_of(k * PAIR_BLOCK, PAIR_BLOCK)
            lo, hi = _unpack_bf16_pair(p_ref[pl.ds(p0, PAIR_BLOCK), :])
            pr = p0 + lane
            seg = ((pr >= first) & (pr < last)).astype(F32).astype(BF16)
            return (acc[0] + jnp.dot(seg, lo, preferred_element_type=F32),
                    acc[1] + jnp.dot(seg, hi, preferred_element_type=F32))

        zero = jnp.zeros((tt, tnw), F32)
        acc_lo, acc_hi = lax.fori_loop(k0, k1, step, (zero, zero))
        o_ref[rows, 0:tnw] = h_ref[rows, 0:tnw] + g2_ref[0, :, 0:tnw] * acc_lo
        o_ref[rows, tnw:2 * tnw] = h_ref[rows, tnw:2 * tnw] + g2_ref[0, :, tnw:2 * tnw] * acc_hi


def _combine(st, off, cnt, pairs, h, mods, pair0, pairs_per_group):
    rows, d = h.shape
    n_tok = st.seq
    tt = min(256, n_tok)
    n_tiles = n_tok // tt
    tn = 512
    assert pairs_per_group % PAIR_BLOCK == 0 and pair0 % pairs_per_group == 0
    off_t = off.reshape(st.batch, 1, n_tiles, tt)
    cnt_t = cnt.reshape(st.batch, 1, n_tiles, tt)
    tile_lo = off_t[..., 0]
    tile_hi = off_t[..., tt - 1] + cnt_t[..., tt - 1]
    row = st.ada_row(n_tok)
    smem = pl.BlockSpec((1, 1, n_tiles), lambda b, n: (b, 0, 0), memory_space=pltpu.SMEM)
    return pl.pallas_call(
        functools.partial(_combine_body, tt=tt, n_tiles=n_tiles, pair0=pair0, pairs_per_group=pairs_per_group),
        grid=(st.batch, d // tn),
        in_specs=[
            smem, smem,
            pl.BlockSpec((n_tok, 1), lambda b, n: (b, 0)),
            pl.BlockSpec((n_tok, 1), lambda b, n: (b, 0)),
            pl.BlockSpec((pairs_per_group, tn // 2), lambda b, n: (pair0 // pairs_per_group + b, n)),
            pl.BlockSpec((n_tok, tn), lambda b, n: (b, n)),
            pl.BlockSpec((1, 1, tn), lambda b, n: (row(b) * N_ADA + 5, 0, n)),
        ],
        out_specs=pl.BlockSpec((n_tok, tn), lambda b, n: (b, n)),
        out_shape=jax.ShapeDtypeStruct((rows, d), F32),
        compiler_params=_params("arbitrary", "arbitrary"),
        name="moe_combine",
    )(tile_lo, tile_hi, off.reshape(rows, 1), cnt.reshape(rows, 1), pairs, h, mods)


def _moe(streams, hs, mods, gain, w_router, w_gate, w_up, w_down, layer):
    n_e = w_router.shape[2]
    routed = []
    pair0 = 0
    slot0 = 0
    src_slots = []
    for st, h in zip(streams, hs):
        cap = CAP_FACTOR * st.seq // n_e
        h2, aff = _router(st, h, mods, gain, w_router[layer])
        idx, dst, gate, off, cnt = _topk(st, aff, cap, pair0)
        routed.append((h2, idx, dst, gate, off, cnt, pair0, n_e * cap))
        src_slots.append((slot0, slot0 + st.batch * cap))
        pair0 += st.batch * n_e * cap
        slot0 += st.batch * cap
    flat = lambda t: t[..., 0].transpose(1, 0, 2).reshape(n_e, -1)
    idx = jnp.concatenate([flat(r[1]) for r in routed], axis=1)
    dst = jnp.concatenate([flat(r[2]) for r in routed], axis=1)
    gate = jnp.concatenate([flat(r[3]) for r in routed], axis=1)[..., None]
    pairs = _experts(idx, dst, gate, [r[0] for r in routed], tuple(src_slots), w_gate, w_up, w_down, layer, pair0)
    return [_combine(st, r[4], r[5], pairs, h, mods, r[6], r[7]) for st, h, r in zip(streams, hs, routed)]


def _forward(dims, x, c, ctx, c_ctx, ada_w, ada_b, norm_mix_g, norm_ffn_g, conv_w_in, conv_w, conv_w_out,
             attn_w_qkv, attn_q_gain, attn_k_gain, attn_sink, attn_w_o, router_w,
             expert_w_gate, expert_w_up, expert_w_down):
    d = dims.d
    lat = Stream(dims.batch, dims.seq)
    con = Stream(dims.batch, dims.ctx, shared_row=dims.batch)
    cvec = jnp.zeros((ADA_ROWS, d), F32).at[:dims.batch].set(c).at[dims.batch].set(c_ctx)
    mods = _adaln(cvec, ada_w, ada_b)
    mods = mods.reshape(mods.shape[0], ADA_ROWS * N_ADA, 1, d)
    hs = [x.reshape(lat.rows, d), ctx.reshape(con.rows, d)]

    mixed = []
    for st, h in zip((lat, con), hs):
        gb, u = _inproj(st, h, mods[0], norm_mix_g[0], conv_w_in, 0)
        mixed.append(_convout(st, gb, u, conv_w, conv_w_out, h, mods[0], 0))
    h_lat, h_ctx = _moe((lat, con), mixed, mods[0], norm_ffn_g[0], router_w,
                        expert_w_gate, expert_w_up, expert_w_down, 0)

    qkv = _qkvproj(dims, lat, h_lat, mods[1], norm_mix_g[1], attn_w_qkv, attn_q_gain[0], attn_k_gain[0], 0, True)
    kv_ctx = _qkvproj(dims, con, h_ctx, mods[1], norm_mix_g[1], attn_w_qkv, attn_q_gain[0], attn_k_gain[0], 0, False)
    o = _attention(dims, qkv, kv_ctx, attn_sink[0])
    h_lat = _oproj(lat, o, attn_w_o, h_lat, mods[1], 0)
    (h_lat,) = _moe((lat,), [h_lat], mods[1], norm_ffn_g[1], router_w,
                    expert_w_gate, expert_w_up, expert_w_down, 1)
    return h_lat.reshape(dims.batch, dims.seq, d)


def kernel(x, c, ctx, c_ctx, ada_w, ada_b, norm_mix_g, norm_ffn_g, conv_w_in, conv_w, conv_w_out, attn_w_qkv, attn_q_gain, attn_k_gain, attn_sink, attn_w_o, router_w, expert_w_gate, expert_w_up, expert_w_down):
    batch, seq, d = x.shape
    dims = Dims(d=d, batch=batch, seq=seq, grid_w=GRID_W, ctx=ctx.shape[1], heads=d // HEAD_DIM,
                kv_heads=(attn_w_qkv.shape[2] // HEAD_DIM - d // HEAD_DIM) // 2,
                experts=router_w.shape[2], d_expert=expert_w_gate.shape[3])
    return _forward(dims, x, c, ctx, c_ctx, ada_w, ada_b, norm_mix_g, norm_ffn_g, conv_w_in, conv_w, conv_w_out,
                    attn_w_qkv, attn_q_gain, attn_k_gain, attn_sink, attn_w_o, router_w,
                    expert_w_gate, expert_w_up, expert_w_down)
```

```python
import dataclasses
import functools

import jax
import jax.numpy as jnp
from jax import lax
from jax.experimental import pallas as pl
from jax.experimental.pallas import tpu as pltpu

F32 = jnp.float32
BF16 = jnp.bfloat16
I32 = jnp.int32

NORM_EPS = 1e-6
MASK_VALUE = -1e30
ROPE_THETA = 10000.0
GRID_W = 64
CAP_FACTOR = 2
HEAD_DIM = 128
ATTN_BLOCK = 128
N_ADA = 6
ADA_ROWS = 16
HI_MASK = -65536
PAIR_BLOCK = 256
EXPERT_DOWN_TILE = 512
VMEM_LIMIT = 58 * 1024 * 1024


@dataclasses.dataclass(frozen=True)
class Dims:
    d: int
    batch: int
    seq: int
    grid_w: int
    ctx: int
    heads: int
    kv_heads: int
    experts: int
    d_expert: int

    @property
    def dq(self):
        return self.heads * HEAD_DIM

    @property
    def dkv(self):
        return self.kv_heads * HEAD_DIM


@dataclasses.dataclass(frozen=True)
class Stream:
    batch: int
    seq: int
    shared_row: int = -1

    @property
    def rows(self):
        return self.batch * self.seq

    def tm(self, cap=1024):
        unit = self.rows if self.shared_row >= 0 else self.seq
        t = cap
        while unit % t:
            t //= 2
        return t

    def ada_row(self, tm):
        if self.shared_row >= 0:
            return lambda m: self.shared_row
        per = self.seq // tm
        assert per >= 1
        return lambda m: m // per


def _params(*sem):
    return pltpu.CompilerParams(dimension_semantics=sem, vmem_limit_bytes=VMEM_LIMIT)


def _modulate(h, gain, shift, scale):
    ms = jnp.mean(h * h, axis=-1, keepdims=True)
    xn = h * lax.rsqrt(ms + NORM_EPS)
    return (xn * gain) * (1.0 + scale) + shift


def _mod_spec(st, tm, which, d, width=None):
    row = st.ada_row(tm)
    if width is None:
        return pl.BlockSpec((1, 1, d), lambda m, n: (row(m) * N_ADA + which, 0, 0))
    return pl.BlockSpec((1, 1, width), lambda m, n: (row(m) * N_ADA + which, 0, n))


def _pack_bf16_pair(lo, hi):
    lo_bits = pltpu.bitcast(lo.astype(BF16).astype(F32), I32)
    hi_bits = pltpu.bitcast(hi.astype(BF16).astype(F32), I32)
    return (hi_bits & HI_MASK) | lax.shift_right_logical(lo_bits, 16)


def _unpack_bf16_pair(w):
    lo = pltpu.bitcast(lax.shift_left(w, 16), F32).astype(BF16)
    hi = pltpu.bitcast(w & HI_MASK, F32).astype(BF16)
    return lo, hi


def _adaln_body(c_ref, w_ref, b_ref, o_ref):
    c = c_ref[...]
    s = c * jax.nn.sigmoid(c)
    o_ref[0] = jnp.dot(s.astype(BF16), w_ref[0].astype(BF16), preferred_element_type=F32) + b_ref[0]


def _adaln(cvec, ada_w, ada_b):
    depth, d, n = ada_w.shape
    tn = min(1024, n)
    return pl.pallas_call(
        _adaln_body,
        grid=(depth, n // tn),
        in_specs=[
            pl.BlockSpec((ADA_ROWS, d), lambda l, j: (0, 0)),
            pl.BlockSpec((1, d, tn), lambda l, j: (l, 0, j)),
            pl.BlockSpec((1, 1, tn), lambda l, j: (l, 0, j)),
        ],
        out_specs=pl.BlockSpec((1, ADA_ROWS, tn), lambda l, j: (l, 0, j)),
        out_shape=jax.ShapeDtypeStruct((depth, ADA_ROWS, n), F32),
        compiler_params=_params("arbitrary", "arbitrary"),
        name="adaln",
    )(cvec, ada_w, ada_b.reshape(depth, 1, n))


def _inproj_body(h_ref, g_ref, sh_ref, sc_ref, wb_ref, wc_ref, wx_ref, gb_ref, u_ref, a_scr):
    @pl.when(pl.program_id(1) == 0)
    def _():
        a_scr[...] = _modulate(h_ref[...], g_ref[...], sh_ref[0], sc_ref[0]).astype(BF16)

    a = a_scr[...]
    gb = jnp.dot(a, wb_ref[...], preferred_element_type=F32)
    gc = jnp.dot(a, wc_ref[...], preferred_element_type=F32)
    xv = jnp.dot(a, wx_ref[...], preferred_element_type=F32)
    gb_ref[...] = gb.astype(BF16)
    u_ref[...] = (gc * xv).astype(BF16)


def _inproj(st, h, mods, gain, w_in):
    rows, d = h.shape
    tm, tn = st.tm(), 256
    nt = d // tn
    return pl.pallas_call(
        _inproj_body,
        grid=(rows // tm, nt),
        in_specs=[
            pl.BlockSpec((tm, d), lambda m, n: (m, 0)),
            pl.BlockSpec((1, d), lambda m, n: (0, 0)),
            _mod_spec(st, tm, 0, d),
            _mod_spec(st, tm, 1, d),
            pl.BlockSpec((d, tn), lambda m, n: (0, n)),
            pl.BlockSpec((d, tn), lambda m, n: (0, nt + n)),
            pl.BlockSpec((d, tn), lambda m, n: (0, 2 * nt + n)),
        ],
        out_specs=[pl.BlockSpec((tm, tn), lambda m, n: (m, n))] * 2,
        out_shape=[jax.ShapeDtypeStruct((rows, d), BF16)] * 2,
        scratch_shapes=[pltpu.VMEM((tm, d), BF16)],
        compiler_params=_params("arbitrary", "arbitrary"),
        name="conv_inproj",
    )(h, gain.reshape(1, d), mods, mods, w_in, w_in, w_in)


def _convout_body(gb_ref, u_ref, up_ref, un_ref, cw_ref, w_ref, h_ref, g1_ref, o_ref, v_scr, *, seq, tm):
    @pl.when(pl.program_id(1) == 0)
    def _():
        u = u_ref[...].astype(F32)
        halo = up_ref.shape[0]
        row = lax.broadcasted_iota(I32, (tm, 1), 0)
        pos = (pl.program_id(0) * tm + row) & (seq - 1)
        u_dn = pltpu.roll(u, 1, 0)
        u_dn = jnp.where(row == 0, up_ref[halo - 1:halo, :].astype(F32), u_dn)
        u_dn = jnp.where(pos == 0, 0.0, u_dn)
        u_up = pltpu.roll(u, tm - 1, 0)
        u_up = jnp.where(row == tm - 1, un_ref[0:1, :].astype(F32), u_up)
        u_up = jnp.where(pos == seq - 1, 0.0, u_up)
        y = cw_ref[0:1, :] * u_dn + cw_ref[1:2, :] * u + cw_ref[2:3, :] * u_up
        v_scr[...] = (gb_ref[...].astype(F32) * y).astype(BF16)

    out = jnp.dot(v_scr[...], w_ref[...], preferred_element_type=F32)
    o_ref[...] = h_ref[...] + g1_ref[0] * out


def _convout(st, gb, u, conv_w, w_out, h, mods):
    rows, d = h.shape
    tm, tn, halo = st.tm(), 512, 16
    assert st.seq & (st.seq - 1) == 0
    per = tm // halo
    last = rows // halo - 1
    return pl.pallas_call(
        functools.partial(_convout_body, seq=st.seq, tm=tm),
        grid=(rows // tm, d // tn),
        in_specs=[
            pl.BlockSpec((tm, d), lambda m, n: (m, 0)),
            pl.BlockSpec((tm, d), lambda m, n: (m, 0)),
            pl.BlockSpec((halo, d), lambda m, n: (jnp.maximum(m * per - 1, 0), 0)),
            pl.BlockSpec((halo, d), lambda m, n: (jnp.minimum((m + 1) * per, last), 0)),
            pl.BlockSpec((3, d), lambda m, n: (0, 0)),
            pl.BlockSpec((d, tn), lambda m, n: (0, n)),
            pl.BlockSpec((tm, tn), lambda m, n: (m, n)),
            _mod_spec(st, tm, 2, d, width=tn),
        ],
        out_specs=pl.BlockSpec((tm, tn), lambda m, n: (m, n)),
        out_shape=jax.ShapeDtypeStruct((rows, d), F32),
        scratch_shapes=[pltpu.VMEM((tm, d), BF16)],
        compiler_params=_params("arbitrary", "arbitrary"),
        name="conv_outproj",
    )(gb, u, u, u, conv_w, w_out, h, mods)


def _qkv_body(h_ref, g_ref, sh_ref, sc_ref, w_ref, qg_ref, kg_ref, cos_ref, sin_ref, o_ref, a_scr, *, n_q, tn):
    n = pl.program_id(1)

    @pl.when(n == 0)
    def _():
        a_scr[...] = _modulate(h_ref[...], g_ref[...], sh_ref[0], sc_ref[0]).astype(BF16)

    acc = jnp.dot(a_scr[...], w_ref[...], preferred_element_type=F32)

    def norm_rope(gain, scale):
        cos = cos_ref[...]
        sin = sin_ref[...]
        lane = lax.broadcasted_iota(I32, (1, HEAD_DIM), 1)
        first = (lane % (HEAD_DIM // 2)) < (HEAD_DIM // 4)
        outs = []
        for hd in range(tn // HEAD_DIM):
            x = acc[:, hd * HEAD_DIM:(hd + 1) * HEAD_DIM]
            ms = jnp.mean(x * x, axis=-1, keepdims=True)
            xn = x * lax.rsqrt(ms + NORM_EPS) * gain
            rot = jnp.where(first, pltpu.roll(xn, HEAD_DIM - HEAD_DIM // 4, 1), pltpu.roll(xn, HEAD_DIM // 4, 1))
            outs.append(((xn * cos + rot * sin) * scale).astype(BF16))
        return outs[0] if len(outs) == 1 else jnp.concatenate(outs, axis=1)

    @pl.when(n < n_q)
    def _():
        o_ref[...] = norm_rope(qg_ref[...], HEAD_DIM ** -0.5)

    @pl.when(n == n_q)
    def _():
        o_ref[...] = norm_rope(kg_ref[...], 1.0)

    @pl.when(n > n_q)
    def _():
        o_ref[...] = acc.astype(BF16)


def _rope_tables(dims, tm):
    half = HEAD_DIM // 4
    inv_freq = ROPE_THETA ** (-jnp.arange(half, dtype=F32) * 2.0 / (HEAD_DIM // 2))
    t = jnp.arange(dims.seq)
    ang_row = (t // dims.grid_w).astype(F32)[:, None] * inv_freq
    ang_col = (t % dims.grid_w).astype(F32)[:, None] * inv_freq
    cos = jnp.concatenate([jnp.cos(ang_row)] * 2 + [jnp.cos(ang_col)] * 2, axis=-1)
    sin = jnp.concatenate([-jnp.sin(ang_row), jnp.sin(ang_row), -jnp.sin(ang_col), jnp.sin(ang_col)], axis=-1)
    cos = jnp.concatenate([cos, jnp.ones((tm, HEAD_DIM), F32)], axis=0)
    sin = jnp.concatenate([sin, jnp.zeros((tm, HEAD_DIM), F32)], axis=0)
    return cos, sin


def _qkvproj(dims, st, h, mods, gain, w_qkv, q_gain, k_gain, with_q):
    rows, d = h.shape
    tm, tn = st.tm(), dims.dkv
    n_q = dims.dq // tn if with_q else 0
    col0 = 0 if with_q else dims.dq // tn
    n_total = n_q + 2
    cos, sin = _rope_tables(dims, tm)
    per = dims.seq // tm
    pos_tile = (lambda m, n: (m % per, 0)) if with_q else (lambda m, n: (per, 0))
    return pl.pallas_call(
        functools.partial(_qkv_body, n_q=n_q, tn=tn),
        grid=(rows // tm, n_total),
        in_specs=[
            pl.BlockSpec((tm, d), lambda m, n: (m, 0)),
            pl.BlockSpec((1, d), lambda m, n: (0, 0)),
            _mod_spec(st, tm, 0, d),
            _mod_spec(st, tm, 1, d),
            pl.BlockSpec((d, tn), lambda m, n: (0, col0 + n)),
            pl.BlockSpec((1, HEAD_DIM), lambda m, n: (0, 0)),
            pl.BlockSpec((1, HEAD_DIM), lambda m, n: (0, 0)),
            pl.BlockSpec((tm, HEAD_DIM), pos_tile),
            pl.BlockSpec((tm, HEAD_DIM), pos_tile),
        ],
        out_specs=pl.BlockSpec((tm, tn), lambda m, n: (m, n)),
        out_shape=jax.ShapeDtypeStruct((rows, n_total * tn), BF16),
        scratch_shapes=[pltpu.VMEM((tm, d), BF16)],
        compiler_params=_params("arbitrary", "arbitrary"),
        name="attn_qkv",
    )(h, gain.reshape(1, d), mods, mods, w_qkv, q_gain.reshape(1, HEAD_DIM), k_gain.reshape(1, HEAD_DIM), cos, sin)


def _attn_body(sink_ref, q_ref, k_ref, v_ref, kc_ref, vc_ref, o_ref, *, dims):
    group = dims.heads // dims.kv_heads
    blk = ATTN_BLOCK
    band = 3 * blk
    n_ctx = dims.ctx
    head0 = pl.program_id(1) * group
    kc = kc_ref[...]
    vc = vc_ref[...]

    def block(n, carry):
        q0 = pl.multiple_of(n * blk, blk)
        start = pl.multiple_of(jnp.clip((n - 1) * blk, 0, dims.seq - band), blk)
        kcat = jnp.concatenate([k_ref[pl.ds(start, band), :], kc], axis=0)
        vcat = jnp.concatenate([v_ref[pl.ds(start, band), :], vc], axis=0)
        q = jnp.concatenate([q_ref[pl.ds(q0, blk), g * HEAD_DIM:(g + 1) * HEAD_DIM] for g in range(group)], axis=0)
        sink = jnp.concatenate([jnp.full((blk, 1), sink_ref[head0 + g], F32) for g in range(group)], axis=0)
        q_pos = q0 + (lax.broadcasted_iota(I32, (group * blk, 1), 0) & (blk - 1))
        col = lax.broadcasted_iota(I32, (1, band + n_ctx), 1)
        valid = (col >= band) | (jnp.abs(q_pos - (start + col)) <= blk)
        s = lax.dot_general(q, kcat, (((1,), (1,)), ((), ())), preferred_element_type=F32)
        s = jnp.where(valid, s, MASK_VALUE)
        m = jnp.maximum(jnp.max(s, axis=-1, keepdims=True), sink)
        p = jnp.exp(s - m)
        den = jnp.sum(p, axis=-1, keepdims=True) + jnp.exp(sink - m)
        o = jnp.dot(p.astype(BF16), vcat, preferred_element_type=F32) / den
        for g in range(group):
            o_ref[pl.ds(q0, blk), g * HEAD_DIM:(g + 1) * HEAD_DIM] = o[g * blk:(g + 1) * blk].astype(BF16)
        return carry

    lax.fori_loop(0, dims.seq // blk, block, 0, unroll=2)


def _attention(dims, qkv, kv_ctx, sink):
    group = dims.heads // dims.kv_heads
    gw = group * HEAD_DIM
    k_col = dims.dq // HEAD_DIM
    v_col = (dims.dq + dims.dkv) // HEAD_DIM
    return pl.pallas_call(
        functools.partial(_attn_body, dims=dims),
        grid=(dims.batch, dims.kv_heads),
        in_specs=[
            pl.BlockSpec(memory_space=pltpu.SMEM),
            pl.BlockSpec((dims.seq, gw), lambda b, h: (b, h)),
            pl.BlockSpec((dims.seq, HEAD_DIM), lambda b, h: (b, k_col + h)),
            pl.BlockSpec((dims.seq, HEAD_DIM), lambda b, h: (b, v_col + h)),
            pl.BlockSpec((dims.ctx, HEAD_DIM), lambda b, h: (b, h)),
            pl.BlockSpec((dims.ctx, HEAD_DIM), lambda b, h: (b, dims.kv_heads + h)),
        ],
        out_specs=pl.BlockSpec((dims.seq, gw), lambda b, h: (b, h)),
        out_shape=jax.ShapeDtypeStruct((dims.batch * dims.seq, dims.dq), BF16),
        compiler_params=_params("arbitrary", "arbitrary"),
        name="attn_core",
    )(sink, qkv, qkv, qkv, kv_ctx, kv_ctx)


def _oproj_body(o_ref, w_ref, h_ref, g1_ref, out_ref):
    y = jnp.dot(o_ref[...], w_ref[...], preferred_element_type=F32)
    out_ref[...] = h_ref[...] + g1_ref[0] * y


def _oproj(st, o, w_o, h, mods):
    rows, dq = o.shape
    d = w_o.shape[1]
    tm, tn = st.tm(), 512
    return pl.pallas_call(
        _oproj_body,
        grid=(rows // tm, d // tn),
        in_specs=[
            pl.BlockSpec((tm, dq), lambda m, n: (m, 0)),
            pl.BlockSpec((dq, tn), lambda m, n: (0, n)),
            pl.BlockSpec((tm, tn), lambda m, n: (m, n)),
            _mod_spec(st, tm, 2, d, width=tn),
        ],
        out_specs=pl.BlockSpec((tm, tn), lambda m, n: (m, n)),
        out_shape=jax.ShapeDtypeStruct((rows, d), F32),
        compiler_params=_params("arbitrary", "arbitrary"),
        name="attn_oproj",
    )(o, w_o, h, mods)


def _router_body(h_ref, g_ref, sh_ref, sc_ref, wr_ref, h2_ref, aff_ref, *, experts):
    a = _modulate(h_ref[...], g_ref[...], sh_ref[0], sc_ref[0])
    tm, half = a.shape[0], a.shape[1] // 2
    packed = _pack_bf16_pair(a[:, :half], a[:, half:])
    nck = half // 128
    for j in range(nck):
        h2_ref[pl.ds(j, tm, stride=nck), :] = packed[:, j * 128:(j + 1) * 128]
    logits = jnp.dot(a.astype(BF16), wr_ref[...], preferred_element_type=F32)
    lt = logits.T[0:experts, :]
    ex = jnp.exp(lt - jnp.max(lt, axis=0, keepdims=True))
    aff_ref[...] = ex / jnp.sum(ex, axis=0, keepdims=True)


def _router(st, h, mods, gain, w_router):
    rows, d = h.shape
    experts = w_router.shape[1]
    tm = st.tm(512)
    wr = jnp.zeros((d, 128), BF16).at[:, :experts].set(w_router.astype(BF16))
    return pl.pallas_call(
        functools.partial(_router_body, experts=experts),
        grid=(rows // tm, 1),
        in_specs=[
            pl.BlockSpec((tm, d), lambda m, n: (m, 0)),
            pl.BlockSpec((1, d), lambda m, n: (0, 0)),
            _mod_spec(st, tm, 3, d),
            _mod_spec(st, tm, 4, d),
            pl.BlockSpec((d, 128), lambda m, n: (0, 0)),
        ],
        out_specs=[
            pl.BlockSpec((tm * (d // 256), 128), lambda m, n: (m, 0)),
            pl.BlockSpec((experts, tm), lambda m, n: (0, m)),
        ],
        out_shape=[
            jax.ShapeDtypeStruct((rows * (d // 256), 128), I32),
            jax.ShapeDtypeStruct((experts, rows), F32),
        ],
        compiler_params=_params("arbitrary", "arbitrary"),
        name="moe_router",
    )(h, gain.reshape(1, d), mods, mods, wr)


def _lane_prefix(x, tri):
    n = x.shape[1]
    off = jnp.zeros((x.shape[0], 1), F32)
    chunks = []
    for ch in range(n // 128):
        xc = x[:, ch * 128:(ch + 1) * 128]
        incl = jnp.dot(xc.astype(BF16), tri, preferred_element_type=F32)
        chunks.append(incl - xc + off)
        off = off + incl[:, 127:128]
    return chunks[0] if len(chunks) == 1 else jnp.concatenate(chunks, axis=1)


def _topk_body(aff_ref, idx_ref, dst_ref, gate_ref, off_ref, cnt_ref, pos_scr, sel_scr, dstv_scr, *,
               n_tok, cap, pair0, experts):
    grp = pl.program_id(0)
    a = aff_ref[...]
    bits = pltpu.bitcast(a, I32)
    lane = lax.broadcasted_iota(I32, (experts, n_tok), 1)

    def count(mask):
        return jnp.sum(mask.astype(I32), axis=1, keepdims=True)

    def thr_step(i, thr):
        cand = thr | jnp.left_shift(jnp.int32(1), 30 - i)
        return jnp.where(count(bits >= cand) >= cap, cand, thr)

    thr = lax.fori_loop(0, 31, thr_step, jnp.zeros((experts, 1), I32))
    above = bits > thr
    tie = bits == thr
    need = cap - count(above)

    top_bit = n_tok.bit_length() - 2

    def tie_step(i, lim):
        cand = lim | jnp.left_shift(jnp.int32(1), top_bit - i)
        return jnp.where(count(tie & (lane < cand)) < need, cand, lim)

    lim = lax.fori_loop(0, top_bit + 1, tie_step, jnp.zeros((experts, 1), I32))
    sel = above | (tie & (lane <= lim))
    sel_f = sel.astype(F32)
    sel_i = sel.astype(I32)

    r = lax.broadcasted_iota(I32, (128, 128), 0)
    c = lax.broadcasted_iota(I32, (128, 128), 1)
    tri = (r <= c).astype(F32).astype(BF16)
    pos_scr[...] = _lane_prefix(sel_f, tri).astype(I32)
    sel_scr[...] = sel_i

    level = jnp.zeros((1, n_tok), I32)
    levels = []
    for e in range(experts):
        levels.append(level)
        level = level + sel_i[e:e + 1, :]
    cnt = level
    off = _lane_prefix(cnt.astype(F32), tri).astype(I32) + (pair0 + grp * (experts * cap))
    off_ref[...] = off
    cnt_ref[...] = cnt
    for e in range(experts):
        dstv_scr[e:e + 1, :] = off + levels[e]

    sc = min(cap, 64)
    tok = lax.broadcasted_iota(I32, (1, n_tok), 1)
    slot_iota = lax.broadcasted_iota(I32, (sc, 1), 0)

    def per_expert(e, carry):
        pe = pos_scr[pl.ds(e, 1), :]
        se = sel_scr[pl.ds(e, 1), :] > 0
        ae = aff_ref[pl.ds(e, 1), :]
        de = dstv_scr[pl.ds(e, 1), :]
        for s0 in range(0, cap, sc):
            hit = (pe == (slot_iota + s0)) & se
            idx_ref[0, e, s0:s0 + sc, :] = (jnp.sum(jnp.where(hit, tok, 0), axis=1, keepdims=True)
                                             + grp * n_tok)
            dst_ref[0, e, s0:s0 + sc, :] = jnp.sum(jnp.where(hit, de, 0), axis=1, keepdims=True)
            gate_ref[0, e, s0:s0 + sc, :] = jnp.sum(jnp.where(hit, ae, 0.0), axis=1, keepdims=True)
        return carry

    lax.fori_loop(0, experts, per_expert, 0)


def _topk(st, aff, cap, pair0):
    e = aff.shape[0]
    n_tok, n_groups = st.seq, st.batch
    slot_shape = jax.ShapeDtypeStruct((n_groups, e, cap, 1), I32)
    slot_spec = pl.BlockSpec((1, e, cap, 1), lambda g: (g, 0, 0, 0))
    return pl.pallas_call(
        functools.partial(_topk_body, n_tok=n_tok, cap=cap, pair0=pair0, experts=e),
        grid=(n_groups,),
        in_specs=[pl.BlockSpec((e, n_tok), lambda g: (0, g))],
        out_specs=[slot_spec, slot_spec, slot_spec,
                   pl.BlockSpec((1, n_tok), lambda g: (0, g)),
                   pl.BlockSpec((1, n_tok), lambda g: (0, g))],
        out_shape=[slot_shape, slot_shape, jax.ShapeDtypeStruct((n_groups, e, cap, 1), F32),
                   jax.ShapeDtypeStruct((1, n_groups * n_tok), I32),
                   jax.ShapeDtypeStruct((1, n_groups * n_tok), I32)],
        scratch_shapes=[pltpu.VMEM((e, n_tok), I32)] * 3,
        compiler_params=_params("arbitrary"),
        name="moe_topk",
    )(aff)


def _expert_body(*refs, n_src, src_slots, slots, rc, n_f, n_n, tf, n_e):
    idx_ref, idx_nxt_ref, dst_ref, dst_prv_ref = refs[0:4]
    srcs = refs[4:4 + n_src]
    gate_ref, wg_ref, wu_ref, wd_ref, pairs_hbm, x_scr, y_scr, mid_scr, gsem, ssem = refs[4 + n_src:]
    e = pl.program_id(0)
    s = pl.program_id(1)
    nck = x_scr.shape[0] // slots
    half = nck * 128
    tnc = nck // n_n

    def tok(ref, r):
        return ref.at[pl.ds(pl.multiple_of(r * nck, nck), nck), :]

    def gather(ref, lo, n):
        def one(i, carry):
            r = lo + i
            for (s0, s1), src in zip(src_slots, srcs):
                if len(srcs) == 1:
                    pltpu.make_async_copy(tok(src, ref[0, 0, r]), tok(x_scr, r), gsem).start()
                else:
                    @pl.when((r >= s0) & (r < s1))
                    def _():
                        pltpu.make_async_copy(tok(src, ref[0, 0, r]), tok(x_scr, r), gsem).start()
            return carry
        lax.fori_loop(0, n, one, 0, unroll=8)

    def scatter(ref, lo, n):
        def one(i, carry):
            r = lo + i
            pltpu.make_async_copy(tok(y_scr, r), tok(pairs_hbm, ref[0, 0, r]), ssem).start()
            return carry
        lax.fori_loop(0, n, one, 0, unroll=8)

    def wait_gather():
        pltpu.make_async_copy(srcs[0].at[pl.ds(0, slots * nck), :], x_scr, gsem).wait()

    def wait_scatter():
        pltpu.make_async_copy(y_scr, pairs_hbm.at[pl.ds(0, slots * nck), :], ssem).wait()

    def x_rows(r0):
        parts = [_unpack_bf16_pair(x_scr[pl.ds(r0 * nck + c, rc, stride=nck), :]) for c in range(nck)]
        return (jnp.concatenate([p[0] for p in parts], axis=1), jnp.concatenate([p[1] for p in parts], axis=1))

    @pl.when((e == 0) & (s == 0))
    def _():
        y_scr[...] = jnp.zeros_like(y_scr)
        gather(idx_ref, 0, slots)

    @pl.when(s == 0)
    def _():
        wait_gather()

    @pl.when(s < n_f)
    def _():
        wg = wg_ref[0, 0].astype(BF16)
        wu = wu_ref[0, 0].astype(BF16)
        for r0 in range(0, slots, rc):
            lo, hi = x_rows(r0)
            a = (jnp.dot(lo, wg[:half], preferred_element_type=F32)
                 + jnp.dot(hi, wg[half:], preferred_element_type=F32))
            u = (jnp.dot(lo, wu[:half], preferred_element_type=F32)
                 + jnp.dot(hi, wu[half:], preferred_element_type=F32))
            mid_scr[s, r0:r0 + rc, :] = (a * jax.nn.sigmoid(a) * u).astype(BF16)
        scatter(dst_prv_ref, s * (slots // n_f), slots // n_f)

    @pl.when(s == n_f)
    def _():
        wait_scatter()

    for j in range(n_n):
        @pl.when(s == n_f + j)
        def _():
            wd = wd_ref[0, 0].astype(BF16)
            for r0 in range(0, slots, rc):
                y = jnp.dot(mid_scr[0, r0:r0 + rc, :], wd[0:tf], preferred_element_type=F32)
                for f in range(1, n_f):
                    y += jnp.dot(mid_scr[f, r0:r0 + rc, :], wd[f * tf:(f + 1) * tf], preferred_element_type=F32)
                y = y * gate_ref[0, r0:r0 + rc, :]
                tnw = tnc * 128
                packed = _pack_bf16_pair(y[:, :tnw], y[:, tnw:])
                for c in range(tnc):
                    y_scr[pl.ds(r0 * nck + j * tnc + c, rc, stride=nck), :] = packed[:, c * 128:(c + 1) * 128]
            gather(idx_nxt_ref, j * (slots // n_n), slots // n_n)

    @pl.when((e == n_e - 1) & (s == n_f + n_n - 1))
    def _():
        wait_gather()
        scatter(dst_ref, 0, slots)
        wait_scatter()


def _experts(idx, dst, gate, sources, src_slots, w_gate, w_up, w_down, layer, n_pairs):
    n_e, slots = idx.shape
    d, fdim = w_gate.shape[2], w_gate.shape[3]
    tf, tn = 256, EXPERT_DOWN_TILE
    n_f, n_n = fdim // tf, d // tn
    rc = slots // 4
    nck = d // 256
    assert slots % (8 * n_f) == 0 and slots % (8 * n_n) == 0 and rc % 16 == 0 and nck % n_n == 0
    idx3 = idx.reshape(n_e, 1, slots)
    dst3 = dst.reshape(n_e, 1, slots)
    smem = lambda f: pl.BlockSpec((1, 1, slots), f, memory_space=pltpu.SMEM)
    up_chunk = lambda i, s: (layer, i, 0, jnp.minimum(s, n_f - 1))
    return pl.pallas_call(
        functools.partial(_expert_body, n_src=len(sources), src_slots=src_slots, slots=slots, rc=rc,
                          n_f=n_f, n_n=n_n, tf=tf, n_e=n_e),
        grid=(n_e, n_f + n_n),
        in_specs=[
            smem(lambda i, s: (i, 0, 0)),
            smem(lambda i, s: (jnp.minimum(i + 1, n_e - 1), 0, 0)),
            smem(lambda i, s: (i, 0, 0)),
            smem(lambda i, s: (jnp.maximum(i - 1, 0), 0, 0)),
        ] + [pl.BlockSpec(memory_space=pl.ANY)] * len(sources) + [
            pl.BlockSpec((1, slots, 1), lambda i, s: (i, 0, 0)),
            pl.BlockSpec((1, 1, d, tf), up_chunk),
            pl.BlockSpec((1, 1, d, tf), up_chunk),
            pl.BlockSpec((1, 1, fdim, tn), lambda i, s: (layer, i, 0, jnp.maximum(s - n_f, 0))),
        ],
        out_specs=pl.BlockSpec(memory_space=pl.ANY),
        out_shape=jax.ShapeDtypeStruct((n_pairs * nck, 128), I32),
        scratch_shapes=[
            pltpu.VMEM((slots * nck, 128), I32),
            pltpu.VMEM((slots * nck, 128), I32),
            pltpu.VMEM((n_f, slots, tf), BF16),
            pltpu.SemaphoreType.DMA(()),
            pltpu.SemaphoreType.DMA(()),
        ],
        compiler_params=_params("arbitrary", "arbitrary"),
        name="moe_experts",
    )(idx3, idx3, dst3, dst3, *sources, gate, w_gate, w_up, w_down)


def _combine_body(lo_ref, hi_ref, off_ref, cnt_ref, p_ref, h_ref, g2_ref, o_ref, acc_lo, acc_hi, *,
                  tt, nck, tnw, pair0, pairs_per_group):
    base = pair0 + pl.program_id(0) * pairs_per_group
    i = pl.program_id(1)
    lane = lax.broadcasted_iota(I32, (1, PAIR_BLOCK), 1)
    first = off_ref[...] - base
    last = first + cnt_ref[...]
    shift = PAIR_BLOCK.bit_length() - 1
    k0 = lax.shift_right_logical(lo_ref[0, 0, i] - base, shift)
    k1 = lax.shift_right_logical(hi_ref[0, 0, i] - base + (PAIR_BLOCK - 1), shift)
    acc_lo[...] = jnp.zeros_like(acc_lo)
    acc_hi[...] = jnp.zeros_like(acc_hi)

    def step(k, carry):
        p0 = pl.multiple_of(k * PAIR_BLOCK, PAIR_BLOCK)
        parts = [_unpack_bf16_pair(p_ref[pl.ds(p0 * nck + c, PAIR_BLOCK, stride=nck), :]) for c in range(nck)]
        lo = jnp.concatenate([p[0] for p in parts], axis=1)
        hi = jnp.concatenate([p[1] for p in parts], axis=1)
        pr = p0 + lane
        seg = ((pr >= first) & (pr < last)).astype(F32).astype(BF16)
        acc_lo[...] += jnp.dot(seg, lo, preferred_element_type=F32)
        acc_hi[...] += jnp.dot(seg, hi, preferred_element_type=F32)
        return carry

    lax.fori_loop(k0, k1, step, 0)
    for j in range(nck * 128 // tnw):
        words = slice(j * tnw, (j + 1) * tnw)
        c_lo = slice(2 * j * tnw, (2 * j + 1) * tnw)
        c_hi = slice((2 * j + 1) * tnw, (2 * j + 2) * tnw)
        o_ref[:, c_lo] = h_ref[:, c_lo] + g2_ref[0, :, c_lo] * acc_lo[:, words]
        o_ref[:, c_hi] = h_ref[:, c_hi] + g2_ref[0, :, c_hi] * acc_hi[:, words]


def _combine(st, off, cnt, pairs, h, mods, pair0, pairs_per_group):
    rows, d = h.shape
    n_tok = st.seq
    tt = min(256, n_tok)
    n_tiles = n_tok // tt
    nck = d // 256
    assert pairs_per_group % PAIR_BLOCK == 0 and pair0 % pairs_per_group == 0
    off_t = off.reshape(st.batch, 1, n_tiles, tt)
    cnt_t = cnt.reshape(st.batch, 1, n_tiles, tt)
    tile_lo = off_t[..., 0]
    tile_hi = off_t[..., tt - 1] + cnt_t[..., tt - 1]
    row = st.ada_row(n_tok)
    smem = pl.BlockSpec((1, 1, n_tiles), lambda b, i: (b, 0, 0), memory_space=pltpu.SMEM)
    tile = lambda b, i: (b * n_tiles + i, 0)
    return pl.pallas_call(
        functools.partial(_combine_body, tt=tt, nck=nck, tnw=EXPERT_DOWN_TILE // 2, pair0=pair0,
                          pairs_per_group=pairs_per_group),
        grid=(st.batch, n_tiles),
        in_specs=[
            smem, smem,
            pl.BlockSpec((tt, 1), tile),
            pl.BlockSpec((tt, 1), tile),
            pl.BlockSpec((pairs_per_group * nck, 128), lambda b, i: (pair0 // pairs_per_group + b, 0)),
            pl.BlockSpec((tt, d), tile),
            pl.BlockSpec((1, 1, d), lambda b, i: (row(b) * N_ADA + 5, 0, 0)),
        ],
        out_specs=pl.BlockSpec((tt, d), tile),
        out_shape=jax.ShapeDtypeStruct((rows, d), F32),
        scratch_shapes=[pltpu.VMEM((tt, d // 2), F32)] * 2,
        compiler_params=_params("arbitrary", "arbitrary"),
        name="moe_combine",
    )(tile_lo, tile_hi, off.reshape(rows, 1), cnt.reshape(rows, 1), pairs, h, mods)


def _moe(streams, hs, mods, gain, w_router, w_gate, w_up, w_down, layer):
    n_e = w_router.shape[2]
    routed = []
    pair0 = 0
    slot0 = 0
    src_slots = []
    for st, h in zip(streams, hs):
        cap = CAP_FACTOR * st.seq // n_e
        h2, aff = _router(st, h, mods, gain, w_router[layer])
        idx, dst, gate, off, cnt = _topk(st, aff, cap, pair0)
        routed.append((h2, idx, dst, gate, off, cnt, pair0, n_e * cap))
        src_slots.append((slot0, slot0 + st.batch * cap))
        pair0 += st.batch * n_e * cap
        slot0 += st.batch * cap
    flat = lambda t: t[..., 0].transpose(1, 0, 2).reshape(n_e, -1)
    idx = jnp.concatenate([flat(r[1]) for r in routed], axis=1)
    dst = jnp.concatenate([flat(r[2]) for r in routed], axis=1)
    gate = jnp.concatenate([flat(r[3]) for r in routed], axis=1)[..., None]
    pairs = _experts(idx, dst, gate, [r[0] for r in routed], tuple(src_slots), w_gate, w_up, w_down, layer, pair0)
    return [_combine(st, r[4], r[5], pairs, h, mods, r[6], r[7]) for st, h, r in zip(streams, hs, routed)]


def _forward(dims, x, c, ctx, c_ctx, ada_w, ada_b, norm_mix_g, norm_ffn_g, conv_w_in, conv_w, conv_w_out,
             attn_w_qkv, attn_q_gain, attn_k_gain, attn_sink, attn_w_o, router_w,
             expert_w_gate, expert_w_up, expert_w_down):
    d = dims.d
    lat = Stream(dims.batch, dims.seq)
    con = Stream(dims.batch, dims.ctx, shared_row=dims.batch)
    cvec = jnp.zeros((ADA_ROWS, d), F32).at[:dims.batch].set(c).at[dims.batch].set(c_ctx)
    mods = _adaln(cvec, ada_w, ada_b)
    mods = mods.reshape(mods.shape[0], ADA_ROWS * N_ADA, 1, d)
    hs = [x.reshape(lat.rows, d), ctx.reshape(con.rows, d)]

    w_in, w_out = conv_w_in[0].astype(BF16), conv_w_out[0].astype(BF16)
    mixed = []
    for st, h in zip((lat, con), hs):
        gb, u = _inproj(st, h, mods[0], norm_mix_g[0], w_in)
        mixed.append(_convout(st, gb, u, conv_w[0], w_out, h, mods[0]))
    h_lat, h_ctx = _moe((lat, con), mixed, mods[0], norm_ffn_g[0], router_w,
                        expert_w_gate, expert_w_up, expert_w_down, 0)

    w_qkv, w_o = attn_w_qkv[0].astype(BF16), attn_w_o[0].astype(BF16)
    qkv = _qkvproj(dims, lat, h_lat, mods[1], norm_mix_g[1], w_qkv, attn_q_gain[0], attn_k_gain[0], True)
    kv_ctx = _qkvproj(dims, con, h_ctx, mods[1], norm_mix_g[1], w_qkv, attn_q_gain[0], attn_k_gain[0], False)
    o = _attention(dims, qkv, kv_ctx, attn_sink[0])
    h_lat = _oproj(lat, o, w_o, h_lat, mods[1])
    (h_lat,) = _moe((lat,), [h_lat], mods[1], norm_ffn_g[1], router_w,
                    expert_w_gate, expert_w_up, expert_w_down, 1)
    return h_lat.reshape(dims.batch, dims.seq, d)


def kernel(x, c, ctx, c_ctx, ada_w, ada_b, norm_mix_g, norm_ffn_g, conv_w_in, conv_w, conv_w_out, attn_w_qkv, attn_q_gain, attn_k_gain, attn_sink, attn_w_o, router_w, expert_w_gate, expert_w_up, expert_w_down):
    batch, seq, d = x.shape
    dims = Dims(d=d, batch=batch, seq=seq, grid_w=GRID_W, ctx=ctx.shape[1], heads=d // HEAD_DIM,
                kv_heads=(attn_w_qkv.shape[2] // HEAD_DIM - d // HEAD_DIM) // 2,
                experts=router_w.shape[2], d_expert=expert_w_gate.shape[3])
    return _forward(dims, x, c, ctx, c_ctx, ada_w, ada_b, norm_mix_g, norm_ffn_g, conv_w_in, conv_w, conv_w_out,
                    attn_w_qkv, attn_q_gain, attn_k_gain, attn_sink, attn_w_o, router_w,
                    expert_w_gate, expert_w_up, expert_w_down)
```

```python
import dataclasses
import functools

import jax
import jax.numpy as jnp
from jax import lax
from jax.experimental import pallas as pl
from jax.experimental.pallas import tpu as pltpu

F32 = jnp.float32
BF16 = jnp.bfloat16
I32 = jnp.int32

NORM_EPS = 1e-6
MASK_VALUE = -1e30
ROPE_THETA = 10000.0
GRID_W = 64
CAP_FACTOR = 2
HEAD_DIM = 128
ATTN_BLOCK = 128
N_ADA = 6
ADA_ROWS = 16
HI_MASK = -65536
PAIR_BLOCK = 256
EXPERT_DOWN_TILE = 512
VMEM_LIMIT = 58 * 1024 * 1024


@dataclasses.dataclass(frozen=True)
class Dims:
    d: int
    batch: int
    seq: int
    grid_w: int
    ctx: int
    heads: int
    kv_heads: int
    experts: int
    d_expert: int

    @property
    def dq(self):
        return self.heads * HEAD_DIM

    @property
    def dkv(self):
        return self.kv_heads * HEAD_DIM


@dataclasses.dataclass(frozen=True)
class Stream:
    batch: int
    seq: int
    shared_row: int = -1

    @property
    def rows(self):
        return self.batch * self.seq

    def tm(self, cap=1024):
        unit = self.rows if self.shared_row >= 0 else self.seq
        t = cap
        while unit % t:
            t //= 2
        return t

    def ada_row(self, tm):
        if self.shared_row >= 0:
            return lambda m: self.shared_row
        per = self.seq // tm
        assert per >= 1
        return lambda m: m // per


def _params(*sem):
    return pltpu.CompilerParams(dimension_semantics=sem, vmem_limit_bytes=VMEM_LIMIT)


def _modulate(h, gain, shift, scale):
    ms = jnp.mean(h * h, axis=-1, keepdims=True)
    xn = h * lax.rsqrt(ms + NORM_EPS)
    return (xn * gain) * (1.0 + scale) + shift


def _mod_spec(st, tm, which, d, width=None):
    row = st.ada_row(tm)
    if width is None:
        return pl.BlockSpec((1, 1, d), lambda m, n: (row(m) * N_ADA + which, 0, 0))
    return pl.BlockSpec((1, 1, width), lambda m, n: (row(m) * N_ADA + which, 0, n))


def _pack_bf16_pair(lo, hi):
    lo_bits = pltpu.bitcast(lo.astype(BF16).astype(F32), I32)
    hi_bits = pltpu.bitcast(hi.astype(BF16).astype(F32), I32)
    return (hi_bits & HI_MASK) | lax.shift_right_logical(lo_bits, 16)


def _unpack_bf16_pair(w):
    lo = pltpu.bitcast(lax.shift_left(w, 16), F32).astype(BF16)
    hi = pltpu.bitcast(w & HI_MASK, F32).astype(BF16)
    return lo, hi


def _adaln_body(c_ref, w_ref, b_ref, o_ref):
    c = c_ref[...]
    s = c * jax.nn.sigmoid(c)
    o_ref[0] = jnp.dot(s.astype(BF16), w_ref[0].astype(BF16), preferred_element_type=F32) + b_ref[0]


def _adaln(cvec, ada_w, ada_b):
    depth, d, n = ada_w.shape
    tn = min(1024, n)
    return pl.pallas_call(
        _adaln_body,
        grid=(depth, n // tn),
        in_specs=[
            pl.BlockSpec((ADA_ROWS, d), lambda l, j: (0, 0)),
            pl.BlockSpec((1, d, tn), lambda l, j: (l, 0, j)),
            pl.BlockSpec((1, 1, tn), lambda l, j: (l, 0, j)),
        ],
        out_specs=pl.BlockSpec((1, ADA_ROWS, tn), lambda l, j: (l, 0, j)),
        out_shape=jax.ShapeDtypeStruct((depth, ADA_ROWS, n), F32),
        compiler_params=_params("arbitrary", "arbitrary"),
        name="adaln",
    )(cvec, ada_w, ada_b.reshape(depth, 1, n))


def _inproj_body(h_ref, g_ref, sh_ref, sc_ref, wb_ref, wc_ref, wx_ref, gb_ref, u_ref, a_scr):
    @pl.when(pl.program_id(1) == 0)
    def _():
        a_scr[...] = _modulate(h_ref[...], g_ref[...], sh_ref[0], sc_ref[0]).astype(BF16)

    a = a_scr[...]
    gb = jnp.dot(a, wb_ref[...], preferred_element_type=F32)
    gc = jnp.dot(a, wc_ref[...], preferred_element_type=F32)
    xv = jnp.dot(a, wx_ref[...], preferred_element_type=F32)
    gb_ref[...] = gb.astype(BF16)
    u_ref[...] = (gc * xv).astype(BF16)


def _inproj(st, h, mods, gain, w_in):
    rows, d = h.shape
    tm, tn = st.tm(), 256
    nt = d // tn
    return pl.pallas_call(
        _inproj_body,
        grid=(rows // tm, nt),
        in_specs=[
            pl.BlockSpec((tm, d), lambda m, n: (m, 0)),
            pl.BlockSpec((1, d), lambda m, n: (0, 0)),
            _mod_spec(st, tm, 0, d),
            _mod_spec(st, tm, 1, d),
            pl.BlockSpec((d, tn), lambda m, n: (0, n)),
            pl.BlockSpec((d, tn), lambda m, n: (0, nt + n)),
            pl.BlockSpec((d, tn), lambda m, n: (0, 2 * nt + n)),
        ],
        out_specs=[pl.BlockSpec((tm, tn), lambda m, n: (m, n))] * 2,
        out_shape=[jax.ShapeDtypeStruct((rows, d), BF16)] * 2,
        scratch_shapes=[pltpu.VMEM((tm, d), BF16)],
        compiler_params=_params("arbitrary", "arbitrary"),
        name="conv_inproj",
    )(h, gain.reshape(1, d), mods, mods, w_in, w_in, w_in)


def _mixout_body(*refs, conv, seq, tm, experts, cw):
    if conv:
        gb_ref, u_ref, up_ref, un_ref, cw_ref = refs[:5]
        refs = refs[5:]
        v_scr = refs[-1]
        halo = up_ref.shape[0]
        row = lax.broadcasted_iota(I32, (tm, 1), 0)
        pos = (pl.program_id(0) * tm + row) & (seq - 1)
        for c0 in range(0, u_ref.shape[1], cw):
            cols = slice(c0, c0 + cw)
            u = u_ref[:, cols].astype(F32)
            u_dn = pltpu.roll(u, 1, 0)
            u_dn = jnp.where(row == 0, up_ref[halo - 1:halo, cols].astype(F32), u_dn)
            u_dn = jnp.where(pos == 0, 0.0, u_dn)
            u_up = pltpu.roll(u, tm - 1, 0)
            u_up = jnp.where(row == tm - 1, un_ref[0:1, cols].astype(F32), u_up)
            u_up = jnp.where(pos == seq - 1, 0.0, u_up)
            y = cw_ref[0:1, cols] * u_dn + cw_ref[1:2, cols] * u + cw_ref[2:3, cols] * u_up
            v_scr[:, cols] = (gb_ref[:, cols].astype(F32) * y).astype(BF16)
        v_ref = v_scr
    else:
        v_ref = refs[0]
        refs = refs[1:]
    w_ref, h_ref, g1_ref, gf_ref, sh_ref, sc_ref, wr_ref, hn_ref, h2_ref, aff_ref = refs[:10]
    half = h_ref.shape[1] // 2
    nck = half // 128
    th = tm // 2
    for r0 in (0, th):
        rows = slice(r0, r0 + th)
        out = jnp.dot(v_ref[rows, :], w_ref[...], preferred_element_type=F32)
        hn = h_ref[rows, :] + g1_ref[0] * out
        hn_ref[rows, :] = hn
        a = _modulate(hn, gf_ref[...], sh_ref[0], sc_ref[0])
        packed = _pack_bf16_pair(a[:, :half], a[:, half:])
        for j in range(nck):
            h2_ref[pl.ds(r0 * nck + j, th, stride=nck), :] = packed[:, j * 128:(j + 1) * 128]
        logits = jnp.dot(a.astype(BF16), wr_ref[...], preferred_element_type=F32)
        lt = logits.T[0:experts, :]
        ex = jnp.exp(lt - jnp.max(lt, axis=0, keepdims=True))
        aff_ref[:, rows] = ex / jnp.sum(ex, axis=0, keepdims=True)


def _mixout(st, v_inputs, conv_w, w_out, h, mods, gain_ffn, w_router):
    rows, d = h.shape
    experts = w_router.shape[1]
    conv = conv_w is not None
    tm, halo = st.tm(512), 16
    nck = d // 256
    assert st.seq & (st.seq - 1) == 0 and tm % 256 == 0
    row = st.ada_row(tm)
    mod = lambda which: pl.BlockSpec((1, 1, d), lambda m: (row(m) * N_ADA + which, 0, 0))
    tile = pl.BlockSpec((tm, d), lambda m: (m, 0))
    const = lambda shape: pl.BlockSpec(shape, lambda m: (0,) * len(shape))
    wr = jnp.zeros((d, 128), BF16).at[:, :experts].set(w_router.astype(BF16))
    if conv:
        gb, u = v_inputs
        per, last = tm // halo, rows // halo - 1
        v_specs = [tile, tile,
                   pl.BlockSpec((halo, d), lambda m: (jnp.maximum(m * per - 1, 0), 0)),
                   pl.BlockSpec((halo, d), lambda m: (jnp.minimum((m + 1) * per, last), 0)),
                   const((3, d))]
        v_args = (gb, u, u, u, conv_w)
        scratch = [pltpu.VMEM((tm, d), BF16)]
    else:
        v_specs, v_args, scratch = [tile], tuple(v_inputs), []
    return pl.pallas_call(
        functools.partial(_mixout_body, conv=conv, seq=st.seq, tm=tm, experts=experts, cw=512),
        grid=(rows // tm,),
        in_specs=v_specs + [
            pl.BlockSpec((d, d), lambda m: (0, 0), pipeline_mode=pl.Buffered(1)),
            tile, mod(2), const((1, d)), mod(3), mod(4), const((d, 128)),
        ],
        out_specs=[
            tile,
            pl.BlockSpec((tm * nck, 128), lambda m: (m, 0)),
            pl.BlockSpec((experts, tm), lambda m: (0, m)),
        ],
        out_shape=[
            jax.ShapeDtypeStruct((rows, d), F32),
            jax.ShapeDtypeStruct((rows * nck, 128), I32),
            jax.ShapeDtypeStruct((experts, rows), F32),
        ],
        scratch_shapes=scratch,
        compiler_params=_params("arbitrary"),
        name="mix_out_router",
    )(*v_args, w_out, h, mods, gain_ffn.reshape(1, d), mods, mods, wr)


def _qkv_body(h_ref, g_ref, sh_ref, sc_ref, w_ref, qg_ref, kg_ref, cos_ref, sin_ref, o_ref, a_scr, *, n_q, tn):
    n = pl.program_id(1)

    @pl.when(n == 0)
    def _():
        a_scr[...] = _modulate(h_ref[...], g_ref[...], sh_ref[0], sc_ref[0]).astype(BF16)

    acc = jnp.dot(a_scr[...], w_ref[...], preferred_element_type=F32)

    def norm_rope(gain, scale):
        cos = cos_ref[...]
        sin = sin_ref[...]
        lane = lax.broadcasted_iota(I32, (1, HEAD_DIM), 1)
        first = (lane % (HEAD_DIM // 2)) < (HEAD_DIM // 4)
        outs = []
        for hd in range(tn // HEAD_DIM):
            x = acc[:, hd * HEAD_DIM:(hd + 1) * HEAD_DIM]
            ms = jnp.mean(x * x, axis=-1, keepdims=True)
            xn = x * lax.rsqrt(ms + NORM_EPS) * gain
            rot = jnp.where(first, pltpu.roll(xn, HEAD_DIM - HEAD_DIM // 4, 1), pltpu.roll(xn, HEAD_DIM // 4, 1))
            outs.append(((xn * cos + rot * sin) * scale).astype(BF16))
        return outs[0] if len(outs) == 1 else jnp.concatenate(outs, axis=1)

    @pl.when(n < n_q)
    def _():
        o_ref[...] = norm_rope(qg_ref[...], HEAD_DIM ** -0.5)

    @pl.when(n == n_q)
    def _():
        o_ref[...] = norm_rope(kg_ref[...], 1.0)

    @pl.when(n > n_q)
    def _():
        o_ref[...] = acc.astype(BF16)


def _rope_tables(dims, tm):
    half = HEAD_DIM // 4
    inv_freq = ROPE_THETA ** (-jnp.arange(half, dtype=F32) * 2.0 / (HEAD_DIM // 2))
    t = jnp.arange(dims.seq)
    ang_row = (t // dims.grid_w).astype(F32)[:, None] * inv_freq
    ang_col = (t % dims.grid_w).astype(F32)[:, None] * inv_freq
    cos = jnp.concatenate([jnp.cos(ang_row)] * 2 + [jnp.cos(ang_col)] * 2, axis=-1)
    sin = jnp.concatenate([-jnp.sin(ang_row), jnp.sin(ang_row), -jnp.sin(ang_col), jnp.sin(ang_col)], axis=-1)
    cos = jnp.concatenate([cos, jnp.ones((tm, HEAD_DIM), F32)], axis=0)
    sin = jnp.concatenate([sin, jnp.zeros((tm, HEAD_DIM), F32)], axis=0)
    return cos, sin


def _qkvproj(dims, st, h, mods, gain, w_qkv, q_gain, k_gain, with_q):
    rows, d = h.shape
    tm, tn = st.tm(), dims.dkv
    n_q = dims.dq // tn if with_q else 0
    col0 = 0 if with_q else dims.dq // tn
    n_total = n_q + 2
    cos, sin = _rope_tables(dims, tm)
    per = dims.seq // tm
    pos_tile = (lambda m, n: (m % per, 0)) if with_q else (lambda m, n: (per, 0))
    return pl.pallas_call(
        functools.partial(_qkv_body, n_q=n_q, tn=tn),
        grid=(rows // tm, n_total),
        in_specs=[
            pl.BlockSpec((tm, d), lambda m, n: (m, 0)),
            pl.BlockSpec((1, d), lambda m, n: (0, 0)),
            _mod_spec(st, tm, 0, d),
            _mod_spec(st, tm, 1, d),
            pl.BlockSpec((d, tn), lambda m, n: (0, col0 + n)),
            pl.BlockSpec((1, HEAD_DIM), lambda m, n: (0, 0)),
            pl.BlockSpec((1, HEAD_DIM), lambda m, n: (0, 0)),
            pl.BlockSpec((tm, HEAD_DIM), pos_tile),
            pl.BlockSpec((tm, HEAD_DIM), pos_tile),
        ],
        out_specs=pl.BlockSpec((tm, tn), lambda m, n: (m, n)),
        out_shape=jax.ShapeDtypeStruct((rows, n_total * tn), BF16),
        scratch_shapes=[pltpu.VMEM((tm, d), BF16)],
        compiler_params=_params("arbitrary", "arbitrary"),
        name="attn_qkv",
    )(h, gain.reshape(1, d), mods, mods, w_qkv, q_gain.reshape(1, HEAD_DIM), k_gain.reshape(1, HEAD_DIM), cos, sin)


def _attn_body(sink_ref, q_ref, k_ref, v_ref, kc_ref, vc_ref, o_ref, *, dims):
    group = dims.heads // dims.kv_heads
    blk = ATTN_BLOCK
    band = 3 * blk
    n_ctx = dims.ctx
    head0 = pl.program_id(1) * group
    kc = kc_ref[...]
    vc = vc_ref[...]

    def block(n, carry):
        q0 = pl.multiple_of(n * blk, blk)
        start = pl.multiple_of(jnp.clip((n - 1) * blk, 0, dims.seq - band), blk)
        kcat = jnp.concatenate([k_ref[pl.ds(start, band), :], kc], axis=0)
        vcat = jnp.concatenate([v_ref[pl.ds(start, band), :], vc], axis=0)
        q = jnp.concatenate([q_ref[pl.ds(q0, blk), g * HEAD_DIM:(g + 1) * HEAD_DIM] for g in range(group)], axis=0)
        sink = jnp.concatenate([jnp.full((blk, 1), sink_ref[head0 + g], F32) for g in range(group)], axis=0)
        q_pos = q0 + (lax.broadcasted_iota(I32, (group * blk, 1), 0) & (blk - 1))
        col = lax.broadcasted_iota(I32, (1, band + n_ctx), 1)
        valid = (col >= band) | (jnp.abs(q_pos - (start + col)) <= blk)
        s = lax.dot_general(q, kcat, (((1,), (1,)), ((), ())), preferred_element_type=F32)
        s = jnp.where(valid, s, MASK_VALUE)
        m = jnp.maximum(jnp.max(s, axis=-1, keepdims=True), sink)
        p = jnp.exp(s - m)
        den = jnp.sum(p, axis=-1, keepdims=True) + jnp.exp(sink - m)
        o = jnp.dot(p.astype(BF16), vcat, preferred_element_type=F32) / den
        for g in range(group):
            o_ref[pl.ds(q0, blk), g * HEAD_DIM:(g + 1) * HEAD_DIM] = o[g * blk:(g + 1) * blk].astype(BF16)
        return carry

    lax.fori_loop(0, dims.seq // blk, block, 0, unroll=2)


def _attention(dims, qkv, kv_ctx, sink):
    group = dims.heads // dims.kv_heads
    gw = group * HEAD_DIM
    k_col = dims.dq // HEAD_DIM
    v_col = (dims.dq + dims.dkv) // HEAD_DIM
    return pl.pallas_call(
        functools.partial(_attn_body, dims=dims),
        grid=(dims.batch, dims.kv_heads),
        in_specs=[
            pl.BlockSpec(memory_space=pltpu.SMEM),
            pl.BlockSpec((dims.seq, gw), lambda b, h: (b, h)),
            pl.BlockSpec((dims.seq, HEAD_DIM), lambda b, h: (b, k_col + h)),
            pl.BlockSpec((dims.seq, HEAD_DIM), lambda b, h: (b, v_col + h)),
            pl.BlockSpec((dims.ctx, HEAD_DIM), lambda b, h: (b, h)),
            pl.BlockSpec((dims.ctx, HEAD_DIM), lambda b, h: (b, dims.kv_heads + h)),
        ],
        out_specs=pl.BlockSpec((dims.seq, gw), lambda b, h: (b, h)),
        out_shape=jax.ShapeDtypeStruct((dims.batch * dims.seq, dims.dq), BF16),
        compiler_params=_params("arbitrary", "arbitrary"),
        name="attn_core",
    )(sink, qkv, qkv, qkv, kv_ctx, kv_ctx)


def _lane_prefix(x, tri):
    n = x.shape[1]
    off = jnp.zeros((x.shape[0], 1), F32)
    chunks = []
    for ch in range(n // 128):
        xc = x[:, ch * 128:(ch + 1) * 128]
        incl = jnp.dot(xc.astype(BF16), tri, preferred_element_type=F32)
        chunks.append(incl - xc + off)
        off = off + incl[:, 127:128]
    return chunks[0] if len(chunks) == 1 else jnp.concatenate(chunks, axis=1)


def _topk_body(aff_ref, idx_ref, dst_ref, gate_ref, off_ref, cnt_ref, w_scr, a_scr, q_scr, clo_scr, chi_scr, *,
               n_tok, cap, pair0, experts):
    grp = pl.program_id(0)
    a = aff_ref[...]
    bits = pltpu.bitcast(a, I32)
    lane = lax.broadcasted_iota(I32, (experts, n_tok), 1)

    def count(mask):
        return jnp.sum(mask.astype(I32), axis=1, keepdims=True)

    def thr_step(i, thr):
        cand = thr | jnp.left_shift(jnp.int32(1), 30 - i)
        return jnp.where(count(bits >= cand) >= cap, cand, thr)

    thr = lax.fori_loop(0, 31, thr_step, jnp.zeros((experts, 1), I32))
    above = bits > thr
    tie = bits == thr
    need = cap - count(above)

    top_bit = n_tok.bit_length() - 2

    def tie_step(i, lim):
        cand = lim | jnp.left_shift(jnp.int32(1), top_bit - i)
        return jnp.where(count(tie & (lane < cand)) < need, cand, lim)

    lim = lax.fori_loop(0, top_bit + 1, tie_step, jnp.zeros((experts, 1), I32))
    sel = above | (tie & (lane <= lim))
    sel_f = sel.astype(F32)
    sel_i = sel.astype(I32)

    r = lax.broadcasted_iota(I32, (128, 128), 0)
    c = lax.broadcasted_iota(I32, (128, 128), 1)
    tri = (r <= c).astype(F32).astype(BF16)

    level = jnp.zeros((1, n_tok), I32)
    levels = []
    for e in range(experts):
        levels.append(level)
        level = level + sel_i[e:e + 1, :]
    cnt = level
    off = _lane_prefix(cnt.astype(F32), tri).astype(I32) + (pair0 + grp * (experts * cap))
    off_ref[...] = off
    cnt_ref[...] = cnt
    pair_row = jnp.concatenate(levels, axis=0) + off

    n_ch = n_tok // 128
    pad = w_scr.shape[0] // experts
    lane_e = lax.broadcasted_iota(I32, (experts, 128), 1)
    w_scr[...] = jnp.zeros_like(w_scr)
    a_scr[...] = jnp.zeros_like(a_scr)
    q_scr[...] = jnp.zeros_like(q_scr)
    first = jnp.zeros((experts, 1), F32)
    c_lo = jnp.full((experts, 128), cap, I32)
    c_hi = jnp.full((experts, 128), cap, I32)
    for ch in range(n_ch):
        lanes = slice(ch * 128, (ch + 1) * 128)
        incl = jnp.dot(sel_f[:, lanes].astype(BF16), tri, preferred_element_type=F32)
        w_scr[pl.ds(ch, experts, stride=pad), :] = incl
        a_scr[pl.ds(ch, experts, stride=pad), :] = a[:, lanes]
        q_scr[pl.ds(ch, experts, stride=pad), :] = pair_row[:, lanes]
        nxt = first + incl[:, 127:128]
        c_lo = jnp.where(lane_e == ch, first.astype(I32), c_lo)
        c_hi = jnp.where(lane_e == ch, nxt.astype(I32), c_hi)
        first = nxt
    clo_scr[...] = c_lo
    chi_scr[...] = c_hi

    idx_ref[0] = jnp.zeros((cap, 128), I32)
    dst_ref[0] = jnp.zeros((cap, 128), I32)
    gate_ref[0] = jnp.zeros((cap, 128), F32)
    slot = lax.broadcasted_iota(I32, (cap, 1), 0)
    lane = lax.broadcasted_iota(I32, (1, 128), 1)
    zpad = jnp.zeros((128 - pad, 128), F32)

    def rhs(m):
        return jnp.concatenate([m, zpad], axis=0).astype(BF16)

    def per_expert(e, carry):
        base = pl.multiple_of(e * pad, pad)
        wm = w_scr[pl.ds(base, pad), :]
        am = a_scr[pl.ds(base, pad), :]
        qm = q_scr[pl.ds(base, pad), :]
        lo = clo_scr[pl.ds(e, 1), :]
        hi = chi_scr[pl.ds(e, 1), :]
        in_chunk = (lo <= slot) & (slot < hi)
        onehot = in_chunk.astype(F32).astype(BF16)

        def rows_of(m):
            return jnp.dot(onehot, rhs(m), preferred_element_type=F32)

        a1 = am.astype(BF16).astype(F32)
        a2 = (am - a1).astype(BF16).astype(F32)
        a3 = am - a1 - a2
        a_rows = rows_of(a1) + rows_of(a2) + rows_of(a3)
        q_rows = (rows_of(lax.shift_right_logical(qm, 8).astype(F32)) * 256.0
                  + rows_of((qm & 255).astype(F32)))
        w_rows = rows_of(wm)
        chunk_first = jnp.sum(jnp.where(in_chunk, lo, 0), axis=1, keepdims=True)
        chunk = jnp.sum(jnp.where(in_chunk, lane, 0), axis=1, keepdims=True)
        rank = (slot - chunk_first).astype(F32)
        pos = jnp.sum((w_rows <= rank).astype(I32), axis=1, keepdims=True)
        here = lane == pos
        gate = jnp.sum(jnp.where(here, a_rows, 0.0), axis=1, keepdims=True)
        dst = jnp.sum(jnp.where(here, q_rows, 0.0), axis=1, keepdims=True).astype(I32)
        token = chunk * 128 + pos + grp * n_tok
        mine = lane == e
        idx_ref[0] = jnp.where(mine, token, idx_ref[0])
        dst_ref[0] = jnp.where(mine, dst, dst_ref[0])
        gate_ref[0] = jnp.where(mine, gate, gate_ref[0])
        return carry

    lax.fori_loop(0, experts, per_expert, 0)


def _topk(st, aff, cap, pair0):
    e = aff.shape[0]
    n_tok, n_groups = st.seq, st.batch
    pad = max(8, n_tok // 128)
    assert e <= 128 and pad <= 128 and n_groups * e * cap < 2 ** 16 * 256
    tile_spec = pl.BlockSpec((1, cap, 128), lambda g: (g, 0, 0))
    return pl.pallas_call(
        functools.partial(_topk_body, n_tok=n_tok, cap=cap, pair0=pair0, experts=e),
        grid=(n_groups,),
        in_specs=[pl.BlockSpec((e, n_tok), lambda g: (0, g))],
        out_specs=[tile_spec, tile_spec, tile_spec,
                   pl.BlockSpec((1, n_tok), lambda g: (0, g)),
                   pl.BlockSpec((1, n_tok), lambda g: (0, g))],
        out_shape=[jax.ShapeDtypeStruct((n_groups, cap, 128), I32),
                   jax.ShapeDtypeStruct((n_groups, cap, 128), I32),
                   jax.ShapeDtypeStruct((n_groups, cap, 128), F32),
                   jax.ShapeDtypeStruct((1, n_groups * n_tok), I32),
                   jax.ShapeDtypeStruct((1, n_groups * n_tok), I32)],
        scratch_shapes=[pltpu.VMEM((e * pad, 128), F32), pltpu.VMEM((e * pad, 128), F32),
                        pltpu.VMEM((e * pad, 128), I32), pltpu.VMEM((e, 128), I32), pltpu.VMEM((e, 128), I32)],
        compiler_params=_params("arbitrary"),
        name="moe_topk",
    )(aff)


def _expert_body(*refs, n_src, src_slots, slots, rc, n_f, n_n, tf, n_e):
    idx_ref, idx_nxt_ref, dst_ref, dst_prv_ref = refs[0:4]
    srcs = refs[4:4 + n_src]
    gate_ref, wg_ref, wu_ref, wd_ref, pairs_hbm, x_scr, y_scr, mid_scr, gsem, ssem = refs[4 + n_src:]
    e = pl.program_id(0)
    s = pl.program_id(1)
    nck = x_scr.shape[0] // slots
    half = nck * 128
    tnc = nck // n_n

    def tok(ref, r):
        return ref.at[pl.ds(pl.multiple_of(r * nck, nck), nck), :]

    def gather(ref, lo, n):
        def one(i, carry):
            r = lo + i
            for (s0, s1), src in zip(src_slots, srcs):
                if len(srcs) == 1:
                    pltpu.make_async_copy(tok(src, ref[0, 0, r]), tok(x_scr, r), gsem).start()
                else:
                    @pl.when((r >= s0) & (r < s1))
                    def _():
                        pltpu.make_async_copy(tok(src, ref[0, 0, r]), tok(x_scr, r), gsem).start()
            return carry
        lax.fori_loop(0, n, one, 0, unroll=8)

    def scatter(ref, lo, n):
        def one(i, carry):
            r = lo + i
            pltpu.make_async_copy(tok(y_scr, r), tok(pairs_hbm, ref[0, 0, r]), ssem).start()
            return carry
        lax.fori_loop(0, n, one, 0, unroll=8)

    def wait_gather():
        pltpu.make_async_copy(srcs[0].at[pl.ds(0, slots * nck), :], x_scr, gsem).wait()

    def wait_scatter():
        pltpu.make_async_copy(y_scr, pairs_hbm.at[pl.ds(0, slots * nck), :], ssem).wait()

    def x_rows(r0):
        parts = [_unpack_bf16_pair(x_scr[pl.ds(r0 * nck + c, rc, stride=nck), :]) for c in range(nck)]
        return (jnp.concatenate([p[0] for p in parts], axis=1), jnp.concatenate([p[1] for p in parts], axis=1))

    @pl.when((e == 0) & (s == 0))
    def _():
        y_scr[...] = jnp.zeros_like(y_scr)
        gather(idx_ref, 0, slots)

    @pl.when(s == 0)
    def _():
        wait_gather()

    @pl.when(s < n_f)
    def _():
        wg = wg_ref[0, 0].astype(BF16)
        wu = wu_ref[0, 0].astype(BF16)
        for r0 in range(0, slots, rc):
            lo, hi = x_rows(r0)
            a = (jnp.dot(lo, wg[:half], preferred_element_type=F32)
                 + jnp.dot(hi, wg[half:], preferred_element_type=F32))
            u = (jnp.dot(lo, wu[:half], preferred_element_type=F32)
                 + jnp.dot(hi, wu[half:], preferred_element_type=F32))
            mid_scr[s, r0:r0 + rc, :] = (a * jax.nn.sigmoid(a) * u).astype(BF16)
        scatter(dst_prv_ref, s * (slots // n_f), slots // n_f)

    @pl.when(s == n_f)
    def _():
        wait_scatter()

    for j in range(n_n):
        @pl.when(s == n_f + j)
        def _():
            wd = wd_ref[0, 0].astype(BF16)
            for r0 in range(0, slots, rc):
                y = jnp.dot(mid_scr[0, r0:r0 + rc, :], wd[0:tf], preferred_element_type=F32)
                for f in range(1, n_f):
                    y += jnp.dot(mid_scr[f, r0:r0 + rc, :], wd[f * tf:(f + 1) * tf], preferred_element_type=F32)
                y = y * gate_ref[0, r0:r0 + rc, :]
                tnw = tnc * 128
                packed = _pack_bf16_pair(y[:, :tnw], y[:, tnw:])
                for c in range(tnc):
                    y_scr[pl.ds(r0 * nck + j * tnc + c, rc, stride=nck), :] = packed[:, c * 128:(c + 1) * 128]
            gather(idx_nxt_ref, j * (slots // n_n), slots // n_n)

    @pl.when((e == n_e - 1) & (s == n_f + n_n - 1))
    def _():
        wait_gather()
        scatter(dst_ref, 0, slots)
        wait_scatter()


def _experts(idx, dst, gate, sources, src_slots, w_gate, w_up, w_down, layer, n_pairs):
    n_e, slots = idx.shape
    d, fdim = w_gate.shape[2], w_gate.shape[3]
    tf, tn = 256, EXPERT_DOWN_TILE
    n_f, n_n = fdim // tf, d // tn
    rc = slots // 4
    nck = d // 256
    assert slots % (8 * n_f) == 0 and slots % (8 * n_n) == 0 and rc % 16 == 0 and nck % n_n == 0
    idx3 = idx.reshape(n_e, 1, slots)
    dst3 = dst.reshape(n_e, 1, slots)
    smem = lambda f: pl.BlockSpec((1, 1, slots), f, memory_space=pltpu.SMEM)
    up_chunk = lambda i, s: (layer, i, 0, jnp.minimum(s, n_f - 1))
    return pl.pallas_call(
        functools.partial(_expert_body, n_src=len(sources), src_slots=src_slots, slots=slots, rc=rc,
                          n_f=n_f, n_n=n_n, tf=tf, n_e=n_e),
        grid=(n_e, n_f + n_n),
        in_specs=[
            smem(lambda i, s: (i, 0, 0)),
            smem(lambda i, s: (jnp.minimum(i + 1, n_e - 1), 0, 0)),
            smem(lambda i, s: (i, 0, 0)),
            smem(lambda i, s: (jnp.maximum(i - 1, 0), 0, 0)),
        ] + [pl.BlockSpec(memory_space=pl.ANY)] * len(sources) + [
            pl.BlockSpec((1, slots, 1), lambda i, s: (i, 0, 0)),
            pl.BlockSpec((1, 1, d, tf), up_chunk),
            pl.BlockSpec((1, 1, d, tf), up_chunk),
            pl.BlockSpec((1, 1, fdim, tn), lambda i, s: (layer, i, 0, jnp.maximum(s - n_f, 0))),
        ],
        out_specs=pl.BlockSpec(memory_space=pl.ANY),
        out_shape=jax.ShapeDtypeStruct((n_pairs * nck, 128), I32),
        scratch_shapes=[
            pltpu.VMEM((slots * nck, 128), I32),
            pltpu.VMEM((slots * nck, 128), I32),
            pltpu.VMEM((n_f, slots, tf), BF16),
            pltpu.SemaphoreType.DMA(()),
            pltpu.SemaphoreType.DMA(()),
        ],
        compiler_params=_params("arbitrary", "arbitrary"),
        name="moe_experts",
    )(idx3, idx3, dst3, dst3, *sources, gate, w_gate, w_up, w_down)


def _combine_body(lo_ref, hi_ref, off_ref, cnt_ref, p_ref, h_ref, g2_ref, o_ref, acc_lo, acc_hi, *,
                  tt, nck, tnw, pair0, pairs_per_group):
    base = pair0 + pl.program_id(0) * pairs_per_group
    i = pl.program_id(1)
    lane = lax.broadcasted_iota(I32, (1, PAIR_BLOCK), 1)
    eye = lax.broadcasted_iota(I32, (128, 128), 0) == lax.broadcasted_iota(I32, (128, 128), 1)

    def column(row):
        cols = [jnp.sum(jnp.where(eye, row[:, c0:c0 + 128], 0), axis=1, keepdims=True) for c0 in range(0, tt, 128)]
        return cols[0] if len(cols) == 1 else jnp.concatenate(cols, axis=0)

    first = column(off_ref[...]) - base
    last = first + column(cnt_ref[...])
    shift = PAIR_BLOCK.bit_length() - 1
    k0 = lax.shift_right_logical(lo_ref[0, 0, i] - base, shift)
    k1 = lax.shift_right_logical(hi_ref[0, 0, i] - base + (PAIR_BLOCK - 1), shift)
    acc_lo[...] = jnp.zeros_like(acc_lo)
    acc_hi[...] = jnp.zeros_like(acc_hi)

    def step(k, carry):
        p0 = pl.multiple_of(k * PAIR_BLOCK, PAIR_BLOCK)
        parts = [_unpack_bf16_pair(p_ref[pl.ds(p0 * nck + c, PAIR_BLOCK, stride=nck), :]) for c in range(nck)]
        lo = jnp.concatenate([p[0] for p in parts], axis=1)
        hi = jnp.concatenate([p[1] for p in parts], axis=1)
        pr = p0 + lane
        seg = ((pr >= first) & (pr < last)).astype(F32).astype(BF16)
        acc_lo[...] += jnp.dot(seg, lo, preferred_element_type=F32)
        acc_hi[...] += jnp.dot(seg, hi, preferred_element_type=F32)
        return carry

    lax.fori_loop(k0, k1, step, 0)
    for j in range(nck * 128 // tnw):
        words = slice(j * tnw, (j + 1) * tnw)
        c_lo = slice(2 * j * tnw, (2 * j + 1) * tnw)
        c_hi = slice((2 * j + 1) * tnw, (2 * j + 2) * tnw)
        o_ref[:, c_lo] = h_ref[:, c_lo] + g2_ref[0, :, c_lo] * acc_lo[:, words]
        o_ref[:, c_hi] = h_ref[:, c_hi] + g2_ref[0, :, c_hi] * acc_hi[:, words]


def _combine(st, off, cnt, pairs, h, mods, pair0, pairs_per_group):
    rows, d = h.shape
    n_tok = st.seq
    tt = min(256, n_tok)
    n_tiles = n_tok // tt
    nck = d // 256
    assert pairs_per_group % PAIR_BLOCK == 0 and pair0 % pairs_per_group == 0
    off_t = off.reshape(st.batch, 1, n_tiles, tt)
    cnt_t = cnt.reshape(st.batch, 1, n_tiles, tt)
    tile_lo = off_t[..., 0]
    tile_hi = off_t[..., tt - 1] + cnt_t[..., tt - 1]
    row = st.ada_row(n_tok)
    smem = pl.BlockSpec((1, 1, n_tiles), lambda b, i: (b, 0, 0), memory_space=pltpu.SMEM)
    tile = lambda b, i: (b * n_tiles + i, 0)
    return pl.pallas_call(
        functools.partial(_combine_body, tt=tt, nck=nck, tnw=EXPERT_DOWN_TILE // 2, pair0=pair0,
                          pairs_per_group=pairs_per_group),
        grid=(st.batch, n_tiles),
        in_specs=[
            smem, smem,
            pl.BlockSpec((1, tt), lambda b, i: (0, b * n_tiles + i)),
            pl.BlockSpec((1, tt), lambda b, i: (0, b * n_tiles + i)),
            pl.BlockSpec((pairs_per_group * nck, 128), lambda b, i: (pair0 // pairs_per_group + b, 0)),
            pl.BlockSpec((tt, d), tile),
            pl.BlockSpec((1, 1, d), lambda b, i: (row(b) * N_ADA + 5, 0, 0)),
        ],
        out_specs=pl.BlockSpec((tt, d), tile),
        out_shape=jax.ShapeDtypeStruct((rows, d), F32),
        scratch_shapes=[pltpu.VMEM((tt, d // 2), F32)] * 2,
        compiler_params=_params("arbitrary", "arbitrary"),
        name="moe_combine",
    )(tile_lo, tile_hi, off, cnt, pairs, h, mods)


def _moe(streams, mixed, mods, w_gate, w_up, w_down, layer):
    n_e = mixed[0][2].shape[0]
    routed = []
    pair0 = 0
    slot0 = 0
    src_slots = []
    for st, (h, h2, aff) in zip(streams, mixed):
        cap = CAP_FACTOR * st.seq // n_e
        idx, dst, gate, off, cnt = _topk(st, aff, cap, pair0)
        routed.append((h2, idx, dst, gate, off, cnt, pair0, n_e * cap))
        src_slots.append((slot0, slot0 + st.batch * cap))
        pair0 += st.batch * n_e * cap
        slot0 += st.batch * cap
    flat = lambda t: t[:, :, :n_e].transpose(2, 0, 1).reshape(n_e, -1)
    idx = jnp.concatenate([flat(r[1]) for r in routed], axis=1)
    dst = jnp.concatenate([flat(r[2]) for r in routed], axis=1)
    gate = jnp.concatenate([flat(r[3]) for r in routed], axis=1)[..., None]
    pairs = _experts(idx, dst, gate, [r[0] for r in routed], tuple(src_slots), w_gate, w_up, w_down, layer, pair0)
    return [_combine(st, r[4], r[5], pairs, m[0], mods, r[6], r[7]) for st, m, r in zip(streams, mixed, routed)]


def _forward(dims, x, c, ctx, c_ctx, ada_w, ada_b, norm_mix_g, norm_ffn_g, conv_w_in, conv_w, conv_w_out,
             attn_w_qkv, attn_q_gain, attn_k_gain, attn_sink, attn_w_o, router_w,
             expert_w_gate, expert_w_up, expert_w_down):
    d = dims.d
    lat = Stream(dims.batch, dims.seq)
    con = Stream(dims.batch, dims.ctx, shared_row=dims.batch)
    cvec = jnp.zeros((ADA_ROWS, d), F32).at[:dims.batch].set(c).at[dims.batch].set(c_ctx)
    mods = _adaln(cvec, ada_w, ada_b)
    mods = mods.reshape(mods.shape[0], ADA_ROWS * N_ADA, 1, d)
    hs = [x.reshape(lat.rows, d), ctx.reshape(con.rows, d)]

    w_in, w_out = conv_w_in[0].astype(BF16), conv_w_out[0].astype(BF16)
    mixed = []
    for st, h in zip((lat, con), hs):
        gb_u = _inproj(st, h, mods[0], norm_mix_g[0], w_in)
        mixed.append(_mixout(st, gb_u, conv_w[0], w_out, h, mods[0], norm_ffn_g[0], router_w[0]))
    h_lat, h_ctx = _moe((lat, con), mixed, mods[0], expert_w_gate, expert_w_up, expert_w_down, 0)

    w_qkv, w_o = attn_w_qkv[0].astype(BF16), attn_w_o[0].astype(BF16)
    qkv = _qkvproj(dims, lat, h_lat, mods[1], norm_mix_g[1], w_qkv, attn_q_gain[0], attn_k_gain[0], True)
    kv_ctx = _qkvproj(dims, con, h_ctx, mods[1], norm_mix_g[1], w_qkv, attn_q_gain[0], attn_k_gain[0], False)
    o = _attention(dims, qkv, kv_ctx, attn_sink[0])
    mixed = [_mixout(lat, (o,), None, w_o, h_lat, mods[1], norm_ffn_g[1], router_w[1])]
    (h_lat,) = _moe((lat,), mixed, mods[1], expert_w_gate, expert_w_up, expert_w_down, 1)
    return h_lat.reshape(dims.batch, dims.seq, d)


def kernel(x, c, ctx, c_ctx, ada_w, ada_b, norm_mix_g, norm_ffn_g, conv_w_in, conv_w, conv_w_out, attn_w_qkv, attn_q_gain, attn_k_gain, attn_sink, attn_w_o, router_w, expert_w_gate, expert_w_up, expert_w_down):
    batch, seq, d = x.shape
    dims = Dims(d=d, batch=batch, seq=seq, grid_w=GRID_W, ctx=ctx.shape[1], heads=d // HEAD_DIM,
                kv_heads=(attn_w_qkv.shape[2] // HEAD_DIM - d // HEAD_DIM) // 2,
                experts=router_w.shape[2], d_expert=expert_w_gate.shape[3])
    return _forward(dims, x, c, ctx, c_ctx, ada_w, ada_b, norm_mix_g, norm_ffn_g, conv_w_in, conv_w, conv_w_out,
                    attn_w_qkv, attn_q_gain, attn_k_gain, attn_sink, attn_w_o, router_w,
                    expert_w_gate, expert_w_up, expert_w_down)
```

```python
import dataclasses
import functools

import jax
import jax.numpy as jnp
from jax import lax
from jax.experimental import pallas as pl
from jax.experimental.pallas import tpu as pltpu

F32 = jnp.float32
BF16 = jnp.bfloat16
I32 = jnp.int32

NORM_EPS = 1e-6
MASK_VALUE = -1e30
ROPE_THETA = 10000.0
GRID_W = 64
CAP_FACTOR = 2
HEAD_DIM = 128
ATTN_BLOCK = 128
N_ADA = 6
ADA_ROWS = 16
HI_MASK = -65536
PAIR_BLOCK = 256
EXPERT_DOWN_TILE = 512
VMEM_LIMIT = 58 * 1024 * 1024


@dataclasses.dataclass(frozen=True)
class Dims:
    d: int
    batch: int
    seq: int
    grid_w: int
    ctx: int
    heads: int
    kv_heads: int
    experts: int
    d_expert: int

    @property
    def dq(self):
        return self.heads * HEAD_DIM

    @property
    def dkv(self):
        return self.kv_heads * HEAD_DIM


@dataclasses.dataclass(frozen=True)
class Stream:
    batch: int
    seq: int
    shared_row: int = -1

    @property
    def rows(self):
        return self.batch * self.seq

    def tm(self, cap=1024):
        unit = self.rows if self.shared_row >= 0 else self.seq
        t = cap
        while unit % t:
            t //= 2
        return t

    def ada_row(self, tm):
        if self.shared_row >= 0:
            return lambda m: self.shared_row
        per = self.seq // tm
        assert per >= 1
        return lambda m: m // per


def _params(*sem):
    return pltpu.CompilerParams(dimension_semantics=sem, vmem_limit_bytes=VMEM_LIMIT)


def _modulate(h, gain, shift, scale):
    ms = jnp.mean(h * h, axis=-1, keepdims=True)
    xn = h * lax.rsqrt(ms + NORM_EPS)
    return (xn * gain) * (1.0 + scale) + shift


def _mod_spec(st, tm, which, d, width=None):
    row = st.ada_row(tm)
    if width is None:
        return pl.BlockSpec((1, 1, d), lambda m, n: (row(m) * N_ADA + which, 0, 0))
    return pl.BlockSpec((1, 1, width), lambda m, n: (row(m) * N_ADA + which, 0, n))


def _pack_bf16_pair(lo, hi):
    lo_bits = pltpu.bitcast(lo.astype(BF16).astype(F32), I32)
    hi_bits = pltpu.bitcast(hi.astype(BF16).astype(F32), I32)
    return (hi_bits & HI_MASK) | lax.shift_right_logical(lo_bits, 16)


def _unpack_bf16_pair(w):
    lo = pltpu.bitcast(lax.shift_left(w, 16), F32).astype(BF16)
    hi = pltpu.bitcast(w & HI_MASK, F32).astype(BF16)
    return lo, hi


def _adaln_body(c_ref, w_ref, b_ref, o_ref):
    c = c_ref[...]
    s = c * jax.nn.sigmoid(c)
    o_ref[0] = jnp.dot(s.astype(BF16), w_ref[0].astype(BF16), preferred_element_type=F32) + b_ref[0]


def _adaln(cvec, ada_w, ada_b):
    depth, d, n = ada_w.shape
    tn = min(1024, n)
    return pl.pallas_call(
        _adaln_body,
        grid=(depth, n // tn),
        in_specs=[
            pl.BlockSpec((ADA_ROWS, d), lambda l, j: (0, 0)),
            pl.BlockSpec((1, d, tn), lambda l, j: (l, 0, j)),
            pl.BlockSpec((1, 1, tn), lambda l, j: (l, 0, j)),
        ],
        out_specs=pl.BlockSpec((1, ADA_ROWS, tn), lambda l, j: (l, 0, j)),
        out_shape=jax.ShapeDtypeStruct((depth, ADA_ROWS, n), F32),
        compiler_params=_params("arbitrary", "arbitrary"),
        name="adaln",
    )(cvec, ada_w, ada_b.reshape(depth, 1, n))


def _inproj_body(h_ref, g_ref, sh_ref, sc_ref, wb_ref, wc_ref, wx_ref, gb_ref, u_ref, a_scr):
    @pl.when(pl.program_id(1) == 0)
    def _():
        a_scr[...] = _modulate(h_ref[...], g_ref[...], sh_ref[0], sc_ref[0]).astype(BF16)

    a = a_scr[...]
    gb = jnp.dot(a, wb_ref[...], preferred_element_type=F32)
    gc = jnp.dot(a, wc_ref[...], preferred_element_type=F32)
    xv = jnp.dot(a, wx_ref[...], preferred_element_type=F32)
    gb_ref[...] = gb.astype(BF16)
    u_ref[...] = (gc * xv).astype(BF16)


def _inproj(st, h, mods, gain, w_in):
    rows, d = h.shape
    tm, tn = st.tm(), 512
    nt = d // tn
    return pl.pallas_call(
        _inproj_body,
        grid=(rows // tm, nt),
        in_specs=[
            pl.BlockSpec((tm, d), lambda m, n: (m, 0)),
            pl.BlockSpec((1, d), lambda m, n: (0, 0)),
            _mod_spec(st, tm, 0, d),
            _mod_spec(st, tm, 1, d),
            pl.BlockSpec((d, tn), lambda m, n: (0, n)),
            pl.BlockSpec((d, tn), lambda m, n: (0, nt + n)),
            pl.BlockSpec((d, tn), lambda m, n: (0, 2 * nt + n)),
        ],
        out_specs=[pl.BlockSpec((tm, tn), lambda m, n: (m, n))] * 2,
        out_shape=[jax.ShapeDtypeStruct((rows, d), BF16)] * 2,
        scratch_shapes=[pltpu.VMEM((tm, d), BF16)],
        compiler_params=_params("arbitrary", "arbitrary"),
        name="conv_inproj",
    )(h, gain.reshape(1, d), mods, mods, w_in, w_in, w_in)


def _mixout_body(*refs, conv, seq, tm, experts, cw):
    if conv:
        gb_ref, u_ref, up_ref, un_ref, cw_ref = refs[:5]
        refs = refs[5:]
        v_scr = refs[-1]
        halo = up_ref.shape[0]
        row = lax.broadcasted_iota(I32, (tm, 1), 0)
        pos = (pl.program_id(0) * tm + row) & (seq - 1)
        for c0 in range(0, u_ref.shape[1], cw):
            cols = slice(c0, c0 + cw)
            u = u_ref[:, cols].astype(F32)
            u_dn = pltpu.roll(u, 1, 0)
            u_dn = jnp.where(row == 0, up_ref[halo - 1:halo, cols].astype(F32), u_dn)
            u_dn = jnp.where(pos == 0, 0.0, u_dn)
            u_up = pltpu.roll(u, tm - 1, 0)
            u_up = jnp.where(row == tm - 1, un_ref[0:1, cols].astype(F32), u_up)
            u_up = jnp.where(pos == seq - 1, 0.0, u_up)
            y = cw_ref[0:1, cols] * u_dn + cw_ref[1:2, cols] * u + cw_ref[2:3, cols] * u_up
            v_scr[:, cols] = (gb_ref[:, cols].astype(F32) * y).astype(BF16)
        v_ref = v_scr
    else:
        v_ref = refs[0]
        refs = refs[1:]
    w_ref, h_ref, g1_ref, gf_ref, sh_ref, sc_ref, wr_ref, hn_ref, h2_ref, aff_ref = refs[:10]
    half = h_ref.shape[1] // 2
    nck = half // 128
    th = tm // 2
    for r0 in (0, th):
        rows = slice(r0, r0 + th)
        out = jnp.dot(v_ref[rows, :], w_ref[...], preferred_element_type=F32)
        hn = h_ref[rows, :] + g1_ref[0] * out
        hn_ref[rows, :] = hn
        a = _modulate(hn, gf_ref[...], sh_ref[0], sc_ref[0])
        packed = _pack_bf16_pair(a[:, :half], a[:, half:])
        for j in range(nck):
            h2_ref[pl.ds(r0 * nck + j, th, stride=nck), :] = packed[:, j * 128:(j + 1) * 128]
        logits = jnp.dot(a.astype(BF16), wr_ref[...], preferred_element_type=F32)
        lt = logits.T[0:experts, :]
        ex = jnp.exp(lt - jnp.max(lt, axis=0, keepdims=True))
        aff_ref[:, rows] = ex / jnp.sum(ex, axis=0, keepdims=True)


def _mixout(st, v_inputs, conv_w, w_out, h, mods, gain_ffn, w_router):
    rows, d = h.shape
    experts = w_router.shape[1]
    conv = conv_w is not None
    tm, halo = st.tm(512), 16
    nck = d // 256
    assert st.seq & (st.seq - 1) == 0 and tm % 256 == 0
    row = st.ada_row(tm)
    mod = lambda which: pl.BlockSpec((1, 1, d), lambda m: (row(m) * N_ADA + which, 0, 0))
    tile = pl.BlockSpec((tm, d), lambda m: (m, 0))
    const = lambda shape: pl.BlockSpec(shape, lambda m: (0,) * len(shape))
    wr = jnp.zeros((d, 128), BF16).at[:, :experts].set(w_router.astype(BF16))
    if conv:
        gb, u = v_inputs
        per, last = tm // halo, rows // halo - 1
        v_specs = [tile, tile,
                   pl.BlockSpec((halo, d), lambda m: (jnp.maximum(m * per - 1, 0), 0)),
                   pl.BlockSpec((halo, d), lambda m: (jnp.minimum((m + 1) * per, last), 0)),
                   const((3, d))]
        v_args = (gb, u, u, u, conv_w)
        scratch = [pltpu.VMEM((tm, d), BF16)]
    else:
        v_specs, v_args, scratch = [tile], tuple(v_inputs), []
    return pl.pallas_call(
        functools.partial(_mixout_body, conv=conv, seq=st.seq, tm=tm, experts=experts, cw=512),
        grid=(rows // tm,),
        in_specs=v_specs + [
            pl.BlockSpec((d, d), lambda m: (0, 0), pipeline_mode=pl.Buffered(1)),
            tile, mod(2), const((1, d)), mod(3), mod(4), const((d, 128)),
        ],
        out_specs=[
            tile,
            pl.BlockSpec((tm * nck, 128), lambda m: (m, 0)),
            pl.BlockSpec((experts, tm), lambda m: (0, m)),
        ],
        out_shape=[
            jax.ShapeDtypeStruct((rows, d), F32),
            jax.ShapeDtypeStruct((rows * nck, 128), I32),
            jax.ShapeDtypeStruct((experts, rows), F32),
        ],
        scratch_shapes=scratch,
        compiler_params=_params("arbitrary"),
        name="mix_out_router",
    )(*v_args, w_out, h, mods, gain_ffn.reshape(1, d), mods, mods, wr)


def _qkv_body(h_ref, g_ref, sh_ref, sc_ref, w_ref, cq_ref, sq_ref, ck_ref, sk_ref, o_ref, a_scr, *, n_q, tn, rc):
    n = pl.program_id(1)

    @pl.when(n == 0)
    def _():
        a_scr[...] = _modulate(h_ref[...], g_ref[...], sh_ref[0], sc_ref[0]).astype(BF16)

    acc = jnp.dot(a_scr[...], w_ref[...], preferred_element_type=F32)

    def norm_rope(cos_ref, sin_ref):
        lane = lax.broadcasted_iota(I32, (1, HEAD_DIM), 1)
        first = (lane % (HEAD_DIM // 2)) < (HEAD_DIM // 4)
        for r0 in range(0, acc.shape[0], rc):
            cos = cos_ref[r0:r0 + rc, :]
            sin = sin_ref[r0:r0 + rc, :]
            for hd in range(tn // HEAD_DIM):
                lanes = slice(hd * HEAD_DIM, (hd + 1) * HEAD_DIM)
                x = acc[r0:r0 + rc, lanes]
                r = lax.rsqrt(jnp.mean(x * x, axis=-1, keepdims=True) + NORM_EPS)
                rot = jnp.where(first, pltpu.roll(x, HEAD_DIM - HEAD_DIM // 4, 1), pltpu.roll(x, HEAD_DIM // 4, 1))
                o_ref[r0:r0 + rc, lanes] = ((x * cos + rot * sin) * r).astype(BF16)

    @pl.when(n < n_q)
    def _():
        norm_rope(cq_ref, sq_ref)

    @pl.when(n == n_q)
    def _():
        norm_rope(ck_ref, sk_ref)

    @pl.when(n > n_q)
    def _():
        o_ref[...] = acc.astype(BF16)


def _rope_tables(dims, tm, gain, scale):
    quarter = HEAD_DIM // 4
    inv_freq = ROPE_THETA ** (-jnp.arange(quarter, dtype=F32) * 2.0 / (HEAD_DIM // 2))
    t = jnp.arange(dims.seq)
    ang_row = (t // dims.grid_w).astype(F32)[:, None] * inv_freq
    ang_col = (t % dims.grid_w).astype(F32)[:, None] * inv_freq
    cos = jnp.concatenate([jnp.cos(ang_row)] * 2 + [jnp.cos(ang_col)] * 2, axis=-1)
    sin = jnp.concatenate([-jnp.sin(ang_row), jnp.sin(ang_row), -jnp.sin(ang_col), jnp.sin(ang_col)], axis=-1)
    cos = jnp.concatenate([cos, jnp.ones((tm, HEAD_DIM), F32)], axis=0)
    sin = jnp.concatenate([sin, jnp.zeros((tm, HEAD_DIM), F32)], axis=0)
    g = gain.reshape(2, 2, quarter)
    partner_gain = g[:, ::-1, :].reshape(HEAD_DIM)
    return cos * (gain * scale), sin * (partner_gain * scale)


def _qkvproj(dims, st, h, mods, gain, w_qkv, q_gain, k_gain, with_q):
    rows, d = h.shape
    tm, tn = st.tm(), dims.dkv
    n_q = dims.dq // tn if with_q else 0
    col0 = 0 if with_q else dims.dq // tn
    n_total = n_q + 2
    cos_q, sin_q = _rope_tables(dims, tm, q_gain, HEAD_DIM ** -0.5)
    cos_k, sin_k = _rope_tables(dims, tm, k_gain, 1.0)
    per = dims.seq // tm
    pos_tile = (lambda m, n: (m % per, 0)) if with_q else (lambda m, n: (per, 0))
    return pl.pallas_call(
        functools.partial(_qkv_body, n_q=n_q, tn=tn, rc=min(tm, 256)),
        grid=(rows // tm, n_total),
        in_specs=[
            pl.BlockSpec((tm, d), lambda m, n: (m, 0)),
            pl.BlockSpec((1, d), lambda m, n: (0, 0)),
            _mod_spec(st, tm, 0, d),
            _mod_spec(st, tm, 1, d),
            pl.BlockSpec((d, tn), lambda m, n: (0, col0 + n)),
            pl.BlockSpec((tm, HEAD_DIM), pos_tile),
            pl.BlockSpec((tm, HEAD_DIM), pos_tile),
            pl.BlockSpec((tm, HEAD_DIM), pos_tile),
            pl.BlockSpec((tm, HEAD_DIM), pos_tile),
        ],
        out_specs=pl.BlockSpec((tm, tn), lambda m, n: (m, n)),
        out_shape=jax.ShapeDtypeStruct((rows, n_total * tn), BF16),
        scratch_shapes=[pltpu.VMEM((tm, d), BF16)],
        compiler_params=_params("arbitrary", "arbitrary"),
        name="attn_qkv",
    )(h, gain.reshape(1, d), mods, mods, w_qkv, cos_q, sin_q, cos_k, sin_k)


def _attn_body(sink_ref, q_ref, k_ref, v_ref, kc_ref, vc_ref, o_ref, *, dims):
    group = dims.heads // dims.kv_heads
    blk = ATTN_BLOCK
    band = 3 * blk
    n_ctx = dims.ctx
    head0 = pl.program_id(1) * group
    kc = kc_ref[...]
    vc = vc_ref[...]

    n_blk = dims.seq // blk
    assert n_blk >= 3
    sink = jnp.concatenate([jnp.full((blk, 1), sink_ref[head0 + g], F32) for g in range(group)], axis=0)

    def mask_bias(first_key):
        q_rel = first_key + (lax.broadcasted_iota(I32, (group * blk, 1), 0) & (blk - 1))
        col = lax.broadcasted_iota(I32, (1, band + n_ctx), 1)
        valid = (col >= band) | (jnp.abs(q_rel - col) <= blk)
        return jnp.where(valid, 0.0, MASK_VALUE)

    def block(n, start, bias):
        q0 = n * blk if isinstance(n, int) else pl.multiple_of(n * blk, blk)
        kcat = jnp.concatenate([k_ref[pl.ds(start, band), :], kc], axis=0)
        vcat = jnp.concatenate([v_ref[pl.ds(start, band), :], vc], axis=0)
        q = jnp.concatenate([q_ref[pl.ds(q0, blk), g * HEAD_DIM:(g + 1) * HEAD_DIM] for g in range(group)], axis=0)
        s = lax.dot_general(q, kcat, (((1,), (1,)), ((), ())), preferred_element_type=F32) + bias
        m = jnp.maximum(jnp.max(s, axis=-1, keepdims=True), sink)
        p = jnp.exp(s - m)
        den = jnp.sum(p, axis=-1, keepdims=True) + jnp.exp(sink - m)
        o = jnp.dot(p.astype(BF16), vcat, preferred_element_type=F32) / den
        for g in range(group):
            o_ref[pl.ds(q0, blk), g * HEAD_DIM:(g + 1) * HEAD_DIM] = o[g * blk:(g + 1) * blk].astype(BF16)

    block(0, 0, mask_bias(0))
    mid_bias = mask_bias(blk)

    def middle(n, carry):
        block(n, pl.multiple_of((n - 1) * blk, blk), mid_bias)
        return carry

    lax.fori_loop(1, n_blk - 1, middle, 0, unroll=2)
    block(n_blk - 1, dims.seq - band, mask_bias(2 * blk))


def _attention(dims, qkv, kv_ctx, sink):
    group = dims.heads // dims.kv_heads
    gw = group * HEAD_DIM
    k_col = dims.dq // HEAD_DIM
    v_col = (dims.dq + dims.dkv) // HEAD_DIM
    return pl.pallas_call(
        functools.partial(_attn_body, dims=dims),
        grid=(dims.batch, dims.kv_heads),
        in_specs=[
            pl.BlockSpec(memory_space=pltpu.SMEM),
            pl.BlockSpec((dims.seq, gw), lambda b, h: (b, h)),
            pl.BlockSpec((dims.seq, HEAD_DIM), lambda b, h: (b, k_col + h)),
            pl.BlockSpec((dims.seq, HEAD_DIM), lambda b, h: (b, v_col + h)),
            pl.BlockSpec((dims.ctx, HEAD_DIM), lambda b, h: (b, h)),
            pl.BlockSpec((dims.ctx, HEAD_DIM), lambda b, h: (b, dims.kv_heads + h)),
        ],
        out_specs=pl.BlockSpec((dims.seq, gw), lambda b, h: (b, h)),
        out_shape=jax.ShapeDtypeStruct((dims.batch * dims.seq, dims.dq), BF16),
        compiler_params=_params("arbitrary", "arbitrary"),
        name="attn_core",
    )(sink, qkv, qkv, qkv, kv_ctx, kv_ctx)


def _lane_prefix(x, tri):
    n = x.shape[1]
    off = jnp.zeros((x.shape[0], 1), F32)
    chunks = []
    for ch in range(n // 128):
        xc = x[:, ch * 128:(ch + 1) * 128]
        incl = jnp.dot(xc.astype(BF16), tri, preferred_element_type=F32)
        chunks.append(incl - xc + off)
        off = off + incl[:, 127:128]
    return chunks[0] if len(chunks) == 1 else jnp.concatenate(chunks, axis=1)


def _topk_body(aff_ref, idx_ref, dst_ref, gate_ref, off_ref, cnt_ref, w_scr, a_scr, q_scr, clo_scr, chi_scr, *,
               n_tok, cap, pair0, experts):
    grp = pl.program_id(0)
    a = aff_ref[...]
    bits = pltpu.bitcast(a, I32)
    lane = lax.broadcasted_iota(I32, (experts, n_tok), 1)

    def count(mask):
        return jnp.sum(mask.astype(I32), axis=1, keepdims=True)

    def thr_step(i, thr):
        hi = jnp.left_shift(jnp.int32(1), 29 - 2 * i)
        lo = jnp.left_shift(jnp.int32(1), 28 - 2 * i)
        best = thr
        for cand in (thr | lo, thr | hi, thr | hi | lo):
            best = jnp.where(count(bits >= cand) >= cap, cand, best)
        return best

    first = jnp.int32(1 << 30)
    thr = jnp.where(count(bits >= first) >= cap, first, jnp.zeros((experts, 1), I32))
    thr = lax.fori_loop(0, 15, thr_step, thr)
    above = bits > thr
    tie = bits == thr
    need = cap - count(above)

    top_bit = n_tok.bit_length() - 2

    def tie_step(i, lim):
        cand = lim | jnp.left_shift(jnp.int32(1), top_bit - i)
        return jnp.where(count(tie & (lane < cand)) < need, cand, lim)

    lim = lax.fori_loop(0, top_bit + 1, tie_step, jnp.zeros((experts, 1), I32))
    sel = above | (tie & (lane <= lim))
    sel_f = sel.astype(F32)
    sel_i = sel.astype(I32)

    r = lax.broadcasted_iota(I32, (128, 128), 0)
    c = lax.broadcasted_iota(I32, (128, 128), 1)
    tri = (r <= c).astype(F32).astype(BF16)

    level = jnp.zeros((1, n_tok), I32)
    levels = []
    for e in range(experts):
        levels.append(level)
        level = level + sel_i[e:e + 1, :]
    cnt = level
    off = _lane_prefix(cnt.astype(F32), tri).astype(I32) + (pair0 + grp * (experts * cap))
    off_ref[...] = off
    cnt_ref[...] = cnt
    pair_row = jnp.concatenate(levels, axis=0) + off

    n_ch = n_tok // 128
    pad = w_scr.shape[0] // experts
    lane_e = lax.broadcasted_iota(I32, (experts, 128), 1)
    w_scr[...] = jnp.zeros_like(w_scr)
    a_scr[...] = jnp.zeros_like(a_scr)
    q_scr[...] = jnp.zeros_like(q_scr)
    first = jnp.zeros((experts, 1), F32)
    c_lo = jnp.full((experts, 128), cap, I32)
    c_hi = jnp.full((experts, 128), cap, I32)
    for ch in range(n_ch):
        lanes = slice(ch * 128, (ch + 1) * 128)
        incl = jnp.dot(sel_f[:, lanes].astype(BF16), tri, preferred_element_type=F32)
        w_scr[pl.ds(ch, experts, stride=pad), :] = incl
        a_scr[pl.ds(ch, experts, stride=pad), :] = a[:, lanes]
        q_scr[pl.ds(ch, experts, stride=pad), :] = pair_row[:, lanes]
        nxt = first + incl[:, 127:128]
        c_lo = jnp.where(lane_e == ch, first.astype(I32), c_lo)
        c_hi = jnp.where(lane_e == ch, nxt.astype(I32), c_hi)
        first = nxt
    clo_scr[...] = c_lo
    chi_scr[...] = c_hi

    idx_ref[0] = jnp.zeros((cap, 128), I32)
    dst_ref[0] = jnp.zeros((cap, 128), I32)
    gate_ref[0] = jnp.zeros((cap, 128), F32)
    slot = lax.broadcasted_iota(I32, (cap, 1), 0)
    lane = lax.broadcasted_iota(I32, (1, 128), 1)
    zpad = jnp.zeros((128 - pad, 128), F32)

    def rhs(m):
        return jnp.concatenate([m, zpad], axis=0).astype(BF16)

    def per_expert(e, carry):
        base = pl.multiple_of(e * pad, pad)
        wm = w_scr[pl.ds(base, pad), :]
        am = a_scr[pl.ds(base, pad), :]
        qm = q_scr[pl.ds(base, pad), :]
        lo = clo_scr[pl.ds(e, 1), :]
        hi = chi_scr[pl.ds(e, 1), :]
        in_chunk = (lo <= slot) & (slot < hi)
        onehot = in_chunk.astype(F32).astype(BF16)

        def rows_of(m):
            return jnp.dot(onehot, rhs(m), preferred_element_type=F32)

        a1 = am.astype(BF16).astype(F32)
        a2 = (am - a1).astype(BF16).astype(F32)
        a3 = am - a1 - a2
        a_rows = rows_of(a1) + rows_of(a2) + rows_of(a3)
        q_rows = (rows_of(lax.shift_right_logical(qm, 8).astype(F32)) * 256.0
                  + rows_of((qm & 255).astype(F32)))
        w_rows = rows_of(wm)
        chunk_first = jnp.sum(jnp.where(in_chunk, lo, 0), axis=1, keepdims=True)
        chunk = jnp.sum(jnp.where(in_chunk, lane, 0), axis=1, keepdims=True)
        rank = (slot - chunk_first).astype(F32)
        pos = jnp.sum((w_rows <= rank).astype(I32), axis=1, keepdims=True)
        here = lane == pos
        gate = jnp.sum(jnp.where(here, a_rows, 0.0), axis=1, keepdims=True)
        dst = jnp.sum(jnp.where(here, q_rows, 0.0), axis=1, keepdims=True).astype(I32)
        token = chunk * 128 + pos + grp * n_tok
        mine = lane == e
        idx_ref[0] = jnp.where(mine, token, idx_ref[0])
        dst_ref[0] = jnp.where(mine, dst, dst_ref[0])
        gate_ref[0] = jnp.where(mine, gate, gate_ref[0])
        return carry

    lax.fori_loop(0, experts, per_expert, 0)


def _topk(st, aff, cap, pair0):
    e = aff.shape[0]
    n_tok, n_groups = st.seq, st.batch
    pad = max(8, n_tok // 128)
    assert e <= 128 and pad <= 128 and n_groups * e * cap < 2 ** 16 * 256
    tile_spec = pl.BlockSpec((1, cap, 128), lambda g: (g, 0, 0))
    return pl.pallas_call(
        functools.partial(_topk_body, n_tok=n_tok, cap=cap, pair0=pair0, experts=e),
        grid=(n_groups,),
        in_specs=[pl.BlockSpec((e, n_tok), lambda g: (0, g))],
        out_specs=[tile_spec, tile_spec, tile_spec,
                   pl.BlockSpec((1, n_tok), lambda g: (0, g)),
                   pl.BlockSpec((1, n_tok), lambda g: (0, g))],
        out_shape=[jax.ShapeDtypeStruct((n_groups, cap, 128), I32),
                   jax.ShapeDtypeStruct((n_groups, cap, 128), I32),
                   jax.ShapeDtypeStruct((n_groups, cap, 128), F32),
                   jax.ShapeDtypeStruct((1, n_groups * n_tok), I32),
                   jax.ShapeDtypeStruct((1, n_groups * n_tok), I32)],
        scratch_shapes=[pltpu.VMEM((e * pad, 128), F32), pltpu.VMEM((e * pad, 128), F32),
                        pltpu.VMEM((e * pad, 128), I32), pltpu.VMEM((e, 128), I32), pltpu.VMEM((e, 128), I32)],
        compiler_params=_params("arbitrary"),
        name="moe_topk",
    )(aff)


def _expert_body(*refs, n_src, src_slots, slots, rc, n_f, n_n, tf, n_e):
    idx_ref, idx_nxt_ref, dst_ref, dst_prv_ref = refs[0:4]
    srcs = refs[4:4 + n_src]
    gate_ref, wg_ref, wu_ref, wd_ref, pairs_hbm, x_scr, y_scr, mid_scr, gcol_scr, gsem, ssem = refs[4 + n_src:]
    e = pl.program_id(0)
    s = pl.program_id(1)
    nck = x_scr.shape[0] // slots
    half = nck * 128
    tnc = nck // n_n

    def tok(ref, r):
        return ref.at[pl.ds(pl.multiple_of(r * nck, nck), nck), :]

    def row_loop(lo, n, start_row):
        def eight(i, carry):
            for k in range(8):
                start_row(lo + i * 8 + k)
            return carry
        lax.fori_loop(0, n // 8, eight, 0)

    def gather(ref, lo, n):
        for (s0, s1), src in zip(src_slots, srcs):
            a, b = max(lo, s0), min(lo + n, s1)
            if a < b:
                row_loop(a, b - a, lambda r, src=src: pltpu.make_async_copy(
                    tok(src, ref[0, 0, r]), tok(x_scr, r), gsem).start())

    def scatter(ref, lo, n):
        row_loop(lo, n, lambda r: pltpu.make_async_copy(
            tok(y_scr, r), tok(pairs_hbm, ref[0, 0, r]), ssem).start())

    g_steps, s_steps = max(n_n - 1, 1), max(n_f - 1, 1)
    g_rows = -(-slots // g_steps // 8) * 8
    s_rows = -(-slots // s_steps // 8) * 8

    def wait_gather():
        pltpu.make_async_copy(srcs[0].at[pl.ds(0, slots * nck), :], x_scr, gsem).wait()

    def wait_scatter():
        pltpu.make_async_copy(y_scr, pairs_hbm.at[pl.ds(0, slots * nck), :], ssem).wait()

    def x_rows(r0):
        parts = [_unpack_bf16_pair(x_scr[pl.ds(r0 * nck + c, rc, stride=nck), :]) for c in range(nck)]
        return (jnp.concatenate([p[0] for p in parts], axis=1), jnp.concatenate([p[1] for p in parts], axis=1))

    @pl.when((e == 0) & (s == 0))
    def _():
        y_scr[...] = jnp.zeros_like(y_scr)
        gather(idx_ref, 0, slots)

    @pl.when(s == 0)
    def _():
        wait_gather()
        eye = lax.broadcasted_iota(I32, (128, 128), 0) == lax.broadcasted_iota(I32, (128, 128), 1)
        for c in range(slots // 128):
            row = gate_ref[0, c:c + 1, :]
            gcol_scr[c * 128:(c + 1) * 128, :] = jnp.sum(jnp.where(eye, row, 0.0), axis=1, keepdims=True)

    @pl.when(s < n_f)
    def _():
        wg = wg_ref[0, 0].astype(BF16)
        wu = wu_ref[0, 0].astype(BF16)
        for r0 in range(0, slots, rc):
            lo, hi = x_rows(r0)
            a = (jnp.dot(lo, wg[:half], preferred_element_type=F32)
                 + jnp.dot(hi, wg[half:], preferred_element_type=F32))
            u = (jnp.dot(lo, wu[:half], preferred_element_type=F32)
                 + jnp.dot(hi, wu[half:], preferred_element_type=F32))
            mid_scr[s, r0:r0 + rc, :] = (a * jax.nn.sigmoid(a) * u).astype(BF16)
        s_lo = jnp.minimum(s * s_rows, slots)
        scatter(dst_prv_ref, s_lo, jnp.minimum(s_rows, slots - s_lo))

    @pl.when(s == n_f)
    def _():
        wait_scatter()

    for j in range(n_n):
        @pl.when(s == n_f + j)
        def _():
            wd = wd_ref[0, 0].astype(BF16)
            for r0 in range(0, slots, rc):
                y = jnp.dot(mid_scr[0, r0:r0 + rc, :], wd[0:tf], preferred_element_type=F32)
                for f in range(1, n_f):
                    y += jnp.dot(mid_scr[f, r0:r0 + rc, :], wd[f * tf:(f + 1) * tf], preferred_element_type=F32)
                y = y * gcol_scr[r0:r0 + rc, :]
                tnw = tnc * 128
                packed = _pack_bf16_pair(y[:, :tnw], y[:, tnw:])
                for c in range(tnc):
                    y_scr[pl.ds(r0 * nck + j * tnc + c, rc, stride=nck), :] = packed[:, c * 128:(c + 1) * 128]
            g_lo = min(j * g_rows, slots)
            gather(idx_nxt_ref, g_lo, min(g_rows, slots - g_lo))

    @pl.when((e == n_e - 1) & (s == n_f + n_n - 1))
    def _():
        wait_gather()
        scatter(dst_ref, 0, slots)
        wait_scatter()


def _experts(idx, dst, gate, sources, src_slots, w_gate, w_up, w_down, layer, n_pairs):
    n_e, slots = idx.shape
    d, fdim = w_gate.shape[2], w_gate.shape[3]
    tf, tn = 256, EXPERT_DOWN_TILE
    n_f, n_n = fdim // tf, d // tn
    rc = slots // 4
    nck = d // 256
    assert slots % 128 == 0 and rc % 16 == 0 and nck % n_n == 0
    idx3 = idx.reshape(n_e, 1, slots)
    dst3 = dst.reshape(n_e, 1, slots)
    smem = lambda f: pl.BlockSpec((1, 1, slots), f, memory_space=pltpu.SMEM)
    up_chunk = lambda i, s: (layer, i, 0, jnp.minimum(s, n_f - 1))
    return pl.pallas_call(
        functools.partial(_expert_body, n_src=len(sources), src_slots=src_slots, slots=slots, rc=rc,
                          n_f=n_f, n_n=n_n, tf=tf, n_e=n_e),
        grid=(n_e, n_f + n_n),
        in_specs=[
            smem(lambda i, s: (i, 0, 0)),
            smem(lambda i, s: (jnp.minimum(i + 1, n_e - 1), 0, 0)),
            smem(lambda i, s: (i, 0, 0)),
            smem(lambda i, s: (jnp.maximum(i - 1, 0), 0, 0)),
        ] + [pl.BlockSpec(memory_space=pl.ANY)] * len(sources) + [
            pl.BlockSpec((1, slots // 128, 128), lambda i, s: (i, 0, 0)),
            pl.BlockSpec((1, 1, d, tf), up_chunk),
            pl.BlockSpec((1, 1, d, tf), up_chunk),
            pl.BlockSpec((1, 1, fdim, tn), lambda i, s: (layer, i, 0, jnp.maximum(s - n_f, 0))),
        ],
        out_specs=pl.BlockSpec(memory_space=pl.ANY),
        out_shape=jax.ShapeDtypeStruct((n_pairs * nck, 128), I32),
        scratch_shapes=[
            pltpu.VMEM((slots * nck, 128), I32),
            pltpu.VMEM((slots * nck, 128), I32),
            pltpu.VMEM((n_f, slots, tf), BF16),
            pltpu.VMEM((slots, 1), F32),
            pltpu.SemaphoreType.DMA(()),
            pltpu.SemaphoreType.DMA(()),
        ],
        compiler_params=_params("arbitrary", "arbitrary"),
        name="moe_experts",
    )(idx3, idx3, dst3, dst3, *sources, gate.reshape(n_e, slots // 128, 128), w_gate, w_up, w_down)


def _combine_body(lo_ref, hi_ref, off_ref, cnt_ref, p_ref, h_ref, g2_ref, o_ref, acc_lo, acc_hi, *,
                  tt, nck, tnw, pair0, pairs_per_group):
    base = pair0 + pl.program_id(0) * pairs_per_group
    i = pl.program_id(1)
    lane = lax.broadcasted_iota(I32, (1, PAIR_BLOCK), 1)
    eye = lax.broadcasted_iota(I32, (128, 128), 0) == lax.broadcasted_iota(I32, (128, 128), 1)

    def column(row):
        cols = [jnp.sum(jnp.where(eye, row[:, c0:c0 + 128], 0), axis=1, keepdims=True) for c0 in range(0, tt, 128)]
        return cols[0] if len(cols) == 1 else jnp.concatenate(cols, axis=0)

    first = column(off_ref[...]) - base
    last = first + column(cnt_ref[...])
    shift = PAIR_BLOCK.bit_length() - 1
    k0 = lax.shift_right_logical(lo_ref[0, 0, i] - base, shift)
    k1 = lax.shift_right_logical(hi_ref[0, 0, i] - base + (PAIR_BLOCK - 1), shift)
    acc_lo[...] = jnp.zeros_like(acc_lo)
    acc_hi[...] = jnp.zeros_like(acc_hi)

    def step(k, carry):
        p0 = pl.multiple_of(k * PAIR_BLOCK, PAIR_BLOCK)
        parts = [_unpack_bf16_pair(p_ref[pl.ds(p0 * nck + c, PAIR_BLOCK, stride=nck), :]) for c in range(nck)]
        lo = jnp.concatenate([p[0] for p in parts], axis=1)
        hi = jnp.concatenate([p[1] for p in parts], axis=1)
        pr = p0 + lane
        seg = ((pr >= first) & (pr < last)).astype(F32).astype(BF16)
        acc_lo[...] += jnp.dot(seg, lo, preferred_element_type=F32)
        acc_hi[...] += jnp.dot(seg, hi, preferred_element_type=F32)
        return carry

    lax.fori_loop(k0, k1, step, 0)
    for j in range(nck * 128 // tnw):
        words = slice(j * tnw, (j + 1) * tnw)
        c_lo = slice(2 * j * tnw, (2 * j + 1) * tnw)
        c_hi = slice((2 * j + 1) * tnw, (2 * j + 2) * tnw)
        o_ref[:, c_lo] = h_ref[:, c_lo] + g2_ref[0, :, c_lo] * acc_lo[:, words]
        o_ref[:, c_hi] = h_ref[:, c_hi] + g2_ref[0, :, c_hi] * acc_hi[:, words]


def _combine(st, off, cnt, pairs, h, mods, pair0, pairs_per_group):
    rows, d = h.shape
    n_tok = st.seq
    tt = min(256, n_tok)
    n_tiles = n_tok // tt
    nck = d // 256
    assert pairs_per_group % PAIR_BLOCK == 0 and pair0 % pairs_per_group == 0
    off_t = off.reshape(st.batch, 1, n_tiles, tt)
    cnt_t = cnt.reshape(st.batch, 1, n_tiles, tt)
    tile_lo = off_t[..., 0]
    tile_hi = off_t[..., tt - 1] + cnt_t[..., tt - 1]
    row = st.ada_row(n_tok)
    smem = pl.BlockSpec((1, 1, n_tiles), lambda b, i: (b, 0, 0), memory_space=pltpu.SMEM)
    tile = lambda b, i: (b * n_tiles + i, 0)
    return pl.pallas_call(
        functools.partial(_combine_body, tt=tt, nck=nck, tnw=EXPERT_DOWN_TILE // 2, pair0=pair0,
                          pairs_per_group=pairs_per_group),
        grid=(st.batch, n_tiles),
        in_specs=[
            smem, smem,
            pl.BlockSpec((1, tt), lambda b, i: (0, b * n_tiles + i)),
            pl.BlockSpec((1, tt), lambda b, i: (0, b * n_tiles + i)),
            pl.BlockSpec((pairs_per_group * nck, 128), lambda b, i: (pair0 // pairs_per_group + b, 0)),
            pl.BlockSpec((tt, d), tile),
            pl.BlockSpec((1, 1, d), lambda b, i: (row(b) * N_ADA + 5, 0, 0)),
        ],
        out_specs=pl.BlockSpec((tt, d), tile),
        out_shape=jax.ShapeDtypeStruct((rows, d), F32),
        scratch_shapes=[pltpu.VMEM((tt, d // 2), F32)] * 2,
        compiler_params=_params("arbitrary", "arbitrary"),
        name="moe_combine",
    )(tile_lo, tile_hi, off, cnt, pairs, h, mods)


def _moe(streams, mixed, mods, w_gate, w_up, w_down, layer):
    n_e = mixed[0][2].shape[0]
    routed = []
    pair0 = 0
    slot0 = 0
    src_slots = []
    for st, (h, h2, aff) in zip(streams, mixed):
        cap = CAP_FACTOR * st.seq // n_e
        idx, dst, gate, off, cnt = _topk(st, aff, cap, pair0)
        routed.append((h2, idx, dst, gate, off, cnt, pair0, n_e * cap))
        src_slots.append((slot0, slot0 + st.batch * cap))
        pair0 += st.batch * n_e * cap
        slot0 += st.batch * cap
    flat = lambda t: t[:, :, :n_e].transpose(2, 0, 1).reshape(n_e, -1)
    idx = jnp.concatenate([flat(r[1]) for r in routed], axis=1)
    dst = jnp.concatenate([flat(r[2]) for r in routed], axis=1)
    gate = jnp.concatenate([flat(r[3]) for r in routed], axis=1)
    pairs = _experts(idx, dst, gate, [r[0] for r in routed], tuple(src_slots), w_gate, w_up, w_down, layer, pair0)
    return [_combine(st, r[4], r[5], pairs, m[0], mods, r[6], r[7]) for st, m, r in zip(streams, mixed, routed)]


def _forward(dims, x, c, ctx, c_ctx, ada_w, ada_b, norm_mix_g, norm_ffn_g, conv_w_in, conv_w, conv_w_out,
             attn_w_qkv, attn_q_gain, attn_k_gain, attn_sink, attn_w_o, router_w,
             expert_w_gate, expert_w_up, expert_w_down):
    d = dims.d
    lat = Stream(dims.batch, dims.seq)
    con = Stream(dims.batch, dims.ctx, shared_row=dims.batch)
    cvec = jnp.zeros((ADA_ROWS, d), F32).at[:dims.batch].set(c).at[dims.batch].set(c_ctx)
    mods = _adaln(cvec, ada_w, ada_b)
    mods = mods.reshape(mods.shape[0], ADA_ROWS * N_ADA, 1, d)
    hs = [x.reshape(lat.rows, d), ctx.reshape(con.rows, d)]

    w_in, w_out = conv_w_in[0].astype(BF16), conv_w_out[0].astype(BF16)
    mixed = []
    for st, h in zip((lat, con), hs):
        gb_u = _inproj(st, h, mods[0], norm_mix_g[0], w_in)
        mixed.append(_mixout(st, gb_u, conv_w[0], w_out, h, mods[0], norm_ffn_g[0], router_w[0]))
    h_lat, h_ctx = _moe((lat, con), mixed, mods[0], expert_w_gate, expert_w_up, expert_w_down, 0)

    w_qkv, w_o = attn_w_qkv[0].astype(BF16), attn_w_o[0].astype(BF16)
    qkv = _qkvproj(dims, lat, h_lat, mods[1], norm_mix_g[1], w_qkv, attn_q_gain[0], attn_k_gain[0], True)
    kv_ctx = _qkvproj(dims, con, h_ctx, mods[1], norm_mix_g[1], w_qkv, attn_q_gain[0], attn_k_gain[0], False)
    o = _attention(dims, qkv, kv_ctx, attn_sink[0])
    mixed = [_mixout(lat, (o,), None, w_o, h_lat, mods[1], norm_ffn_g[1], router_w[1])]
    (h_lat,) = _moe((lat,), mixed, mods[1], expert_w_gate, expert_w_up, expert_w_down, 1)
    return h_lat.reshape(dims.batch, dims.seq, d)


def kernel(x, c, ctx, c_ctx, ada_w, ada_b, norm_mix_g, norm_ffn_g, conv_w_in, conv_w, conv_w_out, attn_w_qkv, attn_q_gain, attn_k_gain, attn_sink, attn_w_o, router_w, expert_w_gate, expert_w_up, expert_w_down):
    batch, seq, d = x.shape
    dims = Dims(d=d, batch=batch, seq=seq, grid_w=GRID_W, ctx=ctx.shape[1], heads=d // HEAD_DIM,
                kv_heads=(attn_w_qkv.shape[2] // HEAD_DIM - d // HEAD_DIM) // 2,
                experts=router_w.shape[2], d_expert=expert_w_gate.shape[3])
    return _forward(dims, x, c, ctx, c_ctx, ada_w, ada_b, norm_mix_g, norm_ffn_g, conv_w_in, conv_w, conv_w_out,
                    attn_w_qkv, attn_q_gain, attn_k_gain, attn_sink, attn_w_o, router_w,
                    expert_w_gate, expert_w_up, expert_w_down)
```

```python
import dataclasses
import functools

import jax
import jax.numpy as jnp
from jax import lax
from jax.experimental import pallas as pl
from jax.experimental.pallas import tpu as pltpu

F32 = jnp.float32
BF16 = jnp.bfloat16
I32 = jnp.int32

NORM_EPS = 1e-6
MASK_VALUE = -1e30
ROPE_THETA = 10000.0
GRID_W = 64
CAP_FACTOR = 2
HEAD_DIM = 128
ATTN_BLOCK = 128
N_ADA = 6
ADA_ROWS = 16
HI_MASK = -65536
PAIR_BLOCK = 256
EXPERT_DOWN_TILE = 512
VMEM_LIMIT = 58 * 1024 * 1024


@dataclasses.dataclass(frozen=True)
class Dims:
    d: int
    batch: int
    seq: int
    grid_w: int
    ctx: int
    heads: int
    kv_heads: int
    experts: int
    d_expert: int

    @property
    def dq(self):
        return self.heads * HEAD_DIM

    @property
    def dkv(self):
        return self.kv_heads * HEAD_DIM


@dataclasses.dataclass(frozen=True)
class Stream:
    batch: int
    seq: int
    shared_row: int = -1

    @property
    def rows(self):
        return self.batch * self.seq

    def tm(self, cap=1024):
        unit = self.rows if self.shared_row >= 0 else self.seq
        t = cap
        while unit % t:
            t //= 2
        return t

    def ada_row(self, tm):
        if self.shared_row >= 0:
            return lambda m: self.shared_row
        per = self.seq // tm
        assert per >= 1
        return lambda m: m // per


def _params(*sem):
    return pltpu.CompilerParams(dimension_semantics=sem, vmem_limit_bytes=VMEM_LIMIT)


def _modulate(h, gain, shift, scale):
    ms = jnp.mean(h * h, axis=-1, keepdims=True)
    xn = h * lax.rsqrt(ms + NORM_EPS)
    return (xn * gain) * (1.0 + scale) + shift


def _mod_spec(st, tm, which, d, width=None):
    row = st.ada_row(tm)
    if width is None:
        return pl.BlockSpec((1, 1, d), lambda m, n: (row(m) * N_ADA + which, 0, 0))
    return pl.BlockSpec((1, 1, width), lambda m, n: (row(m) * N_ADA + which, 0, n))


def _pack_bf16_pair(lo, hi):
    lo_bits = pltpu.bitcast(lo.astype(BF16).astype(F32), I32)
    hi_bits = pltpu.bitcast(hi.astype(BF16).astype(F32), I32)
    return (hi_bits & HI_MASK) | lax.shift_right_logical(lo_bits, 16)


def _unpack_bf16_pair(w):
    lo = pltpu.bitcast(lax.shift_left(w, 16), F32).astype(BF16)
    hi = pltpu.bitcast(w & HI_MASK, F32).astype(BF16)
    return lo, hi


def _adaln_body(c_ref, w_ref, b_ref, o_ref):
    c = c_ref[...]
    s = c * jax.nn.sigmoid(c)
    o_ref[0] = jnp.dot(s.astype(BF16), w_ref[0].astype(BF16), preferred_element_type=F32) + b_ref[0]


def _adaln(cvec, ada_w, ada_b):
    depth, d, n = ada_w.shape
    tn = min(1024, n)
    return pl.pallas_call(
        _adaln_body,
        grid=(depth, n // tn),
        in_specs=[
            pl.BlockSpec((ADA_ROWS, d), lambda l, j: (0, 0)),
            pl.BlockSpec((1, d, tn), lambda l, j: (l, 0, j)),
            pl.BlockSpec((1, 1, tn), lambda l, j: (l, 0, j)),
        ],
        out_specs=pl.BlockSpec((1, ADA_ROWS, tn), lambda l, j: (l, 0, j)),
        out_shape=jax.ShapeDtypeStruct((depth, ADA_ROWS, n), F32),
        compiler_params=_params("arbitrary", "arbitrary"),
        name="adaln",
    )(cvec, ada_w, ada_b.reshape(depth, 1, n))


def _inproj_body(h_ref, g_ref, sh_ref, sc_ref, wb_ref, wc_ref, wx_ref, gb_ref, u_ref, a_scr):
    @pl.when(pl.program_id(1) == 0)
    def _():
        a_scr[...] = _modulate(h_ref[...], g_ref[...], sh_ref[0], sc_ref[0]).astype(BF16)

    a = a_scr[...]
    gb = jnp.dot(a, wb_ref[...], preferred_element_type=F32)
    gc = jnp.dot(a, wc_ref[...], preferred_element_type=F32)
    xv = jnp.dot(a, wx_ref[...], preferred_element_type=F32)
    gb_ref[...] = gb.astype(BF16)
    u_ref[...] = (gc * xv).astype(BF16)


def _inproj(st, h, mods, gain, w_in):
    rows, d = h.shape
    tm, tn = st.tm(), 512
    nt = d // tn
    return pl.pallas_call(
        _inproj_body,
        grid=(rows // tm, nt),
        in_specs=[
            pl.BlockSpec((tm, d), lambda m, n: (m, 0)),
            pl.BlockSpec((1, d), lambda m, n: (0, 0)),
            _mod_spec(st, tm, 0, d),
            _mod_spec(st, tm, 1, d),
            pl.BlockSpec((d, tn), lambda m, n: (0, n)),
            pl.BlockSpec((d, tn), lambda m, n: (0, nt + n)),
            pl.BlockSpec((d, tn), lambda m, n: (0, 2 * nt + n)),
        ],
        out_specs=[pl.BlockSpec((tm, tn), lambda m, n: (m, n))] * 2,
        out_shape=[jax.ShapeDtypeStruct((rows, d), BF16)] * 2,
        scratch_shapes=[pltpu.VMEM((tm, d), BF16)],
        compiler_params=_params("arbitrary", "arbitrary"),
        name="conv_inproj",
    )(h, gain.reshape(1, d), mods, mods, w_in, w_in, w_in)


def _mixout_body(*refs, conv, seq, tm, experts, cw):
    if conv:
        gb_ref, u_ref, up_ref, un_ref, cw_ref = refs[:5]
        refs = refs[5:]
        v_scr = refs[-1]
        halo = up_ref.shape[0]
        row = lax.broadcasted_iota(I32, (tm, 1), 0)
        pos = (pl.program_id(0) * tm + row) & (seq - 1)
        for c0 in range(0, u_ref.shape[1], cw):
            cols = slice(c0, c0 + cw)
            u = u_ref[:, cols].astype(F32)
            u_dn = pltpu.roll(u, 1, 0)
            u_dn = jnp.where(row == 0, up_ref[halo - 1:halo, cols].astype(F32), u_dn)
            u_dn = jnp.where(pos == 0, 0.0, u_dn)
            u_up = pltpu.roll(u, tm - 1, 0)
            u_up = jnp.where(row == tm - 1, un_ref[0:1, cols].astype(F32), u_up)
            u_up = jnp.where(pos == seq - 1, 0.0, u_up)
            y = cw_ref[0:1, cols] * u_dn + cw_ref[1:2, cols] * u + cw_ref[2:3, cols] * u_up
            v_scr[:, cols] = (gb_ref[:, cols].astype(F32) * y).astype(BF16)
        v_ref = v_scr
    else:
        v_ref = refs[0]
        refs = refs[1:]
    w_ref, h_ref, g1_ref, gf_ref, sh_ref, sc_ref, wr_ref, hn_ref, h2_ref, aff_ref = refs[:10]
    half = h_ref.shape[1] // 2
    nck = half // 128
    th = tm // 2
    for r0 in (0, th):
        rows = slice(r0, r0 + th)
        out = jnp.dot(v_ref[rows, :], w_ref[...], preferred_element_type=F32)
        hn = h_ref[rows, :] + g1_ref[0] * out
        hn_ref[rows, :] = hn
        a = _modulate(hn, gf_ref[...], sh_ref[0], sc_ref[0])
        packed = _pack_bf16_pair(a[:, :half], a[:, half:])
        for j in range(nck):
            h2_ref[pl.ds(r0 * nck + j, th, stride=nck), :] = packed[:, j * 128:(j + 1) * 128]
        logits = jnp.dot(a.astype(BF16), wr_ref[...], preferred_element_type=F32)
        lt = logits.T[0:experts, :]
        ex = jnp.exp(lt - jnp.max(lt, axis=0, keepdims=True))
        aff_ref[:, rows] = ex / jnp.sum(ex, axis=0, keepdims=True)


def _mixout(st, v_inputs, conv_w, w_out, h, mods, gain_ffn, w_router):
    rows, d = h.shape
    experts = w_router.shape[1]
    conv = conv_w is not None
    tm, halo = st.tm(512), 16
    nck = d // 256
    assert st.seq & (st.seq - 1) == 0 and tm % 256 == 0
    row = st.ada_row(tm)
    mod = lambda which: pl.BlockSpec((1, 1, d), lambda m: (row(m) * N_ADA + which, 0, 0))
    tile = pl.BlockSpec((tm, d), lambda m: (m, 0))
    const = lambda shape: pl.BlockSpec(shape, lambda m: (0,) * len(shape))
    wr = jnp.zeros((d, 128), BF16).at[:, :experts].set(w_router.astype(BF16))
    if conv:
        gb, u = v_inputs
        per, last = tm // halo, rows // halo - 1
        v_specs = [tile, tile,
                   pl.BlockSpec((halo, d), lambda m: (jnp.maximum(m * per - 1, 0), 0)),
                   pl.BlockSpec((halo, d), lambda m: (jnp.minimum((m + 1) * per, last), 0)),
                   const((3, d))]
        v_args = (gb, u, u, u, conv_w)
        scratch = [pltpu.VMEM((tm, d), BF16)]
    else:
        v_specs, v_args, scratch = [tile], tuple(v_inputs), []
    return pl.pallas_call(
        functools.partial(_mixout_body, conv=conv, seq=st.seq, tm=tm, experts=experts, cw=512),
        grid=(rows // tm,),
        in_specs=v_specs + [
            pl.BlockSpec((d, d), lambda m: (0, 0), pipeline_mode=pl.Buffered(1)),
            tile, mod(2), const((1, d)), mod(3), mod(4), const((d, 128)),
        ],
        out_specs=[
            tile,
            pl.BlockSpec((tm * nck, 128), lambda m: (m, 0)),
            pl.BlockSpec((experts, tm), lambda m: (0, m)),
        ],
        out_shape=[
            jax.ShapeDtypeStruct((rows, d), F32),
            jax.ShapeDtypeStruct((rows * nck, 128), I32),
            jax.ShapeDtypeStruct((experts, rows), F32),
        ],
        scratch_shapes=scratch,
        compiler_params=_params("arbitrary"),
        name="mix_out_router",
    )(*v_args, w_out, h, mods, gain_ffn.reshape(1, d), mods, mods, wr)


def _qkv_body(h_ref, g_ref, sh_ref, sc_ref, w_ref, cq_ref, sq_ref, ck_ref, sk_ref, o_ref, a_scr, *, n_q, tn, rc):
    n = pl.program_id(1)

    @pl.when(n == 0)
    def _():
        a_scr[...] = _modulate(h_ref[...], g_ref[...], sh_ref[0], sc_ref[0]).astype(BF16)

    acc = jnp.dot(a_scr[...], w_ref[...], preferred_element_type=F32)

    def norm_rope(cos_ref, sin_ref):
        lane = lax.broadcasted_iota(I32, (1, HEAD_DIM), 1)
        first = (lane % (HEAD_DIM // 2)) < (HEAD_DIM // 4)
        for r0 in range(0, acc.shape[0], rc):
            cos = cos_ref[r0:r0 + rc, :]
            sin = sin_ref[r0:r0 + rc, :]
            for hd in range(tn // HEAD_DIM):
                lanes = slice(hd * HEAD_DIM, (hd + 1) * HEAD_DIM)
                x = acc[r0:r0 + rc, lanes]
                r = lax.rsqrt(jnp.mean(x * x, axis=-1, keepdims=True) + NORM_EPS)
                rot = jnp.where(first, pltpu.roll(x, HEAD_DIM - HEAD_DIM // 4, 1), pltpu.roll(x, HEAD_DIM // 4, 1))
                o_ref[r0:r0 + rc, lanes] = ((x * cos + rot * sin) * r).astype(BF16)

    @pl.when(n < n_q)
    def _():
        norm_rope(cq_ref, sq_ref)

    @pl.when(n == n_q)
    def _():
        norm_rope(ck_ref, sk_ref)

    @pl.when(n > n_q)
    def _():
        o_ref[...] = acc.astype(BF16)


def _rope_tables(dims, tm, gain, scale):
    quarter = HEAD_DIM // 4
    inv_freq = ROPE_THETA ** (-jnp.arange(quarter, dtype=F32) * 2.0 / (HEAD_DIM // 2))
    t = jnp.arange(dims.seq)
    ang_row = (t // dims.grid_w).astype(F32)[:, None] * inv_freq
    ang_col = (t % dims.grid_w).astype(F32)[:, None] * inv_freq
    cos = jnp.concatenate([jnp.cos(ang_row)] * 2 + [jnp.cos(ang_col)] * 2, axis=-1)
    sin = jnp.concatenate([-jnp.sin(ang_row), jnp.sin(ang_row), -jnp.sin(ang_col), jnp.sin(ang_col)], axis=-1)
    cos = jnp.concatenate([cos, jnp.ones((tm, HEAD_DIM), F32)], axis=0)
    sin = jnp.concatenate([sin, jnp.zeros((tm, HEAD_DIM), F32)], axis=0)
    g = gain.reshape(2, 2, quarter)
    partner_gain = g[:, ::-1, :].reshape(HEAD_DIM)
    return cos * (gain * scale), sin * (partner_gain * scale)


def _qkvproj(dims, st, h, mods, gain, w_qkv, q_gain, k_gain, with_q):
    rows, d = h.shape
    tm, tn = st.tm(), dims.dkv
    n_q = dims.dq // tn if with_q else 0
    col0 = 0 if with_q else dims.dq // tn
    n_total = n_q + 2
    cos_q, sin_q = _rope_tables(dims, tm, q_gain, HEAD_DIM ** -0.5)
    cos_k, sin_k = _rope_tables(dims, tm, k_gain, 1.0)
    per = dims.seq // tm
    pos_tile = (lambda m, n: (m % per, 0)) if with_q else (lambda m, n: (per, 0))
    return pl.pallas_call(
        functools.partial(_qkv_body, n_q=n_q, tn=tn, rc=tm),
        grid=(rows // tm, n_total),
        in_specs=[
            pl.BlockSpec((tm, d), lambda m, n: (m, 0)),
            pl.BlockSpec((1, d), lambda m, n: (0, 0)),
            _mod_spec(st, tm, 0, d),
            _mod_spec(st, tm, 1, d),
            pl.BlockSpec((d, tn), lambda m, n: (0, col0 + n)),
            pl.BlockSpec((tm, HEAD_DIM), pos_tile),
            pl.BlockSpec((tm, HEAD_DIM), pos_tile),
            pl.BlockSpec((tm, HEAD_DIM), pos_tile),
            pl.BlockSpec((tm, HEAD_DIM), pos_tile),
        ],
        out_specs=pl.BlockSpec((tm, tn), lambda m, n: (m, n)),
        out_shape=jax.ShapeDtypeStruct((rows, n_total * tn), BF16),
        scratch_shapes=[pltpu.VMEM((tm, d), BF16)],
        compiler_params=_params("arbitrary", "arbitrary"),
        name="attn_qkv",
    )(h, gain.reshape(1, d), mods, mods, w_qkv, cos_q, sin_q, cos_k, sin_k)


def _attn_body(sink_ref, q_ref, k_ref, v_ref, kc_ref, vc_ref, o_ref, *, dims):
    group = dims.heads // dims.kv_heads
    blk = ATTN_BLOCK
    band = 3 * blk
    n_ctx = dims.ctx
    head0 = pl.program_id(1) * group
    kc = kc_ref[...]
    vc = vc_ref[...]

    n_blk = dims.seq // blk
    assert n_blk >= 3
    sink = jnp.concatenate([jnp.full((blk, 1), sink_ref[head0 + g], F32) for g in range(group)], axis=0)

    def mask_bias(first_key):
        q_rel = first_key + (lax.broadcasted_iota(I32, (group * blk, 1), 0) & (blk - 1))
        col = lax.broadcasted_iota(I32, (1, band + n_ctx), 1)
        valid = (col >= band) | (jnp.abs(q_rel - col) <= blk)
        return jnp.where(valid, 0.0, MASK_VALUE)

    def block(n, start, bias):
        q0 = n * blk if isinstance(n, int) else pl.multiple_of(n * blk, blk)
        kcat = jnp.concatenate([k_ref[pl.ds(start, band), :], kc], axis=0)
        vcat = jnp.concatenate([v_ref[pl.ds(start, band), :], vc], axis=0)
        q = jnp.concatenate([q_ref[pl.ds(q0, blk), g * HEAD_DIM:(g + 1) * HEAD_DIM] for g in range(group)], axis=0)
        s = lax.dot_general(q, kcat, (((1,), (1,)), ((), ())), preferred_element_type=F32) + bias
        m = jnp.maximum(jnp.max(s, axis=-1, keepdims=True), sink)
        p = jnp.exp(s - m)
        den = jnp.sum(p, axis=-1, keepdims=True) + jnp.exp(sink - m)
        o = jnp.dot(p.astype(BF16), vcat, preferred_element_type=F32) / den
        for g in range(group):
            o_ref[pl.ds(q0, blk), g * HEAD_DIM:(g + 1) * HEAD_DIM] = o[g * blk:(g + 1) * blk].astype(BF16)

    block(0, 0, mask_bias(0))
    mid_bias = mask_bias(blk)

    def middle(n, carry):
        block(n, pl.multiple_of((n - 1) * blk, blk), mid_bias)
        return carry

    lax.fori_loop(1, n_blk - 1, middle, 0, unroll=2)
    block(n_blk - 1, dims.seq - band, mask_bias(2 * blk))


def _attention(dims, qkv, kv_ctx, sink):
    group = dims.heads // dims.kv_heads
    gw = group * HEAD_DIM
    k_col = dims.dq // HEAD_DIM
    v_col = (dims.dq + dims.dkv) // HEAD_DIM
    return pl.pallas_call(
        functools.partial(_attn_body, dims=dims),
        grid=(dims.batch, dims.kv_heads),
        in_specs=[
            pl.BlockSpec(memory_space=pltpu.SMEM),
            pl.BlockSpec((dims.seq, gw), lambda b, h: (b, h)),
            pl.BlockSpec((dims.seq, HEAD_DIM), lambda b, h: (b, k_col + h)),
            pl.BlockSpec((dims.seq, HEAD_DIM), lambda b, h: (b, v_col + h)),
            pl.BlockSpec((dims.ctx, HEAD_DIM), lambda b, h: (b, h)),
            pl.BlockSpec((dims.ctx, HEAD_DIM), lambda b, h: (b, dims.kv_heads + h)),
        ],
        out_specs=pl.BlockSpec((dims.seq, gw), lambda b, h: (b, h)),
        out_shape=jax.ShapeDtypeStruct((dims.batch * dims.seq, dims.dq), BF16),
        compiler_params=_params("arbitrary", "arbitrary"),
        name="attn_core",
    )(sink, qkv, qkv, qkv, kv_ctx, kv_ctx)


def _lane_prefix(x, tri):
    n = x.shape[1]
    off = jnp.zeros((x.shape[0], 1), F32)
    chunks = []
    for ch in range(n // 128):
        xc = x[:, ch * 128:(ch + 1) * 128]
        incl = jnp.dot(xc.astype(BF16), tri, preferred_element_type=F32)
        chunks.append(incl - xc + off)
        off = off + incl[:, 127:128]
    return chunks[0] if len(chunks) == 1 else jnp.concatenate(chunks, axis=1)


def _topk_body(aff_ref, idx_ref, dst_ref, gate_ref, off_ref, cnt_ref, w_scr, a_scr, q_scr, clo_scr, chi_scr, *,
               n_tok, cap, pair0, experts):
    grp = pl.program_id(0)
    a = aff_ref[...]
    bits = pltpu.bitcast(a, I32)
    lane = lax.broadcasted_iota(I32, (experts, n_tok), 1)

    def count(mask):
        return jnp.sum(mask.astype(I32), axis=1, keepdims=True)

    def thr_step(i, thr):
        hi = jnp.left_shift(jnp.int32(1), 29 - 2 * i)
        lo = jnp.left_shift(jnp.int32(1), 28 - 2 * i)
        best = thr
        for cand in (thr | lo, thr | hi, thr | hi | lo):
            best = jnp.where(count(bits >= cand) >= cap, cand, best)
        return best

    first = jnp.int32(1 << 30)
    thr = jnp.where(count(bits >= first) >= cap, first, jnp.zeros((experts, 1), I32))
    thr = lax.fori_loop(0, 15, thr_step, thr)
    above = bits > thr
    tie = bits == thr
    need = cap - count(above)

    top_bit = n_tok.bit_length() - 2

    def tie_step(i, lim):
        cand = lim | jnp.left_shift(jnp.int32(1), top_bit - i)
        return jnp.where(count(tie & (lane < cand)) < need, cand, lim)

    lim = lax.fori_loop(0, top_bit + 1, tie_step, jnp.zeros((experts, 1), I32))
    sel = above | (tie & (lane <= lim))
    sel_f = sel.astype(F32)
    sel_i = sel.astype(I32)

    r = lax.broadcasted_iota(I32, (128, 128), 0)
    c = lax.broadcasted_iota(I32, (128, 128), 1)
    tri = (r <= c).astype(F32).astype(BF16)

    level = jnp.zeros((1, n_tok), I32)
    levels = []
    for e in range(experts):
        levels.append(level)
        level = level + sel_i[e:e + 1, :]
    cnt = level
    off = _lane_prefix(cnt.astype(F32), tri).astype(I32) + (pair0 + grp * (experts * cap))
    off_ref[...] = off
    cnt_ref[...] = cnt
    pair_row = jnp.concatenate(levels, axis=0) + off

    n_ch = n_tok // 128
    pad = w_scr.shape[0] // experts
    lane_e = lax.broadcasted_iota(I32, (experts, 128), 1)
    w_scr[...] = jnp.zeros_like(w_scr)
    a_scr[...] = jnp.zeros_like(a_scr)
    q_scr[...] = jnp.zeros_like(q_scr)
    first = jnp.zeros((experts, 1), F32)
    c_lo = jnp.full((experts, 128), cap, I32)
    c_hi = jnp.full((experts, 128), cap, I32)
    for ch in range(n_ch):
        lanes = slice(ch * 128, (ch + 1) * 128)
        incl = jnp.dot(sel_f[:, lanes].astype(BF16), tri, preferred_element_type=F32)
        w_scr[pl.ds(ch, experts, stride=pad), :] = incl
        a_scr[pl.ds(ch, experts, stride=pad), :] = a[:, lanes]
        q_scr[pl.ds(ch, experts, stride=pad), :] = pair_row[:, lanes]
        nxt = first + incl[:, 127:128]
        c_lo = jnp.where(lane_e == ch, first.astype(I32), c_lo)
        c_hi = jnp.where(lane_e == ch, nxt.astype(I32), c_hi)
        first = nxt
    clo_scr[...] = c_lo
    chi_scr[...] = c_hi

    idx_ref[0] = jnp.zeros((cap, 128), I32)
    dst_ref[0] = jnp.zeros((cap, 128), I32)
    gate_ref[0] = jnp.zeros((cap, 128), F32)
    slot = lax.broadcasted_iota(I32, (cap, 1), 0)
    lane = lax.broadcasted_iota(I32, (1, 128), 1)
    zpad = jnp.zeros((128 - pad, 128), F32)

    def rhs(m):
        return jnp.concatenate([m, zpad], axis=0).astype(BF16)

    def per_expert(e, carry):
        base = pl.multiple_of(e * pad, pad)
        wm = w_scr[pl.ds(base, pad), :]
        am = a_scr[pl.ds(base, pad), :]
        qm = q_scr[pl.ds(base, pad), :]
        lo = clo_scr[pl.ds(e, 1), :]
        hi = chi_scr[pl.ds(e, 1), :]
        in_chunk = (lo <= slot) & (slot < hi)
        onehot = in_chunk.astype(F32).astype(BF16)

        def rows_of(m):
            return jnp.dot(onehot, rhs(m), preferred_element_type=F32)

        a1 = am.astype(BF16).astype(F32)
        a2 = (am - a1).astype(BF16).astype(F32)
        a3 = am - a1 - a2
        a_rows = rows_of(a1) + rows_of(a2) + rows_of(a3)
        q_rows = (rows_of(lax.shift_right_logical(qm, 8).astype(F32)) * 256.0
                  + rows_of((qm & 255).astype(F32)))
        w_rows = rows_of(wm)
        chunk_first = jnp.sum(jnp.where(in_chunk, lo, 0), axis=1, keepdims=True)
        chunk = jnp.sum(jnp.where(in_chunk, lane, 0), axis=1, keepdims=True)
        rank = (slot - chunk_first).astype(F32)
        pos = jnp.sum((w_rows <= rank).astype(I32), axis=1, keepdims=True)
        here = lane == pos
        gate = jnp.sum(jnp.where(here, a_rows, 0.0), axis=1, keepdims=True)
        dst = jnp.sum(jnp.where(here, q_rows, 0.0), axis=1, keepdims=True).astype(I32)
        token = chunk * 128 + pos + grp * n_tok
        mine = lane == e
        idx_ref[0] = jnp.where(mine, token, idx_ref[0])
        dst_ref[0] = jnp.where(mine, dst, dst_ref[0])
        gate_ref[0] = jnp.where(mine, gate, gate_ref[0])
        return carry

    lax.fori_loop(0, experts, per_expert, 0)


def _topk(st, aff, cap, pair0):
    e = aff.shape[0]
    n_tok, n_groups = st.seq, st.batch
    pad = max(8, n_tok // 128)
    assert e <= 128 and pad <= 128 and n_groups * e * cap < 2 ** 16 * 256
    tile_spec = pl.BlockSpec((1, cap, 128), lambda g: (g, 0, 0))
    return pl.pallas_call(
        functools.partial(_topk_body, n_tok=n_tok, cap=cap, pair0=pair0, experts=e),
        grid=(n_groups,),
        in_specs=[pl.BlockSpec((e, n_tok), lambda g: (0, g))],
        out_specs=[tile_spec, tile_spec, tile_spec,
                   pl.BlockSpec((1, n_tok), lambda g: (0, g)),
                   pl.BlockSpec((1, n_tok), lambda g: (0, g))],
        out_shape=[jax.ShapeDtypeStruct((n_groups, cap, 128), I32),
                   jax.ShapeDtypeStruct((n_groups, cap, 128), I32),
                   jax.ShapeDtypeStruct((n_groups, cap, 128), F32),
                   jax.ShapeDtypeStruct((1, n_groups * n_tok), I32),
                   jax.ShapeDtypeStruct((1, n_groups * n_tok), I32)],
        scratch_shapes=[pltpu.VMEM((e * pad, 128), F32), pltpu.VMEM((e * pad, 128), F32),
                        pltpu.VMEM((e * pad, 128), I32), pltpu.VMEM((e, 128), I32), pltpu.VMEM((e, 128), I32)],
        compiler_params=_params("arbitrary"),
        name="moe_topk",
    )(aff)


def _expert_body(*refs, n_src, src_slots, slots, rc, n_f, n_n, tf, n_e):
    idx_ref, idx_nxt_ref, dst_ref, dst_prv_ref = refs[0:4]
    srcs = refs[4:4 + n_src]
    gate_ref, wg_ref, wu_ref, wd_ref, pairs_hbm, x_scr, y_scr, mid_scr, gcol_scr, gsem, ssem = refs[4 + n_src:]
    e = pl.program_id(0)
    s = pl.program_id(1)
    nck = x_scr.shape[0] // slots
    half = nck * 128
    tnc = nck // n_n

    def tok(ref, r):
        return ref.at[pl.ds(pl.multiple_of(r * nck, nck), nck), :]

    def row_loop(lo, n, start_row):
        def eight(i, carry):
            for k in range(8):
                start_row(lo + i * 8 + k)
            return carry
        lax.fori_loop(0, n // 8, eight, 0)

    def gather_row(ref, r, slot):
        src = next(src for (s0, s1), src in zip(src_slots, srcs) if s0 <= slot < s1)
        pltpu.make_async_copy(tok(src, ref[0, 0, r]), tok(x_scr, r), gsem).start()

    def scatter_row(ref, r):
        pltpu.make_async_copy(tok(y_scr, r), tok(pairs_hbm, ref[0, 0, r]), ssem).start()

    n_chunks = slots // rc
    dma_chunks = max(n_chunks - 1, 1)

    def share(total, ci):
        base, extra = divmod(total, dma_chunks)
        if ci >= dma_chunks:
            return total, 0
        return ci * base + min(ci, extra), base + (1 if ci < extra else 0)

    def wait_gather():
        pltpu.make_async_copy(srcs[0].at[pl.ds(0, slots * nck), :], x_scr, gsem).wait()

    def wait_scatter():
        pltpu.make_async_copy(y_scr, pairs_hbm.at[pl.ds(0, slots * nck), :], ssem).wait()

    def x_rows(r0):
        parts = [_unpack_bf16_pair(x_scr[pl.ds(r0 * nck + c, rc, stride=nck), :]) for c in range(nck)]
        return (jnp.concatenate([p[0] for p in parts], axis=1), jnp.concatenate([p[1] for p in parts], axis=1))

    @pl.when((e == 0) & (s == 0))
    def _():
        y_scr[...] = jnp.zeros_like(y_scr)
        for (s0, s1), src in zip(src_slots, srcs):
            row_loop(s0, s1 - s0, lambda r, src=src: pltpu.make_async_copy(
                tok(src, idx_ref[0, 0, r]), tok(x_scr, r), gsem).start())

    @pl.when(s == 0)
    def _():
        wait_gather()
        eye = lax.broadcasted_iota(I32, (128, 128), 0) == lax.broadcasted_iota(I32, (128, 128), 1)
        for c in range(slots // 128):
            row = gate_ref[0, c:c + 1, :]
            gcol_scr[c * 128:(c + 1) * 128, :] = jnp.sum(jnp.where(eye, row, 0.0), axis=1, keepdims=True)

    @pl.when(s < n_f)
    def _():
        wg = wg_ref[0, 0].astype(BF16)
        wu = wu_ref[0, 0].astype(BF16)
        for r0 in range(0, slots, rc):
            lo, hi = x_rows(r0)
            a = (jnp.dot(lo, wg[:half], preferred_element_type=F32)
                 + jnp.dot(hi, wg[half:], preferred_element_type=F32))
            u = (jnp.dot(lo, wu[:half], preferred_element_type=F32)
                 + jnp.dot(hi, wu[half:], preferred_element_type=F32))
            mid_scr[s, r0:r0 + rc, :] = (a * jax.nn.sigmoid(a) * u).astype(BF16)
            off, cnt = share(slots // n_f, r0 // rc)
            for k in range(cnt):
                scatter_row(dst_prv_ref, s * (slots // n_f) + off + k)

    @pl.when(s == n_f)
    def _():
        wait_scatter()

    for j in range(n_n):
        @pl.when(s == n_f + j)
        def _():
            wd = wd_ref[0, 0].astype(BF16)
            for r0 in range(0, slots, rc):
                y = jnp.dot(mid_scr[0, r0:r0 + rc, :], wd[0:tf], preferred_element_type=F32)
                for f in range(1, n_f):
                    y += jnp.dot(mid_scr[f, r0:r0 + rc, :], wd[f * tf:(f + 1) * tf], preferred_element_type=F32)
                y = y * gcol_scr[r0:r0 + rc, :]
                tnw = tnc * 128
                packed = _pack_bf16_pair(y[:, :tnw], y[:, tnw:])
                for c in range(tnc):
                    y_scr[pl.ds(r0 * nck + j * tnc + c, rc, stride=nck), :] = packed[:, c * 128:(c + 1) * 128]
                off, cnt = share(slots // n_n, r0 // rc)
                for k in range(cnt):
                    slot = j * (slots // n_n) + off + k
                    gather_row(idx_nxt_ref, slot, slot)

    @pl.when((e == n_e - 1) & (s == n_f + n_n - 1))
    def _():
        wait_gather()
        row_loop(0, slots, lambda r: scatter_row(dst_ref, r))
        wait_scatter()


def _experts(idx, dst, gate, sources, src_slots, w_gate, w_up, w_down, layer, n_pairs):
    n_e, slots = idx.shape
    d, fdim = w_gate.shape[2], w_gate.shape[3]
    tf, tn = 256, EXPERT_DOWN_TILE
    n_f, n_n = fdim // tf, d // tn
    rc = slots // 4
    nck = d // 256
    assert slots % 128 == 0 and rc % 16 == 0 and nck % n_n == 0 and slots % n_f == 0 and slots % n_n == 0
    assert all(s0 % 8 == 0 and s1 % 8 == 0 for s0, s1 in src_slots)
    idx3 = idx.reshape(n_e, 1, slots)
    dst3 = dst.reshape(n_e, 1, slots)
    smem = lambda f: pl.BlockSpec((1, 1, slots), f, memory_space=pltpu.SMEM)
    up_chunk = lambda i, s: (layer, i, 0, jnp.minimum(s, n_f - 1))
    return pl.pallas_call(
        functools.partial(_expert_body, n_src=len(sources), src_slots=src_slots, slots=slots, rc=rc,
                          n_f=n_f, n_n=n_n, tf=tf, n_e=n_e),
        grid=(n_e, n_f + n_n),
        in_specs=[
            smem(lambda i, s: (i, 0, 0)),
            smem(lambda i, s: (jnp.minimum(i + 1, n_e - 1), 0, 0)),
            smem(lambda i, s: (i, 0, 0)),
            smem(lambda i, s: (jnp.maximum(i - 1, 0), 0, 0)),
        ] + [pl.BlockSpec(memory_space=pl.ANY)] * len(sources) + [
            pl.BlockSpec((1, slots // 128, 128), lambda i, s: (i, 0, 0)),
            pl.BlockSpec((1, 1, d, tf), up_chunk),
            pl.BlockSpec((1, 1, d, tf), up_chunk),
            pl.BlockSpec((1, 1, fdim, tn), lambda i, s: (layer, i, 0, jnp.maximum(s - n_f, 0))),
        ],
        out_specs=pl.BlockSpec(memory_space=pl.ANY),
        out_shape=jax.ShapeDtypeStruct((n_pairs * nck, 128), I32),
        scratch_shapes=[
            pltpu.VMEM((slots * nck, 128), I32),
            pltpu.VMEM((slots * nck, 128), I32),
            pltpu.VMEM((n_f, slots, tf), BF16),
            pltpu.VMEM((slots, 1), F32),
            pltpu.SemaphoreType.DMA(()),
            pltpu.SemaphoreType.DMA(()),
        ],
        compiler_params=_params("arbitrary", "arbitrary"),
        name="moe_experts",
    )(idx3, idx3, dst3, dst3, *sources, gate.reshape(n_e, slots // 128, 128), w_gate, w_up, w_down)


def _combine_body(lo_ref, hi_ref, off_ref, cnt_ref, p_ref, h_ref, g2_ref, o_ref, acc_lo, acc_hi, *,
                  tt, nck, tnw, pair0, pairs_per_group):
    base = pair0 + pl.program_id(0) * pairs_per_group
    i = pl.program_id(1)
    lane = lax.broadcasted_iota(I32, (1, PAIR_BLOCK), 1)
    eye = lax.broadcasted_iota(I32, (128, 128), 0) == lax.broadcasted_iota(I32, (128, 128), 1)

    def column(row):
        cols = [jnp.sum(jnp.where(eye, row[:, c0:c0 + 128], 0), axis=1, keepdims=True) for c0 in range(0, tt, 128)]
        return cols[0] if len(cols) == 1 else jnp.concatenate(cols, axis=0)

    first = column(off_ref[...]) - base
    last = first + column(cnt_ref[...])
    shift = PAIR_BLOCK.bit_length() - 1
    k0 = lax.shift_right_logical(lo_ref[0, 0, i] - base, shift)
    k1 = lax.shift_right_logical(hi_ref[0, 0, i] - base + (PAIR_BLOCK - 1), shift)
    acc_lo[...] = jnp.zeros_like(acc_lo)
    acc_hi[...] = jnp.zeros_like(acc_hi)

    def step(k, carry):
        p0 = pl.multiple_of(k * PAIR_BLOCK, PAIR_BLOCK)
        parts = [_unpack_bf16_pair(p_ref[pl.ds(p0 * nck + c, PAIR_BLOCK, stride=nck), :]) for c in range(nck)]
        lo = jnp.concatenate([p[0] for p in parts], axis=1)
        hi = jnp.concatenate([p[1] for p in parts], axis=1)
        pr = p0 + lane
        seg = ((pr >= first) & (pr < last)).astype(F32).astype(BF16)
        acc_lo[...] += jnp.dot(seg, lo, preferred_element_type=F32)
        acc_hi[...] += jnp.dot(seg, hi, preferred_element_type=F32)
        return carry

    lax.fori_loop(k0, k1, step, 0)
    for j in range(nck * 128 // tnw):
        words = slice(j * tnw, (j + 1) * tnw)
        c_lo = slice(2 * j * tnw, (2 * j + 1) * tnw)
        c_hi = slice((2 * j + 1) * tnw, (2 * j + 2) * tnw)
        o_ref[:, c_lo] = h_ref[:, c_lo] + g2_ref[0, :, c_lo] * acc_lo[:, words]
        o_ref[:, c_hi] = h_ref[:, c_hi] + g2_ref[0, :, c_hi] * acc_hi[:, words]


def _combine(st, off, cnt, pairs, h, mods, pair0, pairs_per_group):
    rows, d = h.shape
    n_tok = st.seq
    tt = min(256, n_tok)
    n_tiles = n_tok // tt
    nck = d // 256
    assert pairs_per_group % PAIR_BLOCK == 0 and pair0 % pairs_per_group == 0
    off_t = off.reshape(st.batch, 1, n_tiles, tt)
    cnt_t = cnt.reshape(st.batch, 1, n_tiles, tt)
    tile_lo = off_t[..., 0]
    tile_hi = off_t[..., tt - 1] + cnt_t[..., tt - 1]
    row = st.ada_row(n_tok)
    smem = pl.BlockSpec((1, 1, n_tiles), lambda b, i: (b, 0, 0), memory_space=pltpu.SMEM)
    tile = lambda b, i: (b * n_tiles + i, 0)
    return pl.pallas_call(
        functools.partial(_combine_body, tt=tt, nck=nck, tnw=EXPERT_DOWN_TILE // 2, pair0=pair0,
                          pairs_per_group=pairs_per_group),
        grid=(st.batch, n_tiles),
        in_specs=[
            smem, smem,
            pl.BlockSpec((1, tt), lambda b, i: (0, b * n_tiles + i)),
            pl.BlockSpec((1, tt), lambda b, i: (0, b * n_tiles + i)),
            pl.BlockSpec((pairs_per_group * nck, 128), lambda b, i: (pair0 // pairs_per_group + b, 0)),
            pl.BlockSpec((tt, d), tile),
            pl.BlockSpec((1, 1, d), lambda b, i: (row(b) * N_ADA + 5, 0, 0)),
        ],
        out_specs=pl.BlockSpec((tt, d), tile),
        out_shape=jax.ShapeDtypeStruct((rows, d), F32),
        scratch_shapes=[pltpu.VMEM((tt, d // 2), F32)] * 2,
        compiler_params=_params("arbitrary", "arbitrary"),
        name="moe_combine",
    )(tile_lo, tile_hi, off, cnt, pairs, h, mods)


def _moe(streams, mixed, mods, w_gate, w_up, w_down, layer):
    n_e = mixed[0][2].shape[0]
    routed = []
    pair0 = 0
    slot0 = 0
    src_slots = []
    for st, (h, h2, aff) in zip(streams, mixed):
        cap = CAP_FACTOR * st.seq // n_e
        idx, dst, gate, off, cnt = _topk(st, aff, cap, pair0)
        routed.append((h2, idx, dst, gate, off, cnt, pair0, n_e * cap))
        src_slots.append((slot0, slot0 + st.batch * cap))
        pair0 += st.batch * n_e * cap
        slot0 += st.batch * cap
    flat = lambda t: t[:, :, :n_e].transpose(2, 0, 1).reshape(n_e, -1)
    idx = jnp.concatenate([flat(r[1]) for r in routed], axis=1)
    dst = jnp.concatenate([flat(r[2]) for r in routed], axis=1)
    gate = jnp.concatenate([flat(r[3]) for r in routed], axis=1)
    pairs = _experts(idx, dst, gate, [r[0] for r in routed], tuple(src_slots), w_gate, w_up, w_down, layer, pair0)
    return [_combine(st, r[4], r[5], pairs, m[0], mods, r[6], r[7]) for st, m, r in zip(streams, mixed, routed)]


def _forward(dims, x, c, ctx, c_ctx, ada_w, ada_b, norm_mix_g, norm_ffn_g, conv_w_in, conv_w, conv_w_out,
             attn_w_qkv, attn_q_gain, attn_k_gain, attn_sink, attn_w_o, router_w,
             expert_w_gate, expert_w_up, expert_w_down):
    d = dims.d
    lat = Stream(dims.batch, dims.seq)
    con = Stream(dims.batch, dims.ctx, shared_row=dims.batch)
    cvec = jnp.zeros((ADA_ROWS, d), F32).at[:dims.batch].set(c).at[dims.batch].set(c_ctx)
    mods = _adaln(cvec, ada_w, ada_b)
    mods = mods.reshape(mods.shape[0], ADA_ROWS * N_ADA, 1, d)
    hs = [x.reshape(lat.rows, d), ctx.reshape(con.rows, d)]

    w_in, w_out = conv_w_in[0].astype(BF16), conv_w_out[0].astype(BF16)
    mixed = []
    for st, h in zip((lat, con), hs):
        gb_u = _inproj(st, h, mods[0], norm_mix_g[0], w_in)
        mixed.append(_mixout(st, gb_u, conv_w[0], w_out, h, mods[0], norm_ffn_g[0], router_w[0]))
    h_lat, h_ctx = _moe((lat, con), mixed, mods[0], expert_w_gate, expert_w_up, expert_w_down, 0)

    w_qkv, w_o = attn_w_qkv[0].astype(BF16), attn_w_o[0].astype(BF16)
    qkv = _qkvproj(dims, lat, h_lat, mods[1], norm_mix_g[1], w_qkv, attn_q_gain[0], attn_k_gain[0], True)
    kv_ctx = _qkvproj(dims, con, h_ctx, mods[1], norm_mix_g[1], w_qkv, attn_q_gain[0], attn_k_gain[0], False)
    o = _attention(dims, qkv, kv_ctx, attn_sink[0])
    mixed = [_mixout(lat, (o,), None, w_o, h_lat, mods[1], norm_ffn_g[1], router_w[1])]
    (h_lat,) = _moe((lat,), mixed, mods[1], expert_w_gate, expert_w_up, expert_w_down, 1)
    return h_lat.reshape(dims.batch, dims.seq, d)


def kernel(x, c, ctx, c_ctx, ada_w, ada_b, norm_mix_g, norm_ffn_g, conv_w_in, conv_w, conv_w_out, attn_w_qkv, attn_q_gain, attn_k_gain, attn_sink, attn_w_o, router_w, expert_w_gate, expert_w_up, expert_w_down):
    batch, seq, d = x.shape
    dims = Dims(d=d, batch=batch, seq=seq, grid_w=GRID_W, ctx=ctx.shape[1], heads=d // HEAD_DIM,
                kv_heads=(attn_w_qkv.shape[2] // HEAD_DIM - d // HEAD_DIM) // 2,
                experts=router_w.shape[2], d_expert=expert_w_gate.shape[3])
    return _forward(dims, x, c, ctx, c_ctx, ada_w, ada_b, norm_mix_g, norm_ffn_g, conv_w_in, conv_w, conv_w_out,
                    attn_w_qkv, attn_q_gain, attn_k_gain, attn_sink, attn_w_o, router_w,
                    expert_w_gate, expert_w_up, expert_w_down)
```

```python
import dataclasses
import functools

import jax
import jax.numpy as jnp
from jax import lax
from jax.experimental import pallas as pl
from jax.experimental.pallas import tpu as pltpu

F32 = jnp.float32
BF16 = jnp.bfloat16
I32 = jnp.int32

NORM_EPS = 1e-6
MASK_VALUE = -1e30
ROPE_THETA = 10000.0
GRID_W = 64
CAP_FACTOR = 2
HEAD_DIM = 128
ATTN_BLOCK = 128
N_ADA = 6
ADA_ROWS = 16
HI_MASK = -65536
PAIR_BLOCK = 256
EXPERT_DOWN_TILE = 512
VMEM_LIMIT = 58 * 1024 * 1024


@dataclasses.dataclass(frozen=True)
class Dims:
    d: int
    batch: int
    seq: int
    grid_w: int
    ctx: int
    heads: int
    kv_heads: int
    experts: int
    d_expert: int

    @property
    def dq(self):
        return self.heads * HEAD_DIM

    @property
    def dkv(self):
        return self.kv_heads * HEAD_DIM


@dataclasses.dataclass(frozen=True)
class Stream:
    batch: int
    seq: int
    shared_row: int = -1

    @property
    def rows(self):
        return self.batch * self.seq

    def tm(self, cap=1024):
        unit = self.rows if self.shared_row >= 0 else self.seq
        t = cap
        while unit % t:
            t //= 2
        return t

    def ada_row(self, tm):
        if self.shared_row >= 0:
            return lambda m: self.shared_row
        per = self.seq // tm
        assert per >= 1
        return lambda m: m // per


def _params(*sem):
    return pltpu.CompilerParams(dimension_semantics=sem, vmem_limit_bytes=VMEM_LIMIT)


def _modulate(h, gain, shift, scale):
    ms = jnp.mean(h * h, axis=-1, keepdims=True)
    xn = h * lax.rsqrt(ms + NORM_EPS)
    return (xn * gain) * (1.0 + scale) + shift


def _mod_spec(st, tm, which, d, width=None):
    row = st.ada_row(tm)
    if width is None:
        return pl.BlockSpec((1, 1, d), lambda m, n: (row(m) * N_ADA + which, 0, 0))
    return pl.BlockSpec((1, 1, width), lambda m, n: (row(m) * N_ADA + which, 0, n))


def _pack_bf16_pair(lo, hi):
    lo_bits = pltpu.bitcast(lo.astype(BF16).astype(F32), I32)
    hi_bits = pltpu.bitcast(hi.astype(BF16).astype(F32), I32)
    return (hi_bits & HI_MASK) | lax.shift_right_logical(lo_bits, 16)


def _unpack_bf16_pair(w):
    lo = pltpu.bitcast(lax.shift_left(w, 16), F32).astype(BF16)
    hi = pltpu.bitcast(w & HI_MASK, F32).astype(BF16)
    return lo, hi


def _adaln_body(c_ref, w_ref, b_ref, o_ref):
    c = c_ref[...]
    s = c * jax.nn.sigmoid(c)
    o_ref[0] = jnp.dot(s.astype(BF16), w_ref[0].astype(BF16), preferred_element_type=F32) + b_ref[0]


def _adaln(cvec, ada_w, ada_b):
    depth, d, n = ada_w.shape
    tn = min(1024, n)
    return pl.pallas_call(
        _adaln_body,
        grid=(depth, n // tn),
        in_specs=[
            pl.BlockSpec((ADA_ROWS, d), lambda l, j: (0, 0)),
            pl.BlockSpec((1, d, tn), lambda l, j: (l, 0, j)),
            pl.BlockSpec((1, 1, tn), lambda l, j: (l, 0, j)),
        ],
        out_specs=pl.BlockSpec((1, ADA_ROWS, tn), lambda l, j: (l, 0, j)),
        out_shape=jax.ShapeDtypeStruct((depth, ADA_ROWS, n), F32),
        compiler_params=_params("arbitrary", "arbitrary"),
        name="adaln",
    )(cvec, ada_w, ada_b.reshape(depth, 1, n))


def _inproj_body(h_ref, g_ref, sh_ref, sc_ref, wb_ref, wc_ref, wx_ref, gb_ref, u_ref, a_scr):
    @pl.when(pl.program_id(1) == 0)
    def _():
        a_scr[...] = _modulate(h_ref[...], g_ref[...], sh_ref[0], sc_ref[0]).astype(BF16)

    a = a_scr[...]
    gb = jnp.dot(a, wb_ref[...], preferred_element_type=F32)
    gc = jnp.dot(a, wc_ref[...], preferred_element_type=F32)
    xv = jnp.dot(a, wx_ref[...], preferred_element_type=F32)
    gb_ref[...] = gb.astype(BF16)
    u_ref[...] = (gc * xv).astype(BF16)


def _inproj(st, h, mods, gain, w_in):
    rows, d = h.shape
    tm, tn = st.tm(), 512
    nt = d // tn
    return pl.pallas_call(
        _inproj_body,
        grid=(rows // tm, nt),
        in_specs=[
            pl.BlockSpec((tm, d), lambda m, n: (m, 0)),
            pl.BlockSpec((1, d), lambda m, n: (0, 0)),
            _mod_spec(st, tm, 0, d),
            _mod_spec(st, tm, 1, d),
            pl.BlockSpec((d, tn), lambda m, n: (0, n)),
            pl.BlockSpec((d, tn), lambda m, n: (0, nt + n)),
            pl.BlockSpec((d, tn), lambda m, n: (0, 2 * nt + n)),
        ],
        out_specs=[pl.BlockSpec((tm, tn), lambda m, n: (m, n))] * 2,
        out_shape=[jax.ShapeDtypeStruct((rows, d), BF16)] * 2,
        scratch_shapes=[pltpu.VMEM((tm, d), BF16)],
        compiler_params=_params("arbitrary", "arbitrary"),
        name="conv_inproj",
    )(h, gain.reshape(1, d), mods, mods, w_in, w_in, w_in)


def _mixout_body(*refs, conv, seq, tm, experts, cw):
    if conv:
        gb_ref, u_ref, up_ref, un_ref, cw_ref = refs[:5]
        refs = refs[5:]
        v_scr = refs[-1]
        halo = up_ref.shape[0]
        row = lax.broadcasted_iota(I32, (tm, 1), 0)
        pos = (pl.program_id(0) * tm + row) & (seq - 1)
        for c0 in range(0, u_ref.shape[1], cw):
            cols = slice(c0, c0 + cw)
            u = u_ref[:, cols].astype(F32)
            u_dn = pltpu.roll(u, 1, 0)
            u_dn = jnp.where(row == 0, up_ref[halo - 1:halo, cols].astype(F32), u_dn)
            u_dn = jnp.where(pos == 0, 0.0, u_dn)
            u_up = pltpu.roll(u, tm - 1, 0)
            u_up = jnp.where(row == tm - 1, un_ref[0:1, cols].astype(F32), u_up)
            u_up = jnp.where(pos == seq - 1, 0.0, u_up)
            y = cw_ref[0:1, cols] * u_dn + cw_ref[1:2, cols] * u + cw_ref[2:3, cols] * u_up
            v_scr[:, cols] = (gb_ref[:, cols].astype(F32) * y).astype(BF16)
        v_ref = v_scr
    else:
        v_ref = refs[0]
        refs = refs[1:]
    w_ref, h_ref, g1_ref, gf_ref, sh_ref, sc_ref, wr_ref, hn_ref, h2_ref, aff_ref = refs[:10]
    half = h_ref.shape[1] // 2
    nck = half // 128
    th = tm // 2
    for r0 in (0, th):
        rows = slice(r0, r0 + th)
        out = jnp.dot(v_ref[rows, :], w_ref[...], preferred_element_type=F32)
        hn = h_ref[rows, :] + g1_ref[0] * out
        hn_ref[rows, :] = hn
        a = _modulate(hn, gf_ref[...], sh_ref[0], sc_ref[0])
        packed = _pack_bf16_pair(a[:, :half], a[:, half:])
        for j in range(nck):
            h2_ref[pl.ds(r0 * nck + j, th, stride=nck), :] = packed[:, j * 128:(j + 1) * 128]
        logits = jnp.dot(a.astype(BF16), wr_ref[...], preferred_element_type=F32)
        lt = logits.T[0:experts, :]
        ex = jnp.exp(lt - jnp.max(lt, axis=0, keepdims=True))
        aff_ref[:, rows] = ex / jnp.sum(ex, axis=0, keepdims=True)


def _mixout(st, v_inputs, conv_w, w_out, h, mods, gain_ffn, w_router):
    rows, d = h.shape
    experts = w_router.shape[1]
    conv = conv_w is not None
    tm, halo = st.tm(512), 16
    nck = d // 256
    assert st.seq & (st.seq - 1) == 0 and tm % 256 == 0
    row = st.ada_row(tm)
    mod = lambda which: pl.BlockSpec((1, 1, d), lambda m: (row(m) * N_ADA + which, 0, 0))
    tile = pl.BlockSpec((tm, d), lambda m: (m, 0))
    const = lambda shape: pl.BlockSpec(shape, lambda m: (0,) * len(shape))
    wr = jnp.zeros((d, 128), BF16).at[:, :experts].set(w_router.astype(BF16))
    if conv:
        gb, u = v_inputs
        per, last = tm // halo, rows // halo - 1
        v_specs = [tile, tile,
                   pl.BlockSpec((halo, d), lambda m: (jnp.maximum(m * per - 1, 0), 0)),
                   pl.BlockSpec((halo, d), lambda m: (jnp.minimum((m + 1) * per, last), 0)),
                   const((3, d))]
        v_args = (gb, u, u, u, conv_w)
        scratch = [pltpu.VMEM((tm, d), BF16)]
    else:
        v_specs, v_args, scratch = [tile], tuple(v_inputs), []
    return pl.pallas_call(
        functools.partial(_mixout_body, conv=conv, seq=st.seq, tm=tm, experts=experts, cw=512),
        grid=(rows // tm,),
        in_specs=v_specs + [
            pl.BlockSpec((d, d), lambda m: (0, 0), pipeline_mode=pl.Buffered(1)),
            tile, mod(2), const((1, d)), mod(3), mod(4), const((d, 128)),
        ],
        out_specs=[
            tile,
            pl.BlockSpec((tm * nck, 128), lambda m: (m, 0)),
            pl.BlockSpec((experts, tm), lambda m: (0, m)),
        ],
        out_shape=[
            jax.ShapeDtypeStruct((rows, d), F32),
            jax.ShapeDtypeStruct((rows * nck, 128), I32),
            jax.ShapeDtypeStruct((experts, rows), F32),
        ],
        scratch_shapes=scratch,
        compiler_params=_params("arbitrary"),
        name="mix_out_router",
    )(*v_args, w_out, h, mods, gain_ffn.reshape(1, d), mods, mods, wr)


def _qkv_body(h_ref, g_ref, sh_ref, sc_ref, w_ref, cq_ref, sq_ref, ck_ref, sk_ref, o_ref, a_scr, *, n_q, tn, rc):
    n = pl.program_id(1)

    @pl.when(n == 0)
    def _():
        a_scr[...] = _modulate(h_ref[...], g_ref[...], sh_ref[0], sc_ref[0]).astype(BF16)

    acc = jnp.dot(a_scr[...], w_ref[...], preferred_element_type=F32)

    def norm_rope(cos_ref, sin_ref):
        lane = lax.broadcasted_iota(I32, (1, HEAD_DIM), 1)
        first = (lane % (HEAD_DIM // 2)) < (HEAD_DIM // 4)
        for r0 in range(0, acc.shape[0], rc):
            cos = cos_ref[r0:r0 + rc, :]
            sin = sin_ref[r0:r0 + rc, :]
            for hd in range(tn // HEAD_DIM):
                lanes = slice(hd * HEAD_DIM, (hd + 1) * HEAD_DIM)
                x = acc[r0:r0 + rc, lanes]
                r = lax.rsqrt(jnp.mean(x * x, axis=-1, keepdims=True) + NORM_EPS)
                rot = jnp.where(first, pltpu.roll(x, HEAD_DIM - HEAD_DIM // 4, 1), pltpu.roll(x, HEAD_DIM // 4, 1))
                o_ref[r0:r0 + rc, lanes] = ((x * cos + rot * sin) * r).astype(BF16)

    @pl.when(n < n_q)
    def _():
        norm_rope(cq_ref, sq_ref)

    @pl.when(n == n_q)
    def _():
        norm_rope(ck_ref, sk_ref)

    @pl.when(n > n_q)
    def _():
        o_ref[...] = acc.astype(BF16)


def _rope_tables(dims, tm, gain, scale):
    quarter = HEAD_DIM // 4
    inv_freq = ROPE_THETA ** (-jnp.arange(quarter, dtype=F32) * 2.0 / (HEAD_DIM // 2))
    t = jnp.arange(dims.seq)
    ang_row = (t // dims.grid_w).astype(F32)[:, None] * inv_freq
    ang_col = (t % dims.grid_w).astype(F32)[:, None] * inv_freq
    cos = jnp.concatenate([jnp.cos(ang_row)] * 2 + [jnp.cos(ang_col)] * 2, axis=-1)
    sin = jnp.concatenate([-jnp.sin(ang_row), jnp.sin(ang_row), -jnp.sin(ang_col), jnp.sin(ang_col)], axis=-1)
    cos = jnp.concatenate([cos, jnp.ones((tm, HEAD_DIM), F32)], axis=0)
    sin = jnp.concatenate([sin, jnp.zeros((tm, HEAD_DIM), F32)], axis=0)
    g = gain.reshape(2, 2, quarter)
    partner_gain = g[:, ::-1, :].reshape(HEAD_DIM)
    return cos * (gain * scale), sin * (partner_gain * scale)


def _qkvproj(dims, st, h, mods, gain, w_qkv, q_gain, k_gain, with_q):
    rows, d = h.shape
    tm, tn = st.tm(), dims.dkv
    n_q = dims.dq // tn if with_q else 0
    col0 = 0 if with_q else dims.dq // tn
    n_total = n_q + 2
    cos_q, sin_q = _rope_tables(dims, tm, q_gain, HEAD_DIM ** -0.5)
    cos_k, sin_k = _rope_tables(dims, tm, k_gain, 1.0)
    per = dims.seq // tm
    pos_tile = (lambda m, n: (m % per, 0)) if with_q else (lambda m, n: (per, 0))
    return pl.pallas_call(
        functools.partial(_qkv_body, n_q=n_q, tn=tn, rc=tm),
        grid=(rows // tm, n_total),
        in_specs=[
            pl.BlockSpec((tm, d), lambda m, n: (m, 0)),
            pl.BlockSpec((1, d), lambda m, n: (0, 0)),
            _mod_spec(st, tm, 0, d),
            _mod_spec(st, tm, 1, d),
            pl.BlockSpec((d, tn), lambda m, n: (0, col0 + n)),
            pl.BlockSpec((tm, HEAD_DIM), pos_tile),
            pl.BlockSpec((tm, HEAD_DIM), pos_tile),
            pl.BlockSpec((tm, HEAD_DIM), pos_tile),
            pl.BlockSpec((tm, HEAD_DIM), pos_tile),
        ],
        out_specs=pl.BlockSpec((tm, tn), lambda m, n: (m, n)),
        out_shape=jax.ShapeDtypeStruct((rows, n_total * tn), BF16),
        scratch_shapes=[pltpu.VMEM((tm, d), BF16)],
        compiler_params=_params("arbitrary", "arbitrary"),
        name="attn_qkv",
    )(h, gain.reshape(1, d), mods, mods, w_qkv, cos_q, sin_q, cos_k, sin_k)


def _attn_body(sink_ref, q_ref, k_ref, v_ref, kc_ref, vc_ref, o_ref, *, dims):
    group = dims.heads // dims.kv_heads
    blk = ATTN_BLOCK
    band = 3 * blk
    n_ctx = dims.ctx
    head0 = pl.program_id(1) * group
    kc = kc_ref[...]
    vc = vc_ref[...]

    n_blk = dims.seq // blk
    assert n_blk >= 3
    sink = jnp.concatenate([jnp.full((blk, 1), sink_ref[head0 + g], F32) for g in range(group)], axis=0)

    def mask_bias(first_key):
        q_rel = first_key + (lax.broadcasted_iota(I32, (group * blk, 1), 0) & (blk - 1))
        col = lax.broadcasted_iota(I32, (1, band + n_ctx), 1)
        valid = (col >= band) | (jnp.abs(q_rel - col) <= blk)
        return jnp.where(valid, 0.0, MASK_VALUE)

    def block(n, start, bias):
        q0 = n * blk if isinstance(n, int) else pl.multiple_of(n * blk, blk)
        kcat = jnp.concatenate([k_ref[pl.ds(start, band), :], kc], axis=0)
        vcat = jnp.concatenate([v_ref[pl.ds(start, band), :], vc], axis=0)
        q = jnp.concatenate([q_ref[pl.ds(q0, blk), g * HEAD_DIM:(g + 1) * HEAD_DIM] for g in range(group)], axis=0)
        s = lax.dot_general(q, kcat, (((1,), (1,)), ((), ())), preferred_element_type=F32) + bias
        m = jnp.maximum(jnp.max(s, axis=-1, keepdims=True), sink)
        p = jnp.exp(s - m)
        den = jnp.sum(p, axis=-1, keepdims=True) + jnp.exp(sink - m)
        o = jnp.dot(p.astype(BF16), vcat, preferred_element_type=F32) / den
        for g in range(group):
            o_ref[pl.ds(q0, blk), g * HEAD_DIM:(g + 1) * HEAD_DIM] = o[g * blk:(g + 1) * blk].astype(BF16)

    block(0, 0, mask_bias(0))
    mid_bias = mask_bias(blk)

    def middle(n, carry):
        block(n, pl.multiple_of((n - 1) * blk, blk), mid_bias)
        return carry

    lax.fori_loop(1, n_blk - 1, middle, 0, unroll=2)
    block(n_blk - 1, dims.seq - band, mask_bias(2 * blk))


def _attention(dims, qkv, kv_ctx, sink):
    group = dims.heads // dims.kv_heads
    gw = group * HEAD_DIM
    k_col = dims.dq // HEAD_DIM
    v_col = (dims.dq + dims.dkv) // HEAD_DIM
    return pl.pallas_call(
        functools.partial(_attn_body, dims=dims),
        grid=(dims.batch, dims.kv_heads),
        in_specs=[
            pl.BlockSpec(memory_space=pltpu.SMEM),
            pl.BlockSpec((dims.seq, gw), lambda b, h: (b, h)),
            pl.BlockSpec((dims.seq, HEAD_DIM), lambda b, h: (b, k_col + h)),
            pl.BlockSpec((dims.seq, HEAD_DIM), lambda b, h: (b, v_col + h)),
            pl.BlockSpec((dims.ctx, HEAD_DIM), lambda b, h: (b, h)),
            pl.BlockSpec((dims.ctx, HEAD_DIM), lambda b, h: (b, dims.kv_heads + h)),
        ],
        out_specs=pl.BlockSpec((dims.seq, gw), lambda b, h: (b, h)),
        out_shape=jax.ShapeDtypeStruct((dims.batch * dims.seq, dims.dq), BF16),
        compiler_params=_params("arbitrary", "arbitrary"),
        name="attn_core",
    )(sink, qkv, qkv, qkv, kv_ctx, kv_ctx)


def _lane_prefix(x, tri):
    n = x.shape[1]
    off = jnp.zeros((x.shape[0], 1), F32)
    chunks = []
    for ch in range(n // 128):
        xc = x[:, ch * 128:(ch + 1) * 128]
        incl = jnp.dot(xc.astype(BF16), tri, preferred_element_type=F32)
        chunks.append(incl - xc + off)
        off = off + incl[:, 127:128]
    return chunks[0] if len(chunks) == 1 else jnp.concatenate(chunks, axis=1)


def _topk_body(aff_ref, idx_ref, dst_ref, gate_ref, off_ref, cnt_ref, w_scr, a_scr, q_scr, clo_scr, chi_scr, *,
               n_tok, cap, pair0, experts):
    grp = pl.program_id(0)
    a = aff_ref[...]
    bits = pltpu.bitcast(a, I32)
    lane = lax.broadcasted_iota(I32, (experts, n_tok), 1)

    def count(mask):
        return jnp.sum(mask.astype(I32), axis=1, keepdims=True)

    def thr_step(i, thr):
        hi = jnp.left_shift(jnp.int32(1), 29 - 2 * i)
        lo = jnp.left_shift(jnp.int32(1), 28 - 2 * i)
        best = thr
        for cand in (thr | lo, thr | hi, thr | hi | lo):
            best = jnp.where(count(bits >= cand) >= cap, cand, best)
        return best

    first = jnp.int32(1 << 30)
    thr = jnp.where(count(bits >= first) >= cap, first, jnp.zeros((experts, 1), I32))
    thr = lax.fori_loop(0, 15, thr_step, thr)
    above = bits > thr
    tie = bits == thr
    need = cap - count(above)

    top_bit = n_tok.bit_length() - 2

    def tie_try(lim, cands):
        for cand in cands:
            lim = jnp.where(count(tie & (lane < cand)) < need, cand, lim)
        return lim

    def tie_step(i, lim):
        hi = jnp.left_shift(jnp.int32(1), top_bit - odd - 2 * i)
        lo = jnp.left_shift(jnp.int32(1), top_bit - odd - 2 * i - 1)
        return tie_try(lim, (lim | lo, lim | hi, lim | hi | lo))

    odd = (top_bit + 1) % 2
    lim = jnp.zeros((experts, 1), I32)
    if odd:
        lim = tie_try(lim, (lim | (1 << top_bit),))
    lim = lax.fori_loop(0, (top_bit + 1) // 2, tie_step, lim)
    sel = above | (tie & (lane <= lim))
    sel_f = sel.astype(F32)
    sel_i = sel.astype(I32)

    r = lax.broadcasted_iota(I32, (128, 128), 0)
    c = lax.broadcasted_iota(I32, (128, 128), 1)
    tri = (r <= c).astype(F32).astype(BF16)

    level = jnp.zeros((1, n_tok), I32)
    levels = []
    for e in range(experts):
        levels.append(level)
        level = level + sel_i[e:e + 1, :]
    cnt = level
    off = _lane_prefix(cnt.astype(F32), tri).astype(I32) + (pair0 + grp * (experts * cap))
    off_ref[...] = off
    cnt_ref[...] = cnt
    pair_row = jnp.concatenate(levels, axis=0) + off

    n_ch = n_tok // 128
    pad = w_scr.shape[0] // experts
    lane_e = lax.broadcasted_iota(I32, (experts, 128), 1)
    w_scr[...] = jnp.zeros_like(w_scr)
    a_scr[...] = jnp.zeros_like(a_scr)
    q_scr[...] = jnp.zeros_like(q_scr)
    first = jnp.zeros((experts, 1), F32)
    c_lo = jnp.full((experts, 128), cap, I32)
    c_hi = jnp.full((experts, 128), cap, I32)
    for ch in range(n_ch):
        lanes = slice(ch * 128, (ch + 1) * 128)
        incl = jnp.dot(sel_f[:, lanes].astype(BF16), tri, preferred_element_type=F32)
        w_scr[pl.ds(ch, experts, stride=pad), :] = incl
        a_scr[pl.ds(ch, experts, stride=pad), :] = a[:, lanes]
        q_scr[pl.ds(ch, experts, stride=pad), :] = pair_row[:, lanes]
        nxt = first + incl[:, 127:128]
        c_lo = jnp.where(lane_e == ch, first.astype(I32), c_lo)
        c_hi = jnp.where(lane_e == ch, nxt.astype(I32), c_hi)
        first = nxt
    clo_scr[...] = c_lo
    chi_scr[...] = c_hi

    idx_ref[0] = jnp.zeros((cap, 128), I32)
    dst_ref[0] = jnp.zeros((cap, 128), I32)
    gate_ref[0] = jnp.zeros((cap, 128), F32)
    slot = lax.broadcasted_iota(I32, (cap, 1), 0)
    lane = lax.broadcasted_iota(I32, (1, 128), 1)
    zpad = jnp.zeros((128 - pad, 128), F32)

    def rhs(m):
        return jnp.concatenate([m, zpad], axis=0).astype(BF16)

    def per_expert(e, carry):
        base = pl.multiple_of(e * pad, pad)
        wm = w_scr[pl.ds(base, pad), :]
        am = a_scr[pl.ds(base, pad), :]
        qm = q_scr[pl.ds(base, pad), :]
        lo = clo_scr[pl.ds(e, 1), :]
        hi = chi_scr[pl.ds(e, 1), :]
        in_chunk = (lo <= slot) & (slot < hi)
        onehot = in_chunk.astype(F32).astype(BF16)

        def rows_of(m):
            return jnp.dot(onehot, rhs(m), preferred_element_type=F32)

        a1 = am.astype(BF16).astype(F32)
        a2 = (am - a1).astype(BF16).astype(F32)
        a3 = am - a1 - a2
        a_rows = rows_of(a1) + rows_of(a2) + rows_of(a3)
        q_rows = (rows_of(lax.shift_right_logical(qm, 8).astype(F32)) * 256.0
                  + rows_of((qm & 255).astype(F32)))
        w_rows = rows_of(wm)
        chunk_first = jnp.sum(jnp.where(in_chunk, lo, 0), axis=1, keepdims=True)
        chunk = jnp.sum(jnp.where(in_chunk, lane, 0), axis=1, keepdims=True)
        rank = (slot - chunk_first).astype(F32)
        pos = jnp.sum((w_rows <= rank).astype(I32), axis=1, keepdims=True)
        here = lane == pos
        gate = jnp.sum(jnp.where(here, a_rows, 0.0), axis=1, keepdims=True)
        dst = jnp.sum(jnp.where(here, q_rows, 0.0), axis=1, keepdims=True).astype(I32)
        token = chunk * 128 + pos + grp * n_tok
        mine = lane == e
        idx_ref[0] = jnp.where(mine, token, idx_ref[0])
        dst_ref[0] = jnp.where(mine, dst, dst_ref[0])
        gate_ref[0] = jnp.where(mine, gate, gate_ref[0])
        return carry

    lax.fori_loop(0, experts, per_expert, 0)


def _topk(st, aff, cap, pair0):
    e = aff.shape[0]
    n_tok, n_groups = st.seq, st.batch
    pad = max(8, n_tok // 128)
    assert e <= 128 and pad <= 128 and n_groups * e * cap < 2 ** 16 * 256
    tile_spec = pl.BlockSpec((1, cap, 128), lambda g: (g, 0, 0))
    return pl.pallas_call(
        functools.partial(_topk_body, n_tok=n_tok, cap=cap, pair0=pair0, experts=e),
        grid=(n_groups,),
        in_specs=[pl.BlockSpec((e, n_tok), lambda g: (0, g))],
        out_specs=[tile_spec, tile_spec, tile_spec,
                   pl.BlockSpec((1, n_tok), lambda g: (0, g)),
                   pl.BlockSpec((1, n_tok), lambda g: (0, g))],
        out_shape=[jax.ShapeDtypeStruct((n_groups, cap, 128), I32),
                   jax.ShapeDtypeStruct((n_groups, cap, 128), I32),
                   jax.ShapeDtypeStruct((n_groups, cap, 128), F32),
                   jax.ShapeDtypeStruct((1, n_groups * n_tok), I32),
                   jax.ShapeDtypeStruct((1, n_groups * n_tok), I32)],
        scratch_shapes=[pltpu.VMEM((e * pad, 128), F32), pltpu.VMEM((e * pad, 128), F32),
                        pltpu.VMEM((e * pad, 128), I32), pltpu.VMEM((e, 128), I32), pltpu.VMEM((e, 128), I32)],
        compiler_params=_params("arbitrary"),
        name="moe_topk",
    )(aff)


def _expert_body(*refs, n_src, src_slots, slots, rc, n_f, n_n, tf, n_e):
    idx_ref, idx_nxt_ref, dst_ref, dst_prv_ref = refs[0:4]
    srcs = refs[4:4 + n_src]
    gate_ref, wg_ref, wu_ref, wd_ref, pairs_hbm, x_scr, y_scr, mid_scr, gcol_scr, gsem, ssem = refs[4 + n_src:]
    e = pl.program_id(0)
    s = pl.program_id(1)
    nck = x_scr.shape[0] // slots
    half = nck * 128
    tnc = nck // n_n

    def tok(ref, r):
        return ref.at[pl.ds(pl.multiple_of(r * nck, nck), nck), :]

    def row_loop(lo, n, start_row):
        def eight(i, carry):
            for k in range(8):
                start_row(lo + i * 8 + k)
            return carry
        lax.fori_loop(0, n // 8, eight, 0)

    def gather_row(ref, r, slot):
        src = next(src for (s0, s1), src in zip(src_slots, srcs) if s0 <= slot < s1)
        pltpu.make_async_copy(tok(src, ref[0, 0, r]), tok(x_scr, r), gsem).start()

    def scatter_row(ref, r):
        pltpu.make_async_copy(tok(y_scr, r), tok(pairs_hbm, ref[0, 0, r]), ssem).start()

    n_chunks = slots // rc
    dma_chunks = max(n_chunks - 1, 1)

    def share(total, ci):
        base, extra = divmod(total, dma_chunks)
        if ci >= dma_chunks:
            return total, 0
        return ci * base + min(ci, extra), base + (1 if ci < extra else 0)

    def wait_gather():
        pltpu.make_async_copy(srcs[0].at[pl.ds(0, slots * nck), :], x_scr, gsem).wait()

    def wait_scatter():
        pltpu.make_async_copy(y_scr, pairs_hbm.at[pl.ds(0, slots * nck), :], ssem).wait()

    def x_rows(r0):
        parts = [_unpack_bf16_pair(x_scr[pl.ds(r0 * nck + c, rc, stride=nck), :]) for c in range(nck)]
        return (jnp.concatenate([p[0] for p in parts], axis=1), jnp.concatenate([p[1] for p in parts], axis=1))

    @pl.when((e == 0) & (s == 0))
    def _():
        y_scr[...] = jnp.zeros_like(y_scr)
        for (s0, s1), src in zip(src_slots, srcs):
            row_loop(s0, s1 - s0, lambda r, src=src: pltpu.make_async_copy(
                tok(src, idx_ref[0, 0, r]), tok(x_scr, r), gsem).start())

    @pl.when(s == 0)
    def _():
        wait_gather()
        eye = lax.broadcasted_iota(I32, (128, 128), 0) == lax.broadcasted_iota(I32, (128, 128), 1)
        for c in range(slots // 128):
            row = gate_ref[0, c:c + 1, :]
            gcol_scr[c * 128:(c + 1) * 128, :] = jnp.sum(jnp.where(eye, row, 0.0), axis=1, keepdims=True)

    @pl.when(s < n_f)
    def _():
        wg = wg_ref[0, 0].astype(BF16)
        wu = wu_ref[0, 0].astype(BF16)
        for r0 in range(0, slots, rc):
            lo, hi = x_rows(r0)
            a = (jnp.dot(lo, wg[:half], preferred_element_type=F32)
                 + jnp.dot(hi, wg[half:], preferred_element_type=F32))
            u = (jnp.dot(lo, wu[:half], preferred_element_type=F32)
                 + jnp.dot(hi, wu[half:], preferred_element_type=F32))
            mid_scr[s, r0:r0 + rc, :] = (a * jax.nn.sigmoid(a) * u).astype(BF16)
            off, cnt = share(slots // n_f, r0 // rc)
            for k in range(cnt):
                scatter_row(dst_prv_ref, s * (slots // n_f) + off + k)

    @pl.when(s == n_f)
    def _():
        wait_scatter()

    for j in range(n_n):
        @pl.when(s == n_f + j)
        def _():
            wd = wd_ref[0, 0].astype(BF16)
            for r0 in range(0, slots, rc):
                y = jnp.dot(mid_scr[0, r0:r0 + rc, :], wd[0:tf], preferred_element_type=F32)
                for f in range(1, n_f):
                    y += jnp.dot(mid_scr[f, r0:r0 + rc, :], wd[f * tf:(f + 1) * tf], preferred_element_type=F32)
                y = y * gcol_scr[r0:r0 + rc, :]
                tnw = tnc * 128
                packed = _pack_bf16_pair(y[:, :tnw], y[:, tnw:])
                for c in range(tnc):
                    y_scr[pl.ds(r0 * nck + j * tnc + c, rc, stride=nck), :] = packed[:, c * 128:(c + 1) * 128]
                g_steps = max(n_n - 1, 1)
                step_rows = -(-slots // g_steps)
                lo = min(j * step_rows, slots)
                off, cnt = share(min(step_rows, slots - lo), r0 // rc)
                for k in range(cnt):
                    slot = lo + off + k
                    gather_row(idx_nxt_ref, slot, slot)

    @pl.when((e == n_e - 1) & (s == n_f + n_n - 1))
    def _():
        wait_gather()
        row_loop(0, slots, lambda r: scatter_row(dst_ref, r))
        wait_scatter()


def _experts(idx, dst, gate, sources, src_slots, w_gate, w_up, w_down, layer, n_pairs):
    n_e, slots = idx.shape
    d, fdim = w_gate.shape[2], w_gate.shape[3]
    tf, tn = 256, EXPERT_DOWN_TILE
    n_f, n_n = fdim // tf, d // tn
    rc = slots // 4
    nck = d // 256
    assert slots % 128 == 0 and rc % 16 == 0 and nck % n_n == 0 and slots % n_f == 0 and slots % n_n == 0
    assert all(s0 % 8 == 0 and s1 % 8 == 0 for s0, s1 in src_slots)
    idx3 = idx.reshape(n_e, 1, slots)
    dst3 = dst.reshape(n_e, 1, slots)
    smem = lambda f: pl.BlockSpec((1, 1, slots), f, memory_space=pltpu.SMEM)
    up_chunk = lambda i, s: (layer, i, 0, jnp.minimum(s, n_f - 1))
    return pl.pallas_call(
        functools.partial(_expert_body, n_src=len(sources), src_slots=src_slots, slots=slots, rc=rc,
                          n_f=n_f, n_n=n_n, tf=tf, n_e=n_e),
        grid=(n_e, n_f + n_n),
        in_specs=[
            smem(lambda i, s: (i, 0, 0)),
            smem(lambda i, s: (jnp.minimum(i + 1, n_e - 1), 0, 0)),
            smem(lambda i, s: (i, 0, 0)),
            smem(lambda i, s: (jnp.maximum(i - 1, 0), 0, 0)),
        ] + [pl.BlockSpec(memory_space=pl.ANY)] * len(sources) + [
            pl.BlockSpec((1, slots // 128, 128), lambda i, s: (i, 0, 0)),
            pl.BlockSpec((1, 1, d, tf), up_chunk),
            pl.BlockSpec((1, 1, d, tf), up_chunk),
            pl.BlockSpec((1, 1, fdim, tn), lambda i, s: (layer, i, 0, jnp.maximum(s - n_f, 0))),
        ],
        out_specs=pl.BlockSpec(memory_space=pl.ANY),
        out_shape=jax.ShapeDtypeStruct((n_pairs * nck, 128), I32),
        scratch_shapes=[
            pltpu.VMEM((slots * nck, 128), I32),
            pltpu.VMEM((slots * nck, 128), I32),
            pltpu.VMEM((n_f, slots, tf), BF16),
            pltpu.VMEM((slots, 1), F32),
            pltpu.SemaphoreType.DMA(()),
            pltpu.SemaphoreType.DMA(()),
        ],
        compiler_params=_params("arbitrary", "arbitrary"),
        name="moe_experts",
    )(idx3, idx3, dst3, dst3, *sources, gate.reshape(n_e, slots // 128, 128), w_gate, w_up, w_down)


def _combine_body(lo_ref, hi_ref, off_ref, cnt_ref, p_ref, h_ref, g2_ref, o_ref, acc_lo, acc_hi, *,
                  tt, nck, tnw, pair0, pairs_per_group):
    base = pair0 + pl.program_id(0) * pairs_per_group
    i = pl.program_id(1)
    lane = lax.broadcasted_iota(I32, (1, PAIR_BLOCK), 1)
    eye = lax.broadcasted_iota(I32, (128, 128), 0) == lax.broadcasted_iota(I32, (128, 128), 1)

    def column(row):
        cols = [jnp.sum(jnp.where(eye, row[:, c0:c0 + 128], 0), axis=1, keepdims=True) for c0 in range(0, tt, 128)]
        return cols[0] if len(cols) == 1 else jnp.concatenate(cols, axis=0)

    first = column(off_ref[...]) - base
    last = first + column(cnt_ref[...])
    shift = PAIR_BLOCK.bit_length() - 1
    k0 = lax.shift_right_logical(lo_ref[0, 0, i] - base, shift)
    k1 = lax.shift_right_logical(hi_ref[0, 0, i] - base + (PAIR_BLOCK - 1), shift)
    acc_lo[...] = jnp.zeros_like(acc_lo)
    acc_hi[...] = jnp.zeros_like(acc_hi)

    def step(k, carry):
        p0 = pl.multiple_of(k * PAIR_BLOCK, PAIR_BLOCK)
        parts = [_unpack_bf16_pair(p_ref[pl.ds(p0 * nck + c, PAIR_BLOCK, stride=nck), :]) for c in range(nck)]
        lo = jnp.concatenate([p[0] for p in parts], axis=1)
        hi = jnp.concatenate([p[1] for p in parts], axis=1)
        pr = p0 + lane
        seg = ((pr >= first) & (pr < last)).astype(F32).astype(BF16)
        acc_lo[...] += jnp.dot(seg, lo, preferred_element_type=F32)
        acc_hi[...] += jnp.dot(seg, hi, preferred_element_type=F32)
        return carry

    lax.fori_loop(k0, k1, step, 0)
    for j in range(nck * 128 // tnw):
        words = slice(j * tnw, (j + 1) * tnw)
        c_lo = slice(2 * j * tnw, (2 * j + 1) * tnw)
        c_hi = slice((2 * j + 1) * tnw, (2 * j + 2) * tnw)
        o_ref[:, c_lo] = h_ref[:, c_lo] + g2_ref[0, :, c_lo] * acc_lo[:, words]
        o_ref[:, c_hi] = h_ref[:, c_hi] + g2_ref[0, :, c_hi] * acc_hi[:, words]


def _combine(st, off, cnt, pairs, h, mods, pair0, pairs_per_group):
    rows, d = h.shape
    n_tok = st.seq
    tt = min(256, n_tok)
    n_tiles = n_tok // tt
    nck = d // 256
    assert pairs_per_group % PAIR_BLOCK == 0 and pair0 % pairs_per_group == 0
    off_t = off.reshape(st.batch, 1, n_tiles, tt)
    cnt_t = cnt.reshape(st.batch, 1, n_tiles, tt)
    tile_lo = off_t[..., 0]
    tile_hi = off_t[..., tt - 1] + cnt_t[..., tt - 1]
    row = st.ada_row(n_tok)
    smem = pl.BlockSpec((1, 1, n_tiles), lambda b, i: (b, 0, 0), memory_space=pltpu.SMEM)
    tile = lambda b, i: (b * n_tiles + i, 0)
    return pl.pallas_call(
        functools.partial(_combine_body, tt=tt, nck=nck, tnw=EXPERT_DOWN_TILE // 2, pair0=pair0,
                          pairs_per_group=pairs_per_group),
        grid=(st.batch, n_tiles),
        in_specs=[
            smem, smem,
            pl.BlockSpec((1, tt), lambda b, i: (0, b * n_tiles + i)),
            pl.BlockSpec((1, tt), lambda b, i: (0, b * n_tiles + i)),
            pl.BlockSpec((pairs_per_group * nck, 128), lambda b, i: (pair0 // pairs_per_group + b, 0)),
            pl.BlockSpec((tt, d), tile),
            pl.BlockSpec((1, 1, d), lambda b, i: (row(b) * N_ADA + 5, 0, 0)),
        ],
        out_specs=pl.BlockSpec((tt, d), tile),
        out_shape=jax.ShapeDtypeStruct((rows, d), F32),
        scratch_shapes=[pltpu.VMEM((tt, d // 2), F32)] * 2,
        compiler_params=_params("arbitrary", "arbitrary"),
        name="moe_combine",
    )(tile_lo, tile_hi, off, cnt, pairs, h, mods)


def _moe(streams, mixed, mods, w_gate, w_up, w_down, layer):
    n_e = mixed[0][2].shape[0]
    routed = []
    pair0 = 0
    slot0 = 0
    src_slots = []
    for st, (h, h2, aff) in zip(streams, mixed):
        cap = CAP_FACTOR * st.seq // n_e
        idx, dst, gate, off, cnt = _topk(st, aff, cap, pair0)
        routed.append((h2, idx, dst, gate, off, cnt, pair0, n_e * cap))
        src_slots.append((slot0, slot0 + st.batch * cap))
        pair0 += st.batch * n_e * cap
        slot0 += st.batch * cap
    flat = lambda t: t[:, :, :n_e].transpose(2, 0, 1).reshape(n_e, -1)
    idx = jnp.concatenate([flat(r[1]) for r in routed], axis=1)
    dst = jnp.concatenate([flat(r[2]) for r in routed], axis=1)
    gate = jnp.concatenate([flat(r[3]) for r in routed], axis=1)
    pairs = _experts(idx, dst, gate, [r[0] for r in routed], tuple(src_slots), w_gate, w_up, w_down, layer, pair0)
    return [_combine(st, r[4], r[5], pairs, m[0], mods, r[6], r[7]) for st, m, r in zip(streams, mixed, routed)]


def _forward(dims, x, c, ctx, c_ctx, ada_w, ada_b, norm_mix_g, norm_ffn_g, conv_w_in, conv_w, conv_w_out,
             attn_w_qkv, attn_q_gain, attn_k_gain, attn_sink, attn_w_o, router_w,
             expert_w_gate, expert_w_up, expert_w_down):
    d = dims.d
    lat = Stream(dims.batch, dims.seq)
    con = Stream(dims.batch, dims.ctx, shared_row=dims.batch)
    cvec = jnp.zeros((ADA_ROWS, d), F32).at[:dims.batch].set(c).at[dims.batch].set(c_ctx)
    mods = _adaln(cvec, ada_w, ada_b)
    mods = mods.reshape(mods.shape[0], ADA_ROWS * N_ADA, 1, d)
    hs = [x.reshape(lat.rows, d), ctx.reshape(con.rows, d)]

    w_in, w_out = conv_w_in[0].astype(BF16), conv_w_out[0].astype(BF16)
    mixed = []
    for st, h in zip((lat, con), hs):
        gb_u = _inproj(st, h, mods[0], norm_mix_g[0], w_in)
        mixed.append(_mixout(st, gb_u, conv_w[0], w_out, h, mods[0], norm_ffn_g[0], router_w[0]))
    h_lat, h_ctx = _moe((lat, con), mixed, mods[0], expert_w_gate, expert_w_up, expert_w_down, 0)

    w_qkv, w_o = attn_w_qkv[0].astype(BF16), attn_w_o[0].astype(BF16)
    qkv = _qkvproj(dims, lat, h_lat, mods[1], norm_mix_g[1], w_qkv, attn_q_gain[0], attn_k_gain[0], True)
    kv_ctx = _qkvproj(dims, con, h_ctx, mods[1], norm_mix_g[1], w_qkv, attn_q_gain[0], attn_k_gain[0], False)
    o = _attention(dims, qkv, kv_ctx, attn_sink[0])
    mixed = [_mixout(lat, (o,), None, w_o, h_lat, mods[1], norm_ffn_g[1], router_w[1])]
    (h_lat,) = _moe((lat,), mixed, mods[1], expert_w_gate, expert_w_up, expert_w_down, 1)
    return h_lat.reshape(dims.batch, dims.seq, d)


def kernel(x, c, ctx, c_ctx, ada_w, ada_b, norm_mix_g, norm_ffn_g, conv_w_in, conv_w, conv_w_out, attn_w_qkv, attn_q_gain, attn_k_gain, attn_sink, attn_w_o, router_w, expert_w_gate, expert_w_up, expert_w_down):
    batch, seq, d = x.shape
    dims = Dims(d=d, batch=batch, seq=seq, grid_w=GRID_W, ctx=ctx.shape[1], heads=d // HEAD_DIM,
                kv_heads=(attn_w_qkv.shape[2] // HEAD_DIM - d // HEAD_DIM) // 2,
                experts=router_w.shape[2], d_expert=expert_w_gate.shape[3])
    return _forward(dims, x, c, ctx, c_ctx, ada_w, ada_b, norm_mix_g, norm_ffn_g, conv_w_in, conv_w, conv_w_out,
                    attn_w_qkv, attn_q_gain, attn_k_gain, attn_sink, attn_w_o, router_w,
                    expert_w_gate, expert_w_up, expert_w_down)
```

```python
import dataclasses
import functools

import jax
import jax.numpy as jnp
from jax import lax
from jax.experimental import pallas as pl
from jax.experimental.pallas import tpu as pltpu

F32 = jnp.float32
BF16 = jnp.bfloat16
I32 = jnp.int32

NORM_EPS = 1e-6
MASK_VALUE = -1e30
ROPE_THETA = 10000.0
GRID_W = 64
CAP_FACTOR = 2
HEAD_DIM = 128
ATTN_BLOCK = 128
N_ADA = 6
ADA_ROWS = 16
HI_MASK = -65536
PAIR_BLOCK = 256
EXPERT_DOWN_TILE = 1024
VMEM_LIMIT = 58 * 1024 * 1024


@dataclasses.dataclass(frozen=True)
class Dims:
    d: int
    batch: int
    seq: int
    grid_w: int
    ctx: int
    heads: int
    kv_heads: int
    experts: int
    d_expert: int

    @property
    def dq(self):
        return self.heads * HEAD_DIM

    @property
    def dkv(self):
        return self.kv_heads * HEAD_DIM


@dataclasses.dataclass(frozen=True)
class Stream:
    batch: int
    seq: int
    shared_row: int = -1

    @property
    def rows(self):
        return self.batch * self.seq

    def tm(self, cap=1024):
        unit = self.rows if self.shared_row >= 0 else self.seq
        t = cap
        while unit % t:
            t //= 2
        return t

    def ada_row(self, tm):
        if self.shared_row >= 0:
            return lambda m: self.shared_row
        per = self.seq // tm
        assert per >= 1
        return lambda m: m // per


def _params(*sem):
    return pltpu.CompilerParams(dimension_semantics=sem, vmem_limit_bytes=VMEM_LIMIT)


def _modulate(h, gain, shift, scale):
    ms = jnp.mean(h * h, axis=-1, keepdims=True)
    xn = h * lax.rsqrt(ms + NORM_EPS)
    return (xn * gain) * (1.0 + scale) + shift


def _mod_spec(st, tm, which, d, width=None):
    row = st.ada_row(tm)
    if width is None:
        return pl.BlockSpec((1, 1, d), lambda m, n: (row(m) * N_ADA + which, 0, 0))
    return pl.BlockSpec((1, 1, width), lambda m, n: (row(m) * N_ADA + which, 0, n))


def _pack_bf16_pair(lo, hi):
    lo_bits = pltpu.bitcast(lo.astype(BF16).astype(F32), I32)
    hi_bits = pltpu.bitcast(hi.astype(BF16).astype(F32), I32)
    return (hi_bits & HI_MASK) | lax.shift_right_logical(lo_bits, 16)


def _unpack_bf16_pair(w):
    lo = pltpu.bitcast(lax.shift_left(w, 16), F32).astype(BF16)
    hi = pltpu.bitcast(w & HI_MASK, F32).astype(BF16)
    return lo, hi


def _adaln_body(c_ref, w_ref, b_ref, o_ref):
    c = c_ref[...]
    s = c * jax.nn.sigmoid(c)
    o_ref[0] = jnp.dot(s.astype(BF16), w_ref[0].astype(BF16), preferred_element_type=F32) + b_ref[0]


def _adaln(cvec, ada_w, ada_b):
    depth, d, n = ada_w.shape
    tn = min(1024, n)
    return pl.pallas_call(
        _adaln_body,
        grid=(depth, n // tn),
        in_specs=[
            pl.BlockSpec((ADA_ROWS, d), lambda l, j: (0, 0)),
            pl.BlockSpec((1, d, tn), lambda l, j: (l, 0, j)),
            pl.BlockSpec((1, 1, tn), lambda l, j: (l, 0, j)),
        ],
        out_specs=pl.BlockSpec((1, ADA_ROWS, tn), lambda l, j: (l, 0, j)),
        out_shape=jax.ShapeDtypeStruct((depth, ADA_ROWS, n), F32),
        compiler_params=_params("arbitrary", "arbitrary"),
        name="adaln",
    )(cvec, ada_w, ada_b.reshape(depth, 1, n))


def _inproj_body(h_ref, g_ref, sh_ref, sc_ref, wb_ref, wc_ref, wx_ref, gb_ref, u_ref, a_scr):
    @pl.when(pl.program_id(1) == 0)
    def _():
        a_scr[...] = _modulate(h_ref[...], g_ref[...], sh_ref[0], sc_ref[0]).astype(BF16)

    a = a_scr[...]
    gb = jnp.dot(a, wb_ref[...], preferred_element_type=F32)
    gc = jnp.dot(a, wc_ref[...], preferred_element_type=F32)
    xv = jnp.dot(a, wx_ref[...], preferred_element_type=F32)
    gb_ref[...] = gb.astype(BF16)
    u_ref[...] = (gc * xv).astype(BF16)


def _inproj(st, h, mods, gain, w_in):
    rows, d = h.shape
    tm, tn = st.tm(), 512
    nt = d // tn
    return pl.pallas_call(
        _inproj_body,
        grid=(rows // tm, nt),
        in_specs=[
            pl.BlockSpec((tm, d), lambda m, n: (m, 0)),
            pl.BlockSpec((1, d), lambda m, n: (0, 0)),
            _mod_spec(st, tm, 0, d),
            _mod_spec(st, tm, 1, d),
            pl.BlockSpec((d, tn), lambda m, n: (0, n)),
            pl.BlockSpec((d, tn), lambda m, n: (0, nt + n)),
            pl.BlockSpec((d, tn), lambda m, n: (0, 2 * nt + n)),
        ],
        out_specs=[pl.BlockSpec((tm, tn), lambda m, n: (m, n))] * 2,
        out_shape=[jax.ShapeDtypeStruct((rows, d), BF16)] * 2,
        scratch_shapes=[pltpu.VMEM((tm, d), BF16)],
        compiler_params=_params("arbitrary", "arbitrary"),
        name="conv_inproj",
    )(h, gain.reshape(1, d), mods, mods, w_in, w_in, w_in)


def _mixout_body(*refs, conv, seq, tm, experts, cw):
    if conv:
        gb_ref, u_ref, up_ref, un_ref, cw_ref = refs[:5]
        refs = refs[5:]
        v_scr = refs[-1]
        halo = up_ref.shape[0]
        row = lax.broadcasted_iota(I32, (tm, 1), 0)
        pos = (pl.program_id(0) * tm + row) & (seq - 1)
        for c0 in range(0, u_ref.shape[1], cw):
            cols = slice(c0, c0 + cw)
            u = u_ref[:, cols].astype(F32)
            u_dn = pltpu.roll(u, 1, 0)
            u_dn = jnp.where(row == 0, up_ref[halo - 1:halo, cols].astype(F32), u_dn)
            u_dn = jnp.where(pos == 0, 0.0, u_dn)
            u_up = pltpu.roll(u, tm - 1, 0)
            u_up = jnp.where(row == tm - 1, un_ref[0:1, cols].astype(F32), u_up)
            u_up = jnp.where(pos == seq - 1, 0.0, u_up)
            y = cw_ref[0:1, cols] * u_dn + cw_ref[1:2, cols] * u + cw_ref[2:3, cols] * u_up
            v_scr[:, cols] = (gb_ref[:, cols].astype(F32) * y).astype(BF16)
        v_ref = v_scr
    else:
        v_ref = refs[0]
        refs = refs[1:]
    w_ref, h_ref, g1_ref, gf_ref, sh_ref, sc_ref, wr_ref, hn_ref, h2_ref, aff_ref = refs[:10]
    half = h_ref.shape[1] // 2
    nck = half // 128
    th = tm // 2
    for r0 in (0, th):
        rows = slice(r0, r0 + th)
        out = jnp.dot(v_ref[rows, :], w_ref[...], preferred_element_type=F32)
        hn = h_ref[rows, :] + g1_ref[0] * out
        hn_ref[rows, :] = hn
        a = _modulate(hn, gf_ref[...], sh_ref[0], sc_ref[0])
        packed = _pack_bf16_pair(a[:, :half], a[:, half:])
        for j in range(nck):
            h2_ref[pl.ds(r0 * nck + j, th, stride=nck), :] = packed[:, j * 128:(j + 1) * 128]
        logits = jnp.dot(a.astype(BF16), wr_ref[...], preferred_element_type=F32)
        lt = logits.T[0:experts, :]
        ex = jnp.exp(lt - jnp.max(lt, axis=0, keepdims=True))
        aff_ref[:, rows] = ex / jnp.sum(ex, axis=0, keepdims=True)


def _mixout(st, v_inputs, conv_w, w_out, h, mods, gain_ffn, w_router):
    rows, d = h.shape
    experts = w_router.shape[1]
    conv = conv_w is not None
    tm, halo = st.tm(512), 16
    nck = d // 256
    assert st.seq & (st.seq - 1) == 0 and tm % 256 == 0
    row = st.ada_row(tm)
    mod = lambda which: pl.BlockSpec((1, 1, d), lambda m: (row(m) * N_ADA + which, 0, 0))
    tile = pl.BlockSpec((tm, d), lambda m: (m, 0))
    const = lambda shape: pl.BlockSpec(shape, lambda m: (0,) * len(shape))
    wr = jnp.zeros((d, 128), BF16).at[:, :experts].set(w_router.astype(BF16))
    if conv:
        gb, u = v_inputs
        per, last = tm // halo, rows // halo - 1
        v_specs = [tile, tile,
                   pl.BlockSpec((halo, d), lambda m: (jnp.maximum(m * per - 1, 0), 0)),
                   pl.BlockSpec((halo, d), lambda m: (jnp.minimum((m + 1) * per, last), 0)),
                   const((3, d))]
        v_args = (gb, u, u, u, conv_w)
        scratch = [pltpu.VMEM((tm, d), BF16)]
    else:
        v_specs, v_args, scratch = [tile], tuple(v_inputs), []
    return pl.pallas_call(
        functools.partial(_mixout_body, conv=conv, seq=st.seq, tm=tm, experts=experts, cw=512),
        grid=(rows // tm,),
        in_specs=v_specs + [
            pl.BlockSpec((d, d), lambda m: (0, 0), pipeline_mode=pl.Buffered(1)),
            tile, mod(2), const((1, d)), mod(3), mod(4), const((d, 128)),
        ],
        out_specs=[
            tile,
            pl.BlockSpec((tm * nck, 128), lambda m: (m, 0)),
            pl.BlockSpec((experts, tm), lambda m: (0, m)),
        ],
        out_shape=[
            jax.ShapeDtypeStruct((rows, d), F32),
            jax.ShapeDtypeStruct((rows * nck, 128), I32),
            jax.ShapeDtypeStruct((experts, rows), F32),
        ],
        scratch_shapes=scratch,
        compiler_params=_params("arbitrary"),
        name="mix_out_router",
    )(*v_args, w_out, h, mods, gain_ffn.reshape(1, d), mods, mods, wr)


def _qkv_body(h_ref, g_ref, sh_ref, sc_ref, w_ref, cq_ref, sq_ref, ck_ref, sk_ref, o_ref, a_scr, *, n_q, tn, rc):
    n = pl.program_id(1)

    @pl.when(n == 0)
    def _():
        a_scr[...] = _modulate(h_ref[...], g_ref[...], sh_ref[0], sc_ref[0]).astype(BF16)

    acc = jnp.dot(a_scr[...], w_ref[...], preferred_element_type=F32)

    def norm_rope(cos_ref, sin_ref):
        lane = lax.broadcasted_iota(I32, (1, HEAD_DIM), 1)
        first = (lane % (HEAD_DIM // 2)) < (HEAD_DIM // 4)
        for r0 in range(0, acc.shape[0], rc):
            cos = cos_ref[r0:r0 + rc, :]
            sin = sin_ref[r0:r0 + rc, :]
            for hd in range(tn // HEAD_DIM):
                lanes = slice(hd * HEAD_DIM, (hd + 1) * HEAD_DIM)
                x = acc[r0:r0 + rc, lanes]
                r = lax.rsqrt(jnp.mean(x * x, axis=-1, keepdims=True) + NORM_EPS)
                rot = jnp.where(first, pltpu.roll(x, HEAD_DIM - HEAD_DIM // 4, 1), pltpu.roll(x, HEAD_DIM // 4, 1))
                o_ref[r0:r0 + rc, lanes] = ((x * cos + rot * sin) * r).astype(BF16)

    @pl.when(n < n_q)
    def _():
        norm_rope(cq_ref, sq_ref)

    @pl.when(n == n_q)
    def _():
        norm_rope(ck_ref, sk_ref)

    @pl.when(n > n_q)
    def _():
        o_ref[...] = acc.astype(BF16)


def _rope_tables(dims, tm, gain, scale):
    quarter = HEAD_DIM // 4
    inv_freq = ROPE_THETA ** (-jnp.arange(quarter, dtype=F32) * 2.0 / (HEAD_DIM // 2))
    t = jnp.arange(dims.seq)
    ang_row = (t // dims.grid_w).astype(F32)[:, None] * inv_freq
    ang_col = (t % dims.grid_w).astype(F32)[:, None] * inv_freq
    cos = jnp.concatenate([jnp.cos(ang_row)] * 2 + [jnp.cos(ang_col)] * 2, axis=-1)
    sin = jnp.concatenate([-jnp.sin(ang_row), jnp.sin(ang_row), -jnp.sin(ang_col), jnp.sin(ang_col)], axis=-1)
    cos = jnp.concatenate([cos, jnp.ones((tm, HEAD_DIM), F32)], axis=0)
    sin = jnp.concatenate([sin, jnp.zeros((tm, HEAD_DIM), F32)], axis=0)
    g = gain.reshape(2, 2, quarter)
    partner_gain = g[:, ::-1, :].reshape(HEAD_DIM)
    return cos * (gain * scale), sin * (partner_gain * scale)


def _qkvproj(dims, st, h, mods, gain, w_qkv, q_gain, k_gain, with_q):
    rows, d = h.shape
    tm, tn = st.tm(), dims.dkv
    n_q = dims.dq // tn if with_q else 0
    col0 = 0 if with_q else dims.dq // tn
    n_total = n_q + 2
    cos_q, sin_q = _rope_tables(dims, tm, q_gain, HEAD_DIM ** -0.5)
    cos_k, sin_k = _rope_tables(dims, tm, k_gain, 1.0)
    per = dims.seq // tm
    pos_tile = (lambda m, n: (m % per, 0)) if with_q else (lambda m, n: (per, 0))
    return pl.pallas_call(
        functools.partial(_qkv_body, n_q=n_q, tn=tn, rc=tm),
        grid=(rows // tm, n_total),
        in_specs=[
            pl.BlockSpec((tm, d), lambda m, n: (m, 0)),
            pl.BlockSpec((1, d), lambda m, n: (0, 0)),
            _mod_spec(st, tm, 0, d),
            _mod_spec(st, tm, 1, d),
            pl.BlockSpec((d, tn), lambda m, n: (0, col0 + n)),
            pl.BlockSpec((tm, HEAD_DIM), pos_tile),
            pl.BlockSpec((tm, HEAD_DIM), pos_tile),
            pl.BlockSpec((tm, HEAD_DIM), pos_tile),
            pl.BlockSpec((tm, HEAD_DIM), pos_tile),
        ],
        out_specs=pl.BlockSpec((tm, tn), lambda m, n: (m, n)),
        out_shape=jax.ShapeDtypeStruct((rows, n_total * tn), BF16),
        scratch_shapes=[pltpu.VMEM((tm, d), BF16)],
        compiler_params=_params("arbitrary", "arbitrary"),
        name="attn_qkv",
    )(h, gain.reshape(1, d), mods, mods, w_qkv, cos_q, sin_q, cos_k, sin_k)


def _attn_body(sink_ref, q_ref, k_ref, v_ref, kc_ref, vc_ref, o_ref, *, dims):
    group = dims.heads // dims.kv_heads
    blk = ATTN_BLOCK
    band = 3 * blk
    n_ctx = dims.ctx
    head0 = pl.program_id(1) * group
    kc = kc_ref[...]
    vc = vc_ref[...]

    n_blk = dims.seq // blk
    assert n_blk >= 3
    sink = jnp.concatenate([jnp.full((blk, 1), sink_ref[head0 + g], F32) for g in range(group)], axis=0)

    def mask_bias(first_key):
        q_rel = first_key + (lax.broadcasted_iota(I32, (group * blk, 1), 0) & (blk - 1))
        col = lax.broadcasted_iota(I32, (1, band + n_ctx), 1)
        valid = (col >= band) | (jnp.abs(q_rel - col) <= blk)
        return jnp.where(valid, 0.0, MASK_VALUE)

    def block(n, start, bias):
        q0 = n * blk if isinstance(n, int) else pl.multiple_of(n * blk, blk)
        kcat = jnp.concatenate([k_ref[pl.ds(start, band), :], kc], axis=0)
        vcat = jnp.concatenate([v_ref[pl.ds(start, band), :], vc], axis=0)
        q = jnp.concatenate([q_ref[pl.ds(q0, blk), g * HEAD_DIM:(g + 1) * HEAD_DIM] for g in range(group)], axis=0)
        s = lax.dot_general(q, kcat, (((1,), (1,)), ((), ())), preferred_element_type=F32) + bias
        m = jnp.maximum(jnp.max(s, axis=-1, keepdims=True), sink)
        p = jnp.exp(s - m)
        den = jnp.sum(p, axis=-1, keepdims=True) + jnp.exp(sink - m)
        o = jnp.dot(p.astype(BF16), vcat, preferred_element_type=F32) / den
        for g in range(group):
            o_ref[pl.ds(q0, blk), g * HEAD_DIM:(g + 1) * HEAD_DIM] = o[g * blk:(g + 1) * blk].astype(BF16)

    block(0, 0, mask_bias(0))
    mid_bias = mask_bias(blk)

    def middle(n, carry):
        block(n, pl.multiple_of((n - 1) * blk, blk), mid_bias)
        return carry

    lax.fori_loop(1, n_blk - 1, middle, 0, unroll=2)
    block(n_blk - 1, dims.seq - band, mask_bias(2 * blk))


def _attention(dims, qkv, kv_ctx, sink):
    group = dims.heads // dims.kv_heads
    gw = group * HEAD_DIM
    k_col = dims.dq // HEAD_DIM
    v_col = (dims.dq + dims.dkv) // HEAD_DIM
    return pl.pallas_call(
        functools.partial(_attn_body, dims=dims),
        grid=(dims.batch, dims.kv_heads),
        in_specs=[
            pl.BlockSpec(memory_space=pltpu.SMEM),
            pl.BlockSpec((dims.seq, gw), lambda b, h: (b, h)),
            pl.BlockSpec((dims.seq, HEAD_DIM), lambda b, h: (b, k_col + h)),
            pl.BlockSpec((dims.seq, HEAD_DIM), lambda b, h: (b, v_col + h)),
            pl.BlockSpec((dims.ctx, HEAD_DIM), lambda b, h: (b, h)),
            pl.BlockSpec((dims.ctx, HEAD_DIM), lambda b, h: (b, dims.kv_heads + h)),
        ],
        out_specs=pl.BlockSpec((dims.seq, gw), lambda b, h: (b, h)),
        out_shape=jax.ShapeDtypeStruct((dims.batch * dims.seq, dims.dq), BF16),
        compiler_params=_params("arbitrary", "arbitrary"),
        name="attn_core",
    )(sink, qkv, qkv, qkv, kv_ctx, kv_ctx)


def _lane_prefix(x, tri):
    n = x.shape[1]
    off = jnp.zeros((x.shape[0], 1), F32)
    chunks = []
    for ch in range(n // 128):
        xc = x[:, ch * 128:(ch + 1) * 128]
        incl = jnp.dot(xc.astype(BF16), tri, preferred_element_type=F32)
        chunks.append(incl - xc + off)
        off = off + incl[:, 127:128]
    return chunks[0] if len(chunks) == 1 else jnp.concatenate(chunks, axis=1)


def _topk_body(aff_ref, idx_ref, dst_ref, gate_ref, off_ref, cnt_ref, w_scr, a_scr, q_scr, clo_scr, chi_scr, *,
               n_tok, cap, pair0, experts):
    grp = pl.program_id(0)
    a = aff_ref[...]
    bits = pltpu.bitcast(a, I32)
    lane = lax.broadcasted_iota(I32, (experts, n_tok), 1)

    def count(mask):
        return jnp.sum(mask.astype(I32), axis=1, keepdims=True)

    def thr_step(i, thr):
        hi = jnp.left_shift(jnp.int32(1), 29 - 2 * i)
        lo = jnp.left_shift(jnp.int32(1), 28 - 2 * i)
        best = thr
        for cand in (thr | lo, thr | hi, thr | hi | lo):
            best = jnp.where(count(bits >= cand) >= cap, cand, best)
        return best

    first = jnp.int32(1 << 30)
    thr = jnp.where(count(bits >= first) >= cap, first, jnp.zeros((experts, 1), I32))
    thr = lax.fori_loop(0, 15, thr_step, thr)
    above = bits > thr
    tie = bits == thr
    need = cap - count(above)

    top_bit = n_tok.bit_length() - 2

    def tie_try(lim, cands):
        for cand in cands:
            lim = jnp.where(count(tie & (lane < cand)) < need, cand, lim)
        return lim

    def tie_step(i, lim):
        hi = jnp.left_shift(jnp.int32(1), top_bit - odd - 2 * i)
        lo = jnp.left_shift(jnp.int32(1), top_bit - odd - 2 * i - 1)
        return tie_try(lim, (lim | lo, lim | hi, lim | hi | lo))

    odd = (top_bit + 1) % 2
    lim = jnp.zeros((experts, 1), I32)
    if odd:
        lim = tie_try(lim, (lim | (1 << top_bit),))
    lim = lax.fori_loop(0, (top_bit + 1) // 2, tie_step, lim)
    sel = above | (tie & (lane <= lim))
    sel_f = sel.astype(F32)
    sel_i = sel.astype(I32)

    r = lax.broadcasted_iota(I32, (128, 128), 0)
    c = lax.broadcasted_iota(I32, (128, 128), 1)
    tri = (r <= c).astype(F32).astype(BF16)

    level = jnp.zeros((1, n_tok), I32)
    levels = []
    for e in range(experts):
        levels.append(level)
        level = level + sel_i[e:e + 1, :]
    cnt = level
    off = _lane_prefix(cnt.astype(F32), tri).astype(I32) + (pair0 + grp * (experts * cap))
    off_ref[...] = off
    cnt_ref[...] = cnt
    pair_row = jnp.concatenate(levels, axis=0) + off

    n_ch = n_tok // 128
    pad = w_scr.shape[0] // experts
    lane_e = lax.broadcasted_iota(I32, (experts, 128), 1)
    w_scr[...] = jnp.zeros_like(w_scr)
    a_scr[...] = jnp.zeros_like(a_scr)
    q_scr[...] = jnp.zeros_like(q_scr)
    first = jnp.zeros((experts, 1), F32)
    c_lo = jnp.full((experts, 128), cap, I32)
    c_hi = jnp.full((experts, 128), cap, I32)
    for ch in range(n_ch):
        lanes = slice(ch * 128, (ch + 1) * 128)
        incl = jnp.dot(sel_f[:, lanes].astype(BF16), tri, preferred_element_type=F32)
        w_scr[pl.ds(ch, experts, stride=pad), :] = incl
        a_scr[pl.ds(ch, experts, stride=pad), :] = a[:, lanes]
        q_scr[pl.ds(ch, experts, stride=pad), :] = pair_row[:, lanes]
        nxt = first + incl[:, 127:128]
        c_lo = jnp.where(lane_e == ch, first.astype(I32), c_lo)
        c_hi = jnp.where(lane_e == ch, nxt.astype(I32), c_hi)
        first = nxt
    clo_scr[...] = c_lo
    chi_scr[...] = c_hi

    idx_ref[0] = jnp.zeros((cap, 128), I32)
    dst_ref[0] = jnp.zeros((cap, 128), I32)
    gate_ref[0] = jnp.zeros((cap, 128), F32)
    slot = lax.broadcasted_iota(I32, (cap, 1), 0)
    lane = lax.broadcasted_iota(I32, (1, 128), 1)
    zpad = jnp.zeros((128 - pad, 128), F32)

    def rhs(m):
        return jnp.concatenate([m, zpad], axis=0).astype(BF16)

    def per_expert(e, carry):
        base = pl.multiple_of(e * pad, pad)
        wm = w_scr[pl.ds(base, pad), :]
        am = a_scr[pl.ds(base, pad), :]
        qm = q_scr[pl.ds(base, pad), :]
        lo = clo_scr[pl.ds(e, 1), :]
        hi = chi_scr[pl.ds(e, 1), :]
        in_chunk = (lo <= slot) & (slot < hi)
        onehot = in_chunk.astype(F32).astype(BF16)

        def rows_of(m):
            return jnp.dot(onehot, rhs(m), preferred_element_type=F32)

        a1 = am.astype(BF16).astype(F32)
        a2 = (am - a1).astype(BF16).astype(F32)
        a3 = am - a1 - a2
        a_rows = rows_of(a1) + rows_of(a2) + rows_of(a3)
        q_rows = (rows_of(lax.shift_right_logical(qm, 8).astype(F32)) * 256.0
                  + rows_of((qm & 255).astype(F32)))
        w_rows = rows_of(wm)
        chunk_first = jnp.sum(jnp.where(in_chunk, lo, 0), axis=1, keepdims=True)
        chunk = jnp.sum(jnp.where(in_chunk, lane, 0), axis=1, keepdims=True)
        rank = (slot - chunk_first).astype(F32)
        pos = jnp.sum((w_rows <= rank).astype(I32), axis=1, keepdims=True)
        here = lane == pos
        gate = jnp.sum(jnp.where(here, a_rows, 0.0), axis=1, keepdims=True)
        dst = jnp.sum(jnp.where(here, q_rows, 0.0), axis=1, keepdims=True).astype(I32)
        token = chunk * 128 + pos + grp * n_tok
        mine = lane == e
        idx_ref[0] = jnp.where(mine, token, idx_ref[0])
        dst_ref[0] = jnp.where(mine, dst, dst_ref[0])
        gate_ref[0] = jnp.where(mine, gate, gate_ref[0])
        return carry

    lax.fori_loop(0, experts, per_expert, 0)


def _topk(st, aff, cap, pair0):
    e = aff.shape[0]
    n_tok, n_groups = st.seq, st.batch
    pad = max(8, n_tok // 128)
    assert e <= 128 and pad <= 128 and n_groups * e * cap < 2 ** 16 * 256
    tile_spec = pl.BlockSpec((1, cap, 128), lambda g: (g, 0, 0))
    return pl.pallas_call(
        functools.partial(_topk_body, n_tok=n_tok, cap=cap, pair0=pair0, experts=e),
        grid=(n_groups,),
        in_specs=[pl.BlockSpec((e, n_tok), lambda g: (0, g))],
        out_specs=[tile_spec, tile_spec, tile_spec,
                   pl.BlockSpec((1, n_tok), lambda g: (0, g)),
                   pl.BlockSpec((1, n_tok), lambda g: (0, g))],
        out_shape=[jax.ShapeDtypeStruct((n_groups, cap, 128), I32),
                   jax.ShapeDtypeStruct((n_groups, cap, 128), I32),
                   jax.ShapeDtypeStruct((n_groups, cap, 128), F32),
                   jax.ShapeDtypeStruct((1, n_groups * n_tok), I32),
                   jax.ShapeDtypeStruct((1, n_groups * n_tok), I32)],
        scratch_shapes=[pltpu.VMEM((e * pad, 128), F32), pltpu.VMEM((e * pad, 128), F32),
                        pltpu.VMEM((e * pad, 128), I32), pltpu.VMEM((e, 128), I32), pltpu.VMEM((e, 128), I32)],
        compiler_params=_params("arbitrary"),
        name="moe_topk",
    )(aff)


def _expert_body(*refs, n_src, src_slots, slots, rc, n_f, n_n, tf, n_e):
    idx_ref, idx_nxt_ref, dst_ref, dst_prv_ref = refs[0:4]
    srcs = refs[4:4 + n_src]
    gate_ref, wg_ref, wu_ref, wd_ref, pairs_hbm, x_scr, y_scr, mid_scr, gcol_scr, gsem, ssem = refs[4 + n_src:]
    e = pl.program_id(0)
    s = pl.program_id(1)
    nck = x_scr.shape[0] // slots
    half = nck * 128
    tnc = nck // n_n

    def tok(ref, r):
        return ref.at[pl.ds(pl.multiple_of(r * nck, nck), nck), :]

    def row_loop(lo, n, start_row):
        def eight(i, carry):
            for k in range(8):
                start_row(lo + i * 8 + k)
            return carry
        lax.fori_loop(0, n // 8, eight, 0)

    def gather_row(ref, r, slot):
        src = next(src for (s0, s1), src in zip(src_slots, srcs) if s0 <= slot < s1)
        pltpu.make_async_copy(tok(src, ref[0, 0, r]), tok(x_scr, r), gsem).start()

    def scatter_row(ref, r):
        pltpu.make_async_copy(tok(y_scr, r), tok(pairs_hbm, ref[0, 0, r]), ssem).start()

    n_chunks = slots // rc
    dma_chunks = max(n_chunks - 1, 1)

    def share(total, ci):
        base, extra = divmod(total, dma_chunks)
        if ci >= dma_chunks:
            return total, 0
        return ci * base + min(ci, extra), base + (1 if ci < extra else 0)

    def wait_gather():
        pltpu.make_async_copy(srcs[0].at[pl.ds(0, slots * nck), :], x_scr, gsem).wait()

    def wait_scatter():
        pltpu.make_async_copy(y_scr, pairs_hbm.at[pl.ds(0, slots * nck), :], ssem).wait()

    def x_rows(r0):
        parts = [_unpack_bf16_pair(x_scr[pl.ds(r0 * nck + c, rc, stride=nck), :]) for c in range(nck)]
        return (jnp.concatenate([p[0] for p in parts], axis=1), jnp.concatenate([p[1] for p in parts], axis=1))

    @pl.when((e == 0) & (s == 0))
    def _():
        y_scr[...] = jnp.zeros_like(y_scr)
        for (s0, s1), src in zip(src_slots, srcs):
            row_loop(s0, s1 - s0, lambda r, src=src: pltpu.make_async_copy(
                tok(src, idx_ref[0, 0, r]), tok(x_scr, r), gsem).start())

    @pl.when(s == 0)
    def _():
        wait_gather()
        eye = lax.broadcasted_iota(I32, (128, 128), 0) == lax.broadcasted_iota(I32, (128, 128), 1)
        for c in range(slots // 128):
            row = gate_ref[0, c:c + 1, :]
            gcol_scr[c * 128:(c + 1) * 128, :] = jnp.sum(jnp.where(eye, row, 0.0), axis=1, keepdims=True)

    @pl.when(s < n_f)
    def _():
        wg = wg_ref[0, 0].astype(BF16)
        wu = wu_ref[0, 0].astype(BF16)
        for r0 in range(0, slots, rc):
            lo, hi = x_rows(r0)
            a = (jnp.dot(lo, wg[:half], preferred_element_type=F32)
                 + jnp.dot(hi, wg[half:], preferred_element_type=F32))
            u = (jnp.dot(lo, wu[:half], preferred_element_type=F32)
                 + jnp.dot(hi, wu[half:], preferred_element_type=F32))
            mid_scr[s, r0:r0 + rc, :] = (a * jax.nn.sigmoid(a) * u).astype(BF16)
            off, cnt = share(slots // n_f, r0 // rc)
            for k in range(cnt):
                scatter_row(dst_prv_ref, s * (slots // n_f) + off + k)

    @pl.when(s == n_f)
    def _():
        wait_scatter()

    for j in range(n_n):
        @pl.when(s == n_f + j)
        def _():
            wd = wd_ref[0, 0].astype(BF16)
            for r0 in range(0, slots, rc):
                y = jnp.dot(mid_scr[0, r0:r0 + rc, :], wd[0:tf], preferred_element_type=F32)
                for f in range(1, n_f):
                    y += jnp.dot(mid_scr[f, r0:r0 + rc, :], wd[f * tf:(f + 1) * tf], preferred_element_type=F32)
                y = y * gcol_scr[r0:r0 + rc, :]
                tnw = tnc * 128
                packed = _pack_bf16_pair(y[:, :tnw], y[:, tnw:])
                for c in range(tnc):
                    y_scr[pl.ds(r0 * nck + j * tnc + c, rc, stride=nck), :] = packed[:, c * 128:(c + 1) * 128]
                g_steps = max(n_n - 1, 1)
                step_rows = -(-slots // g_steps)
                lo = min(j * step_rows, slots)
                off, cnt = share(min(step_rows, slots - lo), r0 // rc)
                for k in range(cnt):
                    slot = lo + off + k
                    gather_row(idx_nxt_ref, slot, slot)

    @pl.when((e == n_e - 1) & (s == n_f + n_n - 1))
    def _():
        wait_gather()
        row_loop(0, slots, lambda r: scatter_row(dst_ref, r))
        wait_scatter()


def _experts(idx, dst, gate, sources, src_slots, w_gate, w_up, w_down, layer, n_pairs):
    n_e, slots = idx.shape
    d, fdim = w_gate.shape[2], w_gate.shape[3]
    tf, tn = min(512, fdim), EXPERT_DOWN_TILE
    n_f, n_n = fdim // tf, d // tn
    rc = slots // 4
    nck = d // 256
    assert slots % 128 == 0 and rc % 16 == 0 and nck % n_n == 0 and slots % n_f == 0 and slots % n_n == 0
    assert all(s0 % 8 == 0 and s1 % 8 == 0 for s0, s1 in src_slots)
    idx3 = idx.reshape(n_e, 1, slots)
    dst3 = dst.reshape(n_e, 1, slots)
    smem = lambda f: pl.BlockSpec((1, 1, slots), f, memory_space=pltpu.SMEM)
    up_chunk = lambda i, s: (layer, i, 0, jnp.minimum(s, n_f - 1))
    return pl.pallas_call(
        functools.partial(_expert_body, n_src=len(sources), src_slots=src_slots, slots=slots, rc=rc,
                          n_f=n_f, n_n=n_n, tf=tf, n_e=n_e),
        grid=(n_e, n_f + n_n),
        in_specs=[
            smem(lambda i, s: (i, 0, 0)),
            smem(lambda i, s: (jnp.minimum(i + 1, n_e - 1), 0, 0)),
            smem(lambda i, s: (i, 0, 0)),
            smem(lambda i, s: (jnp.maximum(i - 1, 0), 0, 0)),
        ] + [pl.BlockSpec(memory_space=pl.ANY)] * len(sources) + [
            pl.BlockSpec((1, slots // 128, 128), lambda i, s: (i, 0, 0)),
            pl.BlockSpec((1, 1, d, tf), up_chunk),
            pl.BlockSpec((1, 1, d, tf), up_chunk),
            pl.BlockSpec((1, 1, fdim, tn), lambda i, s: (layer, i, 0, jnp.maximum(s - n_f, 0))),
        ],
        out_specs=pl.BlockSpec(memory_space=pl.ANY),
        out_shape=jax.ShapeDtypeStruct((n_pairs * nck, 128), I32),
        scratch_shapes=[
            pltpu.VMEM((slots * nck, 128), I32),
            pltpu.VMEM((slots * nck, 128), I32),
            pltpu.VMEM((n_f, slots, tf), BF16),
            pltpu.VMEM((slots, 1), F32),
            pltpu.SemaphoreType.DMA(()),
            pltpu.SemaphoreType.DMA(()),
        ],
        compiler_params=_params("arbitrary", "arbitrary"),
        name="moe_experts",
    )(idx3, idx3, dst3, dst3, *sources, gate.reshape(n_e, slots // 128, 128), w_gate, w_up, w_down)


def _combine_body(lo_ref, hi_ref, off_ref, cnt_ref, p_ref, h_ref, g2_ref, o_ref, acc_lo, acc_hi, *,
                  tt, nck, tnw, pair0, pairs_per_group):
    base = pair0 + pl.program_id(0) * pairs_per_group
    i = pl.program_id(1)
    lane = lax.broadcasted_iota(I32, (1, PAIR_BLOCK), 1)
    eye = lax.broadcasted_iota(I32, (128, 128), 0) == lax.broadcasted_iota(I32, (128, 128), 1)

    def column(row):
        cols = [jnp.sum(jnp.where(eye, row[:, c0:c0 + 128], 0), axis=1, keepdims=True) for c0 in range(0, tt, 128)]
        return cols[0] if len(cols) == 1 else jnp.concatenate(cols, axis=0)

    first = column(off_ref[...]) - base
    last = first + column(cnt_ref[...])
    shift = PAIR_BLOCK.bit_length() - 1
    k0 = lax.shift_right_logical(lo_ref[0, 0, i] - base, shift)
    k1 = lax.shift_right_logical(hi_ref[0, 0, i] - base + (PAIR_BLOCK - 1), shift)
    acc_lo[...] = jnp.zeros_like(acc_lo)
    acc_hi[...] = jnp.zeros_like(acc_hi)

    def step(k, carry):
        p0 = pl.multiple_of(k * PAIR_BLOCK, PAIR_BLOCK)
        parts = [_unpack_bf16_pair(p_ref[pl.ds(p0 * nck + c, PAIR_BLOCK, stride=nck), :]) for c in range(nck)]
        lo = jnp.concatenate([p[0] for p in parts], axis=1)
        hi = jnp.concatenate([p[1] for p in parts], axis=1)
        pr = p0 + lane
        seg = ((pr >= first) & (pr < last)).astype(F32).astype(BF16)
        acc_lo[...] += jnp.dot(seg, lo, preferred_element_type=F32)
        acc_hi[...] += jnp.dot(seg, hi, preferred_element_type=F32)
        return carry

    lax.fori_loop(k0, k1, step, 0)
    for j in range(nck * 128 // tnw):
        words = slice(j * tnw, (j + 1) * tnw)
        c_lo = slice(2 * j * tnw, (2 * j + 1) * tnw)
        c_hi = slice((2 * j + 1) * tnw, (2 * j + 2) * tnw)
        o_ref[:, c_lo] = h_ref[:, c_lo] + g2_ref[0, :, c_lo] * acc_lo[:, words]
        o_ref[:, c_hi] = h_ref[:, c_hi] + g2_ref[0, :, c_hi] * acc_hi[:, words]


def _combine(st, off, cnt, pairs, h, mods, pair0, pairs_per_group):
    rows, d = h.shape
    n_tok = st.seq
    tt = min(256, n_tok)
    n_tiles = n_tok // tt
    nck = d // 256
    assert pairs_per_group % PAIR_BLOCK == 0 and pair0 % pairs_per_group == 0
    off_t = off.reshape(st.batch, 1, n_tiles, tt)
    cnt_t = cnt.reshape(st.batch, 1, n_tiles, tt)
    tile_lo = off_t[..., 0]
    tile_hi = off_t[..., tt - 1] + cnt_t[..., tt - 1]
    row = st.ada_row(n_tok)
    smem = pl.BlockSpec((1, 1, n_tiles), lambda b, i: (b, 0, 0), memory_space=pltpu.SMEM)
    tile = lambda b, i: (b * n_tiles + i, 0)
    return pl.pallas_call(
        functools.partial(_combine_body, tt=tt, nck=nck, tnw=EXPERT_DOWN_TILE // 2, pair0=pair0,
                          pairs_per_group=pairs_per_group),
        grid=(st.batch, n_tiles),
        in_specs=[
            smem, smem,
            pl.BlockSpec((1, tt), lambda b, i: (0, b * n_tiles + i)),
            pl.BlockSpec((1, tt), lambda b, i: (0, b * n_tiles + i)),
            pl.BlockSpec((pairs_per_group * nck, 128), lambda b, i: (pair0 // pairs_per_group + b, 0)),
            pl.BlockSpec((tt, d), tile),
            pl.BlockSpec((1, 1, d), lambda b, i: (row(b) * N_ADA + 5, 0, 0)),
        ],
        out_specs=pl.BlockSpec((tt, d), tile),
        out_shape=jax.ShapeDtypeStruct((rows, d), F32),
        scratch_shapes=[pltpu.VMEM((tt, d // 2), F32)] * 2,
        compiler_params=_params("arbitrary", "arbitrary"),
        name="moe_combine",
    )(tile_lo, tile_hi, off, cnt, pairs, h, mods)


def _moe(streams, mixed, mods, w_gate, w_up, w_down, layer):
    n_e = mixed[0][2].shape[0]
    routed = []
    pair0 = 0
    slot0 = 0
    src_slots = []
    for st, (h, h2, aff) in zip(streams, mixed):
        cap = CAP_FACTOR * st.seq // n_e
        idx, dst, gate, off, cnt = _topk(st, aff, cap, pair0)
        routed.append((h2, idx, dst, gate, off, cnt, pair0, n_e * cap))
        src_slots.append((slot0, slot0 + st.batch * cap))
        pair0 += st.batch * n_e * cap
        slot0 += st.batch * cap
    flat = lambda t: t[:, :, :n_e].transpose(2, 0, 1).reshape(n_e, -1)
    idx = jnp.concatenate([flat(r[1]) for r in routed], axis=1)
    dst = jnp.concatenate([flat(r[2]) for r in routed], axis=1)
    gate = jnp.concatenate([flat(r[3]) for r in routed], axis=1)
    pairs = _experts(idx, dst, gate, [r[0] for r in routed], tuple(src_slots), w_gate, w_up, w_down, layer, pair0)
    return [_combine(st, r[4], r[5], pairs, m[0], mods, r[6], r[7]) for st, m, r in zip(streams, mixed, routed)]


def _forward(dims, x, c, ctx, c_ctx, ada_w, ada_b, norm_mix_g, norm_ffn_g, conv_w_in, conv_w, conv_w_out,
             attn_w_qkv, attn_q_gain, attn_k_gain, attn_sink, attn_w_o, router_w,
             expert_w_gate, expert_w_up, expert_w_down):
    d = dims.d
    lat = Stream(dims.batch, dims.seq)
    con = Stream(dims.batch, dims.ctx, shared_row=dims.batch)
    cvec = jnp.zeros((ADA_ROWS, d), F32).at[:dims.batch].set(c).at[dims.batch].set(c_ctx)
    mods = _adaln(cvec, ada_w, ada_b)
    mods = mods.reshape(mods.shape[0], ADA_ROWS * N_ADA, 1, d)
    hs = [x.reshape(lat.rows, d), ctx.reshape(con.rows, d)]

    w_in, w_out = conv_w_in[0].astype(BF16), conv_w_out[0].astype(BF16)
    mixed = []
    for st, h in zip((lat, con), hs):
        gb_u = _inproj(st, h, mods[0], norm_mix_g[0], w_in)
        mixed.append(_mixout(st, gb_u, conv_w[0], w_out, h, mods[0], norm_ffn_g[0], router_w[0]))
    h_lat, h_ctx = _moe((lat, con), mixed, mods[0], expert_w_gate, expert_w_up, expert_w_down, 0)

    w_qkv, w_o = attn_w_qkv[0].astype(BF16), attn_w_o[0].astype(BF16)
    qkv = _qkvproj(dims, lat, h_lat, mods[1], norm_mix_g[1], w_qkv, attn_q_gain[0], attn_k_gain[0], True)
    kv_ctx = _qkvproj(dims, con, h_ctx, mods[1], norm_mix_g[1], w_qkv, attn_q_gain[0], attn_k_gain[0], False)
    o = _attention(dims, qkv, kv_ctx, attn_sink[0])
    mixed = [_mixout(lat, (o,), None, w_o, h_lat, mods[1], norm_ffn_g[1], router_w[1])]
    (h_lat,) = _moe((lat,), mixed, mods[1], expert_w_gate, expert_w_up, expert_w_down, 1)
    return h_lat.reshape(dims.batch, dims.seq, d)


def kernel(x, c, ctx, c_ctx, ada_w, ada_b, norm_mix_g, norm_ffn_g, conv_w_in, conv_w, conv_w_out, attn_w_qkv, attn_q_gain, attn_k_gain, attn_sink, attn_w_o, router_w, expert_w_gate, expert_w_up, expert_w_down):
    batch, seq, d = x.shape
    dims = Dims(d=d, batch=batch, seq=seq, grid_w=GRID_W, ctx=ctx.shape[1], heads=d // HEAD_DIM,
                kv_heads=(attn_w_qkv.shape[2] // HEAD_DIM - d // HEAD_DIM) // 2,
                experts=router_w.shape[2], d_expert=expert_w_gate.shape[3])
    return _forward(dims, x, c, ctx, c_ctx, ada_w, ada_b, norm_mix_g, norm_ffn_g, conv_w_in, conv_w, conv_w_out,
                    attn_w_qkv, attn_q_gain, attn_k_gain, attn_sink, attn_w_o, router_w,
                    expert_w_gate, expert_w_up, expert_w_down)
```

```python
import dataclasses
import functools

import jax
import jax.numpy as jnp
from jax import lax
from jax.experimental import pallas as pl
from jax.experimental.pallas import tpu as pltpu

F32 = jnp.float32
BF16 = jnp.bfloat16
I32 = jnp.int32

NORM_EPS = 1e-6
MASK_VALUE = -1e30
ROPE_THETA = 10000.0
GRID_W = 64
CAP_FACTOR = 2
HEAD_DIM = 128
ATTN_BLOCK = 128
N_ADA = 6
ADA_ROWS = 16
HI_MASK = -65536
PAIR_BLOCK = 256
EXPERT_DOWN_TILE = 512
VMEM_LIMIT = 58 * 1024 * 1024


@dataclasses.dataclass(frozen=True)
class Dims:
    d: int
    batch: int
    seq: int
    grid_w: int
    ctx: int
    heads: int
    kv_heads: int
    experts: int
    d_expert: int

    @property
    def dq(self):
        return self.heads * HEAD_DIM

    @property
    def dkv(self):
        return self.kv_heads * HEAD_DIM


@dataclasses.dataclass(frozen=True)
class Stream:
    batch: int
    seq: int
    shared_row: int = -1

    @property
    def rows(self):
        return self.batch * self.seq

    def tm(self, cap=1024):
        unit = self.rows if self.shared_row >= 0 else self.seq
        t = cap
        while unit % t:
            t //= 2
        return t

    def ada_row(self, tm):
        if self.shared_row >= 0:
            return lambda m: self.shared_row
        per = self.seq // tm
        assert per >= 1
        return lambda m: m // per


def _params(*sem):
    return pltpu.CompilerParams(dimension_semantics=sem, vmem_limit_bytes=VMEM_LIMIT)


def _modulate(h, gain, shift, scale):
    ms = jnp.mean(h * h, axis=-1, keepdims=True)
    xn = h * lax.rsqrt(ms + NORM_EPS)
    return (xn * gain) * (1.0 + scale) + shift


def _mod_spec(st, tm, which, d, width=None):
    row = st.ada_row(tm)
    if width is None:
        return pl.BlockSpec((1, 1, d), lambda m, n: (row(m) * N_ADA + which, 0, 0))
    return pl.BlockSpec((1, 1, width), lambda m, n: (row(m) * N_ADA + which, 0, n))


def _pack_bf16_pair(lo, hi):
    lo_bits = pltpu.bitcast(lo.astype(BF16).astype(F32), I32)
    hi_bits = pltpu.bitcast(hi.astype(BF16).astype(F32), I32)
    return (hi_bits & HI_MASK) | lax.shift_right_logical(lo_bits, 16)


def _unpack_bf16_pair(w):
    lo = pltpu.bitcast(lax.shift_left(w, 16), F32).astype(BF16)
    hi = pltpu.bitcast(w & HI_MASK, F32).astype(BF16)
    return lo, hi


def _adaln_body(c_ref, w_ref, b_ref, o_ref):
    c = c_ref[...]
    s = c * jax.nn.sigmoid(c)
    o_ref[0] = jnp.dot(s.astype(BF16), w_ref[0].astype(BF16), preferred_element_type=F32) + b_ref[0]


def _adaln(cvec, ada_w, ada_b):
    depth, d, n = ada_w.shape
    tn = min(1024, n)
    return pl.pallas_call(
        _adaln_body,
        grid=(depth, n // tn),
        in_specs=[
            pl.BlockSpec((ADA_ROWS, d), lambda l, j: (0, 0)),
            pl.BlockSpec((1, d, tn), lambda l, j: (l, 0, j)),
            pl.BlockSpec((1, 1, tn), lambda l, j: (l, 0, j)),
        ],
        out_specs=pl.BlockSpec((1, ADA_ROWS, tn), lambda l, j: (l, 0, j)),
        out_shape=jax.ShapeDtypeStruct((depth, ADA_ROWS, n), F32),
        compiler_params=_params("arbitrary", "arbitrary"),
        name="adaln",
    )(cvec, ada_w, ada_b.reshape(depth, 1, n))


def _inproj_body(h_ref, g_ref, sh_ref, sc_ref, wb_ref, wc_ref, wx_ref, gb_ref, u_ref, a_scr):
    def project(rows):
        a = a_scr[rows, :]
        gb = jnp.dot(a, wb_ref[...], preferred_element_type=F32)
        gc = jnp.dot(a, wc_ref[...], preferred_element_type=F32)
        xv = jnp.dot(a, wx_ref[...], preferred_element_type=F32)
        gb_ref[rows, :] = gb.astype(BF16)
        u_ref[rows, :] = (gc * xv).astype(BF16)

    tm = a_scr.shape[0]

    @pl.when(pl.program_id(1) == 0)
    def _():
        for r0 in range(0, tm, tm // 4):
            rows = slice(r0, r0 + tm // 4)
            a_scr[rows, :] = _modulate(h_ref[rows, :], g_ref[...], sh_ref[0], sc_ref[0]).astype(BF16)
            project(rows)

    @pl.when(pl.program_id(1) > 0)
    def _():
        project(slice(0, tm))


def _inproj(st, h, mods, gain, w_in):
    rows, d = h.shape
    tm, tn = st.tm(), 512
    nt = d // tn
    return pl.pallas_call(
        _inproj_body,
        grid=(rows // tm, nt),
        in_specs=[
            pl.BlockSpec((tm, d), lambda m, n: (m, 0)),
            pl.BlockSpec((1, d), lambda m, n: (0, 0)),
            _mod_spec(st, tm, 0, d),
            _mod_spec(st, tm, 1, d),
            pl.BlockSpec((d, tn), lambda m, n: (0, n)),
            pl.BlockSpec((d, tn), lambda m, n: (0, nt + n)),
            pl.BlockSpec((d, tn), lambda m, n: (0, 2 * nt + n)),
        ],
        out_specs=[pl.BlockSpec((tm, tn), lambda m, n: (m, n))] * 2,
        out_shape=[jax.ShapeDtypeStruct((rows, d), BF16)] * 2,
        scratch_shapes=[pltpu.VMEM((tm, d), BF16)],
        compiler_params=_params("arbitrary", "arbitrary"),
        name="conv_inproj",
    )(h, gain.reshape(1, d), mods, mods, w_in, w_in, w_in)


def _mixout_body(*refs, conv, seq, tm, experts, cw):
    if conv:
        gb_ref, u_ref, up_ref, un_ref, cw_ref = refs[:5]
        refs = refs[5:]
        v_scr = refs[-1]
        halo = up_ref.shape[0]
        row = lax.broadcasted_iota(I32, (tm, 1), 0)
        pos = (pl.program_id(0) * tm + row) & (seq - 1)
        for c0 in range(0, u_ref.shape[1], cw):
            cols = slice(c0, c0 + cw)
            u = u_ref[:, cols].astype(F32)
            u_dn = pltpu.roll(u, 1, 0)
            u_dn = jnp.where(row == 0, up_ref[halo - 1:halo, cols].astype(F32), u_dn)
            u_dn = jnp.where(pos == 0, 0.0, u_dn)
            u_up = pltpu.roll(u, tm - 1, 0)
            u_up = jnp.where(row == tm - 1, un_ref[0:1, cols].astype(F32), u_up)
            u_up = jnp.where(pos == seq - 1, 0.0, u_up)
            y = cw_ref[0:1, cols] * u_dn + cw_ref[1:2, cols] * u + cw_ref[2:3, cols] * u_up
            v_scr[:, cols] = (gb_ref[:, cols].astype(F32) * y).astype(BF16)
        v_ref = v_scr
    else:
        v_ref = refs[0]
        refs = refs[1:]
    w_ref, h_ref, g1_ref, gf_ref, sh_ref, sc_ref, wr_ref, hn_ref, h2_ref, aff_ref = refs[:10]
    half = h_ref.shape[1] // 2
    nck = half // 128
    th = tm // 2
    for r0 in (0, th):
        rows = slice(r0, r0 + th)
        out = jnp.dot(v_ref[rows, :], w_ref[...], preferred_element_type=F32)
        hn = h_ref[rows, :] + g1_ref[0] * out
        hn_ref[rows, :] = hn
        a = _modulate(hn, gf_ref[...], sh_ref[0], sc_ref[0])
        packed = _pack_bf16_pair(a[:, :half], a[:, half:])
        for j in range(nck):
            h2_ref[pl.ds(r0 * nck + j, th, stride=nck), :] = packed[:, j * 128:(j + 1) * 128]
        logits = jnp.dot(a.astype(BF16), wr_ref[...], preferred_element_type=F32)
        lt = logits.T[0:experts, :]
        ex = jnp.exp(lt - jnp.max(lt, axis=0, keepdims=True))
        aff_ref[:, rows] = ex / jnp.sum(ex, axis=0, keepdims=True)


def _mixout(st, v_inputs, conv_w, w_out, h, mods, gain_ffn, w_router):
    rows, d = h.shape
    experts = w_router.shape[1]
    conv = conv_w is not None
    tm, halo = st.tm(512), 16
    nck = d // 256
    assert st.seq & (st.seq - 1) == 0 and tm % 256 == 0
    row = st.ada_row(tm)
    mod = lambda which: pl.BlockSpec((1, 1, d), lambda m: (row(m) * N_ADA + which, 0, 0))
    tile = pl.BlockSpec((tm, d), lambda m: (m, 0))
    const = lambda shape: pl.BlockSpec(shape, lambda m: (0,) * len(shape))
    wr = jnp.zeros((d, 128), BF16).at[:, :experts].set(w_router.astype(BF16))
    if conv:
        gb, u = v_inputs
        per, last = tm // halo, rows // halo - 1
        v_specs = [tile, tile,
                   pl.BlockSpec((halo, d), lambda m: (jnp.maximum(m * per - 1, 0), 0)),
                   pl.BlockSpec((halo, d), lambda m: (jnp.minimum((m + 1) * per, last), 0)),
                   const((3, d))]
        v_args = (gb, u, u, u, conv_w)
        scratch = [pltpu.VMEM((tm, d), BF16)]
    else:
        v_specs, v_args, scratch = [tile], tuple(v_inputs), []
    return pl.pallas_call(
        functools.partial(_mixout_body, conv=conv, seq=st.seq, tm=tm, experts=experts, cw=512),
        grid=(rows // tm,),
        in_specs=v_specs + [
            pl.BlockSpec((d, d), lambda m: (0, 0), pipeline_mode=pl.Buffered(1)),
            tile, mod(2), const((1, d)), mod(3), mod(4), const((d, 128)),
        ],
        out_specs=[
            tile,
            pl.BlockSpec((tm * nck, 128), lambda m: (m, 0)),
            pl.BlockSpec((experts, tm), lambda m: (0, m)),
        ],
        out_shape=[
            jax.ShapeDtypeStruct((rows, d), F32),
            jax.ShapeDtypeStruct((rows * nck, 128), I32),
            jax.ShapeDtypeStruct((experts, rows), F32),
        ],
        scratch_shapes=scratch,
        compiler_params=_params("arbitrary"),
        name="mix_out_router",
    )(*v_args, w_out, h, mods, gain_ffn.reshape(1, d), mods, mods, wr)


def _qkv_body(h_ref, g_ref, sh_ref, sc_ref, w_ref, cq_ref, sq_ref, ck_ref, sk_ref, o_ref, a_scr, *, n_q, tn):
    n = pl.program_id(1)
    tm = a_scr.shape[0]

    def tile(rows, kind):
        acc = jnp.dot(a_scr[rows, :], w_ref[...], preferred_element_type=F32)
        if kind == "v":
            o_ref[rows, :] = acc.astype(BF16)
            return
        cos_ref, sin_ref = (cq_ref, sq_ref) if kind == "q" else (ck_ref, sk_ref)
        cos = cos_ref[rows, :]
        sin = sin_ref[rows, :]
        lane = lax.broadcasted_iota(I32, (1, HEAD_DIM), 1)
        first = (lane % (HEAD_DIM // 2)) < (HEAD_DIM // 4)
        for hd in range(tn // HEAD_DIM):
            lanes = slice(hd * HEAD_DIM, (hd + 1) * HEAD_DIM)
            x = acc[:, lanes]
            r = lax.rsqrt(jnp.mean(x * x, axis=-1, keepdims=True) + NORM_EPS)
            rot = jnp.where(first, pltpu.roll(x, HEAD_DIM - HEAD_DIM // 4, 1), pltpu.roll(x, HEAD_DIM // 4, 1))
            o_ref[rows, lanes] = ((x * cos + rot * sin) * r).astype(BF16)

    quarters = [slice(r0, r0 + tm // 4) for r0 in range(0, tm, tm // 4)]

    @pl.when(n == 0)
    def _():
        for rows in quarters:
            a_scr[rows, :] = _modulate(h_ref[rows, :], g_ref[...], sh_ref[0], sc_ref[0]).astype(BF16)
            tile(rows, "q" if n_q > 0 else "k")

    @pl.when((n > 0) & (n < n_q))
    def _():
        for rows in quarters:
            tile(rows, "q")

    @pl.when((n > 0) & (n == n_q))
    def _():
        for rows in quarters:
            tile(rows, "k")

    @pl.when(n > n_q)
    def _():
        tile(slice(0, tm), "v")


def _rope_tables(dims, tm, gain, scale):
    quarter = HEAD_DIM // 4
    inv_freq = ROPE_THETA ** (-jnp.arange(quarter, dtype=F32) * 2.0 / (HEAD_DIM // 2))
    t = jnp.arange(dims.seq)
    ang_row = (t // dims.grid_w).astype(F32)[:, None] * inv_freq
    ang_col = (t % dims.grid_w).astype(F32)[:, None] * inv_freq
    cos = jnp.concatenate([jnp.cos(ang_row)] * 2 + [jnp.cos(ang_col)] * 2, axis=-1)
    sin = jnp.concatenate([-jnp.sin(ang_row), jnp.sin(ang_row), -jnp.sin(ang_col), jnp.sin(ang_col)], axis=-1)
    cos = jnp.concatenate([cos, jnp.ones((tm, HEAD_DIM), F32)], axis=0)
    sin = jnp.concatenate([sin, jnp.zeros((tm, HEAD_DIM), F32)], axis=0)
    g = gain.reshape(2, 2, quarter)
    partner_gain = g[:, ::-1, :].reshape(HEAD_DIM)
    return cos * (gain * scale), sin * (partner_gain * scale)


def _qkvproj(dims, st, h, mods, gain, w_qkv, q_gain, k_gain, with_q):
    rows, d = h.shape
    tm, tn = st.tm(), dims.dkv
    n_q = dims.dq // tn if with_q else 0
    col0 = 0 if with_q else dims.dq // tn
    n_total = n_q + 2
    cos_q, sin_q = _rope_tables(dims, tm, q_gain, HEAD_DIM ** -0.5)
    cos_k, sin_k = _rope_tables(dims, tm, k_gain, 1.0)
    per = dims.seq // tm
    pos_tile = (lambda m, n: (m % per, 0)) if with_q else (lambda m, n: (per, 0))
    return pl.pallas_call(
        functools.partial(_qkv_body, n_q=n_q, tn=tn),
        grid=(rows // tm, n_total),
        in_specs=[
            pl.BlockSpec((tm, d), lambda m, n: (m, 0)),
            pl.BlockSpec((1, d), lambda m, n: (0, 0)),
            _mod_spec(st, tm, 0, d),
            _mod_spec(st, tm, 1, d),
            pl.BlockSpec((d, tn), lambda m, n: (0, col0 + n)),
            pl.BlockSpec((tm, HEAD_DIM), pos_tile),
            pl.BlockSpec((tm, HEAD_DIM), pos_tile),
            pl.BlockSpec((tm, HEAD_DIM), pos_tile),
            pl.BlockSpec((tm, HEAD_DIM), pos_tile),
        ],
        out_specs=pl.BlockSpec((tm, tn), lambda m, n: (m, n)),
        out_shape=jax.ShapeDtypeStruct((rows, n_total * tn), BF16),
        scratch_shapes=[pltpu.VMEM((tm, d), BF16)],
        compiler_params=_params("arbitrary", "arbitrary"),
        name="attn_qkv",
    )(h, gain.reshape(1, d), mods, mods, w_qkv, cos_q, sin_q, cos_k, sin_k)


def _attn_body(sink_ref, q_ref, k_ref, v_ref, kc_ref, vc_ref, o_ref, *, dims):
    group = dims.heads // dims.kv_heads
    blk = ATTN_BLOCK
    band = 3 * blk
    n_ctx = dims.ctx
    head0 = pl.program_id(1) * group
    kc = kc_ref[...]
    vc = vc_ref[...]

    n_blk = dims.seq // blk
    assert n_blk >= 3
    sink = jnp.concatenate([jnp.full((blk, 1), sink_ref[head0 + g], F32) for g in range(group)], axis=0)

    def mask_bias(first_key):
        q_rel = first_key + (lax.broadcasted_iota(I32, (group * blk, 1), 0) & (blk - 1))
        col = lax.broadcasted_iota(I32, (1, band + n_ctx), 1)
        valid = (col >= band) | (jnp.abs(q_rel - col) <= blk)
        return jnp.where(valid, 0.0, MASK_VALUE)

    def block(n, start, bias):
        q0 = n * blk if isinstance(n, int) else pl.multiple_of(n * blk, blk)
        kcat = jnp.concatenate([k_ref[pl.ds(start, band), :], kc], axis=0)
        vcat = jnp.concatenate([v_ref[pl.ds(start, band), :], vc], axis=0)
        q = jnp.concatenate([q_ref[pl.ds(q0, blk), g * HEAD_DIM:(g + 1) * HEAD_DIM] for g in range(group)], axis=0)
        s = lax.dot_general(q, kcat, (((1,), (1,)), ((), ())), preferred_element_type=F32) + bias
        m = jnp.maximum(jnp.max(s, axis=-1, keepdims=True), sink)
        p = jnp.exp(s - m)
        den = jnp.sum(p, axis=-1, keepdims=True) + jnp.exp(sink - m)
        o = jnp.dot(p.astype(BF16), vcat, preferred_element_type=F32) / den
        for g in range(group):
            o_ref[pl.ds(q0, blk), g * HEAD_DIM:(g + 1) * HEAD_DIM] = o[g * blk:(g + 1) * blk].astype(BF16)

    block(0, 0, mask_bias(0))
    mid_bias = mask_bias(blk)

    def middle(n, carry):
        block(n, pl.multiple_of((n - 1) * blk, blk), mid_bias)
        return carry

    lax.fori_loop(1, n_blk - 1, middle, 0, unroll=2)
    block(n_blk - 1, dims.seq - band, mask_bias(2 * blk))


def _attention(dims, qkv, kv_ctx, sink):
    group = dims.heads // dims.kv_heads
    gw = group * HEAD_DIM
    k_col = dims.dq // HEAD_DIM
    v_col = (dims.dq + dims.dkv) // HEAD_DIM
    return pl.pallas_call(
        functools.partial(_attn_body, dims=dims),
        grid=(dims.batch, dims.kv_heads),
        in_specs=[
            pl.BlockSpec(memory_space=pltpu.SMEM),
            pl.BlockSpec((dims.seq, gw), lambda b, h: (b, h)),
            pl.BlockSpec((dims.seq, HEAD_DIM), lambda b, h: (b, k_col + h)),
            pl.BlockSpec((dims.seq, HEAD_DIM), lambda b, h: (b, v_col + h)),
            pl.BlockSpec((dims.ctx, HEAD_DIM), lambda b, h: (b, h)),
            pl.BlockSpec((dims.ctx, HEAD_DIM), lambda b, h: (b, dims.kv_heads + h)),
        ],
        out_specs=pl.BlockSpec((dims.seq, gw), lambda b, h: (b, h)),
        out_shape=jax.ShapeDtypeStruct((dims.batch * dims.seq, dims.dq), BF16),
        compiler_params=_params("arbitrary", "arbitrary"),
        name="attn_core",
    )(sink, qkv, qkv, qkv, kv_ctx, kv_ctx)


def _lane_prefix(x, tri):
    n = x.shape[1]
    off = jnp.zeros((x.shape[0], 1), F32)
    chunks = []
    for ch in range(n // 128):
        xc = x[:, ch * 128:(ch + 1) * 128]
        incl = jnp.dot(xc.astype(BF16), tri, preferred_element_type=F32)
        chunks.append(incl - xc + off)
        off = off + incl[:, 127:128]
    return chunks[0] if len(chunks) == 1 else jnp.concatenate(chunks, axis=1)


def _topk_body(aff_ref, idx_ref, dst_ref, gate_ref, off_ref, cnt_ref, w_scr, a_scr, q_scr, clo_scr, chi_scr, *,
               n_tok, cap, pair0, experts):
    grp = pl.program_id(0)
    a = aff_ref[...]
    bits = pltpu.bitcast(a, I32)
    lane = lax.broadcasted_iota(I32, (experts, n_tok), 1)

    def count(mask):
        return jnp.sum(mask.astype(I32), axis=1, keepdims=True)

    def thr_step(i, thr):
        hi = jnp.left_shift(jnp.int32(1), 29 - 2 * i)
        lo = jnp.left_shift(jnp.int32(1), 28 - 2 * i)
        best = thr
        for cand in (thr | lo, thr | hi, thr | hi | lo):
            best = jnp.where(count(bits >= cand) >= cap, cand, best)
        return best

    first = jnp.int32(1 << 30)
    thr = jnp.where(count(bits >= first) >= cap, first, jnp.zeros((experts, 1), I32))
    thr = lax.fori_loop(0, 15, thr_step, thr)
    above = bits > thr
    tie = bits == thr
    need = cap - count(above)

    top_bit = n_tok.bit_length() - 2

    def tie_try(lim, cands):
        for cand in cands:
            lim = jnp.where(count(tie & (lane < cand)) < need, cand, lim)
        return lim

    def tie_step(i, lim):
        hi = jnp.left_shift(jnp.int32(1), top_bit - odd - 2 * i)
        lo = jnp.left_shift(jnp.int32(1), top_bit - odd - 2 * i - 1)
        return tie_try(lim, (lim | lo, lim | hi, lim | hi | lo))

    odd = (top_bit + 1) % 2
    lim = jnp.zeros((experts, 1), I32)
    if odd:
        lim = tie_try(lim, (lim | (1 << top_bit),))
    lim = lax.fori_loop(0, (top_bit + 1) // 2, tie_step, lim)
    sel = above | (tie & (lane <= lim))
    sel_f = sel.astype(F32)
    sel_i = sel.astype(I32)

    r = lax.broadcasted_iota(I32, (128, 128), 0)
    c = lax.broadcasted_iota(I32, (128, 128), 1)
    tri = (r <= c).astype(F32).astype(BF16)

    level = jnp.zeros((1, n_tok), I32)
    levels = []
    for e in range(experts):
        levels.append(level)
        level = level + sel_i[e:e + 1, :]
    cnt = level
    off = _lane_prefix(cnt.astype(F32), tri).astype(I32) + (pair0 + grp * (experts * cap))
    off_ref[...] = off
    cnt_ref[...] = cnt
    pair_row = jnp.concatenate(levels, axis=0) + off

    n_ch = n_tok // 128
    pad = w_scr.shape[0] // experts
    lane_e = lax.broadcasted_iota(I32, (experts, 128), 1)
    w_scr[...] = jnp.zeros_like(w_scr)
    a_scr[...] = jnp.zeros_like(a_scr)
    q_scr[...] = jnp.zeros_like(q_scr)
    first = jnp.zeros((experts, 1), F32)
    c_lo = jnp.full((experts, 128), cap, I32)
    c_hi = jnp.full((experts, 128), cap, I32)
    for ch in range(n_ch):
        lanes = slice(ch * 128, (ch + 1) * 128)
        incl = jnp.dot(sel_f[:, lanes].astype(BF16), tri, preferred_element_type=F32)
        w_scr[pl.ds(ch, experts, stride=pad), :] = incl
        a_scr[pl.ds(ch, experts, stride=pad), :] = a[:, lanes]
        q_scr[pl.ds(ch, experts, stride=pad), :] = pair_row[:, lanes]
        nxt = first + incl[:, 127:128]
        c_lo = jnp.where(lane_e == ch, first.astype(I32), c_lo)
        c_hi = jnp.where(lane_e == ch, nxt.astype(I32), c_hi)
        first = nxt
    clo_scr[...] = c_lo
    chi_scr[...] = c_hi

    idx_ref[0] = jnp.zeros((cap, 128), I32)
    dst_ref[0] = jnp.zeros((cap, 128), I32)
    gate_ref[0] = jnp.zeros((cap, 128), F32)
    slot = lax.broadcasted_iota(I32, (cap, 1), 0)
    lane = lax.broadcasted_iota(I32, (1, 128), 1)
    zpad = jnp.zeros((128 - pad, 128), F32)

    def rhs(m):
        return jnp.concatenate([m, zpad], axis=0).astype(BF16)

    def per_expert(e, carry):
        base = pl.multiple_of(e * pad, pad)
        wm = w_scr[pl.ds(base, pad), :]
        am = a_scr[pl.ds(base, pad), :]
        qm = q_scr[pl.ds(base, pad), :]
        lo = clo_scr[pl.ds(e, 1), :]
        hi = chi_scr[pl.ds(e, 1), :]
        in_chunk = (lo <= slot) & (slot < hi)
        onehot = in_chunk.astype(F32).astype(BF16)

        def rows_of(m):
            return jnp.dot(onehot, rhs(m), preferred_element_type=F32)

        a1 = am.astype(BF16).astype(F32)
        a2 = (am - a1).astype(BF16).astype(F32)
        a3 = am - a1 - a2
        a_rows = rows_of(a1) + rows_of(a2) + rows_of(a3)
        q_rows = (rows_of(lax.shift_right_logical(qm, 8).astype(F32)) * 256.0
                  + rows_of((qm & 255).astype(F32)))
        w_rows = rows_of(wm)
        chunk_first = jnp.sum(jnp.where(in_chunk, lo, 0), axis=1, keepdims=True)
        chunk = jnp.sum(jnp.where(in_chunk, lane, 0), axis=1, keepdims=True)
        rank = (slot - chunk_first).astype(F32)
        pos = jnp.sum((w_rows <= rank).astype(I32), axis=1, keepdims=True)
        here = lane == pos
        gate = jnp.sum(jnp.where(here, a_rows, 0.0), axis=1, keepdims=True)
        dst = jnp.sum(jnp.where(here, q_rows, 0.0), axis=1, keepdims=True).astype(I32)
        token = chunk * 128 + pos + grp * n_tok
        mine = lane == e
        idx_ref[0] = jnp.where(mine, token, idx_ref[0])
        dst_ref[0] = jnp.where(mine, dst, dst_ref[0])
        gate_ref[0] = jnp.where(mine, gate, gate_ref[0])
        return carry

    lax.fori_loop(0, experts, per_expert, 0)


def _topk(st, aff, cap, pair0):
    e = aff.shape[0]
    n_tok, n_groups = st.seq, st.batch
    pad = max(8, n_tok // 128)
    assert e <= 128 and pad <= 128 and n_groups * e * cap < 2 ** 16 * 256
    tile_spec = pl.BlockSpec((1, cap, 128), lambda g: (g, 0, 0))
    return pl.pallas_call(
        functools.partial(_topk_body, n_tok=n_tok, cap=cap, pair0=pair0, experts=e),
        grid=(n_groups,),
        in_specs=[pl.BlockSpec((e, n_tok), lambda g: (0, g))],
        out_specs=[tile_spec, tile_spec, tile_spec,
                   pl.BlockSpec((1, n_tok), lambda g: (0, g)),
                   pl.BlockSpec((1, n_tok), lambda g: (0, g))],
        out_shape=[jax.ShapeDtypeStruct((n_groups, cap, 128), I32),
                   jax.ShapeDtypeStruct((n_groups, cap, 128), I32),
                   jax.ShapeDtypeStruct((n_groups, cap, 128), F32),
                   jax.ShapeDtypeStruct((1, n_groups * n_tok), I32),
                   jax.ShapeDtypeStruct((1, n_groups * n_tok), I32)],
        scratch_shapes=[pltpu.VMEM((e * pad, 128), F32), pltpu.VMEM((e * pad, 128), F32),
                        pltpu.VMEM((e * pad, 128), I32), pltpu.VMEM((e, 128), I32), pltpu.VMEM((e, 128), I32)],
        compiler_params=_params("arbitrary"),
        name="moe_topk",
    )(aff)


def _expert_body(*refs, n_src, src_slots, slots, rc, n_f, n_n, tf, n_e):
    idx_ref, idx_nxt_ref, dst_ref, dst_prv_ref = refs[0:4]
    srcs = refs[4:4 + n_src]
    gate_ref, wg_ref, wu_ref, wd_ref, pairs_hbm, x_scr, y_scr, mid_scr, gcol_scr, gsem, ssem = refs[4 + n_src:]
    e = pl.program_id(0)
    s = pl.program_id(1)
    nck = x_scr.shape[0] // slots
    half = nck * 128
    tnc = nck // n_n

    def tok(ref, r):
        return ref.at[pl.ds(pl.multiple_of(r * nck, nck), nck), :]

    def row_loop(lo, n, start_row):
        def eight(i, carry):
            for k in range(8):
                start_row(lo + i * 8 + k)
            return carry
        lax.fori_loop(0, n // 8, eight, 0)

    def gather_row(ref, r, slot):
        src = next(src for (s0, s1), src in zip(src_slots, srcs) if s0 <= slot < s1)
        pltpu.make_async_copy(tok(src, ref[0, 0, r]), tok(x_scr, r), gsem).start()

    def scatter_row(ref, r):
        pltpu.make_async_copy(tok(y_scr, r), tok(pairs_hbm, ref[0, 0, r]), ssem).start()

    n_chunks = slots // rc
    dma_chunks = max(n_chunks - 1, 1)

    def share(total, ci):
        base, extra = divmod(total, dma_chunks)
        if ci >= dma_chunks:
            return total, 0
        return ci * base + min(ci, extra), base + (1 if ci < extra else 0)

    def wait_gather():
        pltpu.make_async_copy(srcs[0].at[pl.ds(0, slots * nck), :], x_scr, gsem).wait()

    def wait_scatter():
        pltpu.make_async_copy(y_scr, pairs_hbm.at[pl.ds(0, slots * nck), :], ssem).wait()

    def x_rows(r0):
        parts = [_unpack_bf16_pair(x_scr[pl.ds(r0 * nck + c, rc, stride=nck), :]) for c in range(nck)]
        return (jnp.concatenate([p[0] for p in parts], axis=1), jnp.concatenate([p[1] for p in parts], axis=1))

    @pl.when((e == 0) & (s == 0))
    def _():
        y_scr[...] = jnp.zeros_like(y_scr)
        for (s0, s1), src in zip(src_slots, srcs):
            row_loop(s0, s1 - s0, lambda r, src=src: pltpu.make_async_copy(
                tok(src, idx_ref[0, 0, r]), tok(x_scr, r), gsem).start())

    @pl.when(s == 0)
    def _():
        wait_gather()
        eye = lax.broadcasted_iota(I32, (128, 128), 0) == lax.broadcasted_iota(I32, (128, 128), 1)
        for c in range(slots // 128):
            row = gate_ref[0, c:c + 1, :]
            gcol_scr[c * 128:(c + 1) * 128, :] = jnp.sum(jnp.where(eye, row, 0.0), axis=1, keepdims=True)

    @pl.when(s < n_f)
    def _():
        wg = wg_ref[0, 0].astype(BF16)
        wu = wu_ref[0, 0].astype(BF16)
        for r0 in range(0, slots, rc):
            lo, hi = x_rows(r0)
            a = (jnp.dot(lo, wg[:half], preferred_element_type=F32)
                 + jnp.dot(hi, wg[half:], preferred_element_type=F32))
            u = (jnp.dot(lo, wu[:half], preferred_element_type=F32)
                 + jnp.dot(hi, wu[half:], preferred_element_type=F32))
            mid_scr[s, r0:r0 + rc, :] = (a * jax.nn.sigmoid(a) * u).astype(BF16)
            off, cnt = share(slots // n_f, r0 // rc)
            for k in range(cnt):
                scatter_row(dst_prv_ref, s * (slots // n_f) + off + k)

    @pl.when(s == n_f)
    def _():
        wait_scatter()

    for j in range(n_n):
        @pl.when(s == n_f + j)
        def _():
            wd = wd_ref[0, 0].astype(BF16)
            for r0 in range(0, slots, rc):
                y = jnp.dot(mid_scr[0, r0:r0 + rc, :], wd[0:tf], preferred_element_type=F32)
                for f in range(1, n_f):
                    y += jnp.dot(mid_scr[f, r0:r0 + rc, :], wd[f * tf:(f + 1) * tf], preferred_element_type=F32)
                y = y * gcol_scr[r0:r0 + rc, :]
                tnw = tnc * 128
                packed = _pack_bf16_pair(y[:, :tnw], y[:, tnw:])
                for c in range(tnc):
                    y_scr[pl.ds(r0 * nck + j * tnc + c, rc, stride=nck), :] = packed[:, c * 128:(c + 1) * 128]
                g_steps = max(n_n - 1, 1)
                step_rows = -(-slots // g_steps)
                lo = min(j * step_rows, slots)
                off, cnt = share(min(step_rows, slots - lo), r0 // rc)
                for k in range(cnt):
                    slot = lo + off + k
                    gather_row(idx_nxt_ref, slot, slot)

    @pl.when((e == n_e - 1) & (s == n_f + n_n - 1))
    def _():
        wait_gather()
        row_loop(0, slots, lambda r: scatter_row(dst_ref, r))
        wait_scatter()


def _experts(idx, dst, gate, sources, src_slots, w_gate, w_up, w_down, layer, n_pairs):
    n_e, slots = idx.shape
    d, fdim = w_gate.shape[2], w_gate.shape[3]
    tf, tn = 256, EXPERT_DOWN_TILE
    n_f, n_n = fdim // tf, d // tn
    rc = slots // 4
    nck = d // 256
    assert slots % 128 == 0 and rc % 16 == 0 and nck % n_n == 0 and slots % n_f == 0 and slots % n_n == 0
    assert all(s0 % 8 == 0 and s1 % 8 == 0 for s0, s1 in src_slots)
    idx3 = idx.reshape(n_e, 1, slots)
    dst3 = dst.reshape(n_e, 1, slots)
    smem = lambda f: pl.BlockSpec((1, 1, slots), f, memory_space=pltpu.SMEM)
    up_chunk = lambda i, s: (layer, i, 0, jnp.minimum(s, n_f - 1))
    return pl.pallas_call(
        functools.partial(_expert_body, n_src=len(sources), src_slots=src_slots, slots=slots, rc=rc,
                          n_f=n_f, n_n=n_n, tf=tf, n_e=n_e),
        grid=(n_e, n_f + n_n),
        in_specs=[
            smem(lambda i, s: (i, 0, 0)),
            smem(lambda i, s: (jnp.minimum(i + 1, n_e - 1), 0, 0)),
            smem(lambda i, s: (i, 0, 0)),
            smem(lambda i, s: (jnp.maximum(i - 1, 0), 0, 0)),
        ] + [pl.BlockSpec(memory_space=pl.ANY)] * len(sources) + [
            pl.BlockSpec((1, slots // 128, 128), lambda i, s: (i, 0, 0)),
            pl.BlockSpec((1, 1, d, tf), up_chunk),
            pl.BlockSpec((1, 1, d, tf), up_chunk),
            pl.BlockSpec((1, 1, fdim, tn), lambda i, s: (layer, i, 0, jnp.maximum(s - n_f, 0))),
        ],
        out_specs=pl.BlockSpec(memory_space=pl.ANY),
        out_shape=jax.ShapeDtypeStruct((n_pairs * nck, 128), I32),
        scratch_shapes=[
            pltpu.VMEM((slots * nck, 128), I32),
            pltpu.VMEM((slots * nck, 128), I32),
            pltpu.VMEM((n_f, slots, tf), BF16),
            pltpu.VMEM((slots, 1), F32),
            pltpu.SemaphoreType.DMA(()),
            pltpu.SemaphoreType.DMA(()),
        ],
        compiler_params=_params("arbitrary", "arbitrary"),
        name="moe_experts",
    )(idx3, idx3, dst3, dst3, *sources, gate.reshape(n_e, slots // 128, 128), w_gate, w_up, w_down)


def _combine_body(lo_ref, hi_ref, off_ref, cnt_ref, p_ref, h_ref, g2_ref, o_ref, acc_lo, acc_hi, *,
                  tt, nck, tnw, pair0, pairs_per_group):
    base = pair0 + pl.program_id(0) * pairs_per_group
    i = pl.program_id(1)
    lane = lax.broadcasted_iota(I32, (1, PAIR_BLOCK), 1)
    eye = lax.broadcasted_iota(I32, (128, 128), 0) == lax.broadcasted_iota(I32, (128, 128), 1)

    def column(row):
        cols = [jnp.sum(jnp.where(eye, row[:, c0:c0 + 128], 0), axis=1, keepdims=True) for c0 in range(0, tt, 128)]
        return cols[0] if len(cols) == 1 else jnp.concatenate(cols, axis=0)

    first = column(off_ref[...]) - base
    last = first + column(cnt_ref[...])
    shift = PAIR_BLOCK.bit_length() - 1
    k0 = lax.shift_right_logical(lo_ref[0, 0, i] - base, shift)
    k1 = lax.shift_right_logical(hi_ref[0, 0, i] - base + (PAIR_BLOCK - 1), shift)
    acc_lo[...] = jnp.zeros_like(acc_lo)
    acc_hi[...] = jnp.zeros_like(acc_hi)

    def step(k, carry):
        p0 = pl.multiple_of(k * PAIR_BLOCK, PAIR_BLOCK)
        parts = [_unpack_bf16_pair(p_ref[pl.ds(p0 * nck + c, PAIR_BLOCK, stride=nck), :]) for c in range(nck)]
        lo = jnp.concatenate([p[0] for p in parts], axis=1)
        hi = jnp.concatenate([p[1] for p in parts], axis=1)
        pr = p0 + lane
        seg = ((pr >= first) & (pr < last)).astype(F32).astype(BF16)
        acc_lo[...] += jnp.dot(seg, lo, preferred_element_type=F32)
        acc_hi[...] += jnp.dot(seg, hi, preferred_element_type=F32)
        return carry

    lax.fori_loop(k0, k1, step, 0)
    for j in range(nck * 128 // tnw):
        words = slice(j * tnw, (j + 1) * tnw)
        c_lo = slice(2 * j * tnw, (2 * j + 1) * tnw)
        c_hi = slice((2 * j + 1) * tnw, (2 * j + 2) * tnw)
        o_ref[:, c_lo] = h_ref[:, c_lo] + g2_ref[0, :, c_lo] * acc_lo[:, words]
        o_ref[:, c_hi] = h_ref[:, c_hi] + g2_ref[0, :, c_hi] * acc_hi[:, words]


def _combine(st, off, cnt, pairs, h, mods, pair0, pairs_per_group):
    rows, d = h.shape
    n_tok = st.seq
    tt = min(256, n_tok)
    n_tiles = n_tok // tt
    nck = d // 256
    assert pairs_per_group % PAIR_BLOCK == 0 and pair0 % pairs_per_group == 0
    off_t = off.reshape(st.batch, 1, n_tiles, tt)
    cnt_t = cnt.reshape(st.batch, 1, n_tiles, tt)
    tile_lo = off_t[..., 0]
    tile_hi = off_t[..., tt - 1] + cnt_t[..., tt - 1]
    row = st.ada_row(n_tok)
    smem = pl.BlockSpec((1, 1, n_tiles), lambda b, i: (b, 0, 0), memory_space=pltpu.SMEM)
    tile = lambda b, i: (b * n_tiles + i, 0)
    return pl.pallas_call(
        functools.partial(_combine_body, tt=tt, nck=nck, tnw=EXPERT_DOWN_TILE // 2, pair0=pair0,
                          pairs_per_group=pairs_per_group),
        grid=(st.batch, n_tiles),
        in_specs=[
            smem, smem,
            pl.BlockSpec((1, tt), lambda b, i: (0, b * n_tiles + i)),
            pl.BlockSpec((1, tt), lambda b, i: (0, b * n_tiles + i)),
            pl.BlockSpec((pairs_per_group * nck, 128), lambda b, i: (pair0 // pairs_per_group + b, 0)),
            pl.BlockSpec((tt, d), tile),
            pl.BlockSpec((1, 1, d), lambda b, i: (row(b) * N_ADA + 5, 0, 0)),
        ],
        out_specs=pl.BlockSpec((tt, d), tile),
        out_shape=jax.ShapeDtypeStruct((rows, d), F32),
        scratch_shapes=[pltpu.VMEM((tt, d // 2), F32)] * 2,
        compiler_params=_params("arbitrary", "arbitrary"),
        name="moe_combine",
    )(tile_lo, tile_hi, off, cnt, pairs, h, mods)


def _moe(streams, mixed, mods, w_gate, w_up, w_down, layer):
    n_e = mixed[0][2].shape[0]
    routed = []
    pair0 = 0
    slot0 = 0
    src_slots = []
    for st, (h, h2, aff) in zip(streams, mixed):
        cap = CAP_FACTOR * st.seq // n_e
        idx, dst, gate, off, cnt = _topk(st, aff, cap, pair0)
        routed.append((h2, idx, dst, gate, off, cnt, pair0, n_e * cap))
        src_slots.append((slot0, slot0 + st.batch * cap))
        pair0 += st.batch * n_e * cap
        slot0 += st.batch * cap
    flat = lambda t: t[:, :, :n_e].transpose(2, 0, 1).reshape(n_e, -1)
    idx = jnp.concatenate([flat(r[1]) for r in routed], axis=1)
    dst = jnp.concatenate([flat(r[2]) for r in routed], axis=1)
    gate = jnp.concatenate([flat(r[3]) for r in routed], axis=1)
    pairs = _experts(idx, dst, gate, [r[0] for r in routed], tuple(src_slots), w_gate, w_up, w_down, layer, pair0)
    return [_combine(st, r[4], r[5], pairs, m[0], mods, r[6], r[7]) for st, m, r in zip(streams, mixed, routed)]


def _forward(dims, x, c, ctx, c_ctx, ada_w, ada_b, norm_mix_g, norm_ffn_g, conv_w_in, conv_w, conv_w_out,
             attn_w_qkv, attn_q_gain, attn_k_gain, attn_sink, attn_w_o, router_w,
             expert_w_gate, expert_w_up, expert_w_down):
    d = dims.d
    lat = Stream(dims.batch, dims.seq)
    con = Stream(dims.batch, dims.ctx, shared_row=dims.batch)
    cvec = jnp.zeros((ADA_ROWS, d), F32).at[:dims.batch].set(c).at[dims.batch].set(c_ctx)
    mods = _adaln(cvec, ada_w, ada_b)
    mods = mods.reshape(mods.shape[0], ADA_ROWS * N_ADA, 1, d)
    hs = [x.reshape(lat.rows, d), ctx.reshape(con.rows, d)]

    w_in, w_out = conv_w_in[0].astype(BF16), conv_w_out[0].astype(BF16)
    mixed = []
    for st, h in zip((lat, con), hs):
        gb_u = _inproj(st, h, mods[0], norm_mix_g[0], w_in)
        mixed.append(_mixout(st, gb_u, conv_w[0], w_out, h, mods[0], norm_ffn_g[0], router_w[0]))
    h_lat, h_ctx = _moe((lat, con), mixed, mods[0], expert_w_gate, expert_w_up, expert_w_down, 0)

    w_qkv, w_o = attn_w_qkv[0].astype(BF16), attn_w_o[0].astype(BF16)
    qkv = _qkvproj(dims, lat, h_lat, mods[1], norm_mix_g[1], w_qkv, attn_q_gain[0], attn_k_gain[0], True)
    kv_ctx = _qkvproj(dims, con, h_ctx, mods[1], norm_mix_g[1], w_qkv, attn_q_gain[0], attn_k_gain[0], False)
    o = _attention(dims, qkv, kv_ctx, attn_sink[0])
    mixed = [_mixout(lat, (o,), None, w_o, h_lat, mods[1], norm_ffn_g[1], router_w[1])]
    (h_lat,) = _moe((lat,), mixed, mods[1], expert_w_gate, expert_w_up, expert_w_down, 1)
    return h_lat.reshape(dims.batch, dims.seq, d)


def kernel(x, c, ctx, c_ctx, ada_w, ada_b, norm_mix_g, norm_ffn_g, conv_w_in, conv_w, conv_w_out, attn_w_qkv, attn_q_gain, attn_k_gain, attn_sink, attn_w_o, router_w, expert_w_gate, expert_w_up, expert_w_down):
    batch, seq, d = x.shape
    dims = Dims(d=d, batch=batch, seq=seq, grid_w=GRID_W, ctx=ctx.shape[1], heads=d // HEAD_DIM,
                kv_heads=(attn_w_qkv.shape[2] // HEAD_DIM - d // HEAD_DIM) // 2,
                experts=router_w.shape[2], d_expert=expert_w_gate.shape[3])
    return _forward(dims, x, c, ctx, c_ctx, ada_w, ada_b, norm_mix_g, norm_ffn_g, conv_w_in, conv_w, conv_w_out,
                    attn_w_qkv, attn_q_gain, attn_k_gain, attn_sink, attn_w_o, router_w,
                    expert_w_gate, expert_w_up, expert_w_down)
```

```python
import dataclasses
import functools

import jax
import jax.numpy as jnp
from jax import lax
from jax.experimental import pallas as pl
from jax.experimental.pallas import tpu as pltpu

F32 = jnp.float32
BF16 = jnp.bfloat16
I32 = jnp.int32

NORM_EPS = 1e-6
MASK_VALUE = -1e30
ROPE_THETA = 10000.0
GRID_W = 64
CAP_FACTOR = 2
HEAD_DIM = 128
ATTN_BLOCK = 128
N_ADA = 6
ADA_ROWS = 16
HI_MASK = -65536
PAIR_BLOCK = 256
EXPERT_DOWN_TILE = 512
VMEM_LIMIT = 58 * 1024 * 1024


@dataclasses.dataclass(frozen=True)
class Dims:
    d: int
    batch: int
    seq: int
    grid_w: int
    ctx: int
    heads: int
    kv_heads: int
    experts: int
    d_expert: int

    @property
    def dq(self):
        return self.heads * HEAD_DIM

    @property
    def dkv(self):
        return self.kv_heads * HEAD_DIM


@dataclasses.dataclass(frozen=True)
class Stream:
    batch: int
    seq: int
    shared_row: int = -1

    @property
    def rows(self):
        return self.batch * self.seq

    def tm(self, cap=1024):
        unit = self.rows if self.shared_row >= 0 else self.seq
        t = cap
        while unit % t:
            t //= 2
        return t

    def ada_row(self, tm):
        if self.shared_row >= 0:
            return lambda m: self.shared_row
        per = self.seq // tm
        assert per >= 1
        return lambda m: m // per


def _params(*sem):
    return pltpu.CompilerParams(dimension_semantics=sem, vmem_limit_bytes=VMEM_LIMIT)


def _modulate(h, gain, shift, scale):
    ms = jnp.mean(h * h, axis=-1, keepdims=True)
    xn = h * lax.rsqrt(ms + NORM_EPS)
    return (xn * gain) * (1.0 + scale) + shift


def _mod_spec(st, tm, which, d, width=None):
    row = st.ada_row(tm)
    if width is None:
        return pl.BlockSpec((1, 1, d), lambda m, n: (row(m) * N_ADA + which, 0, 0))
    return pl.BlockSpec((1, 1, width), lambda m, n: (row(m) * N_ADA + which, 0, n))


def _pack_bf16_pair(lo, hi):
    lo_bits = pltpu.bitcast(lo.astype(BF16).astype(F32), I32)
    hi_bits = pltpu.bitcast(hi.astype(BF16).astype(F32), I32)
    return (hi_bits & HI_MASK) | lax.shift_right_logical(lo_bits, 16)


def _unpack_bf16_pair(w):
    lo = pltpu.bitcast(lax.shift_left(w, 16), F32).astype(BF16)
    hi = pltpu.bitcast(w & HI_MASK, F32).astype(BF16)
    return lo, hi


def _adaln_body(c_ref, w_ref, b_ref, o_ref):
    c = c_ref[...]
    s = c * jax.nn.sigmoid(c)
    o_ref[0] = jnp.dot(s.astype(BF16), w_ref[0].astype(BF16), preferred_element_type=F32) + b_ref[0]


def _adaln(cvec, ada_w, ada_b):
    depth, d, n = ada_w.shape
    tn = min(1024, n)
    return pl.pallas_call(
        _adaln_body,
        grid=(depth, n // tn),
        in_specs=[
            pl.BlockSpec((ADA_ROWS, d), lambda l, j: (0, 0)),
            pl.BlockSpec((1, d, tn), lambda l, j: (l, 0, j)),
            pl.BlockSpec((1, 1, tn), lambda l, j: (l, 0, j)),
        ],
        out_specs=pl.BlockSpec((1, ADA_ROWS, tn), lambda l, j: (l, 0, j)),
        out_shape=jax.ShapeDtypeStruct((depth, ADA_ROWS, n), F32),
        compiler_params=_params("arbitrary", "arbitrary"),
        name="adaln",
    )(cvec, ada_w, ada_b.reshape(depth, 1, n))


def _inproj_body(h_ref, g_ref, sh_ref, sc_ref, wb_ref, wc_ref, wx_ref, gb_ref, u_ref, a_scr):
    def project(rows):
        a = a_scr[rows, :]
        gb = jnp.dot(a, wb_ref[...], preferred_element_type=F32)
        gc = jnp.dot(a, wc_ref[...], preferred_element_type=F32)
        xv = jnp.dot(a, wx_ref[...], preferred_element_type=F32)
        gb_ref[rows, :] = gb.astype(BF16)
        u_ref[rows, :] = (gc * xv).astype(BF16)

    tm = a_scr.shape[0]

    @pl.when(pl.program_id(1) == 0)
    def _():
        for r0 in range(0, tm, tm // 4):
            rows = slice(r0, r0 + tm // 4)
            a_scr[rows, :] = _modulate(h_ref[rows, :], g_ref[...], sh_ref[0], sc_ref[0]).astype(BF16)
            project(rows)

    @pl.when(pl.program_id(1) > 0)
    def _():
        project(slice(0, tm))


def _inproj(st, h, mods, gain, w_in):
    rows, d = h.shape
    tm, tn = st.tm(), 512
    nt = d // tn
    return pl.pallas_call(
        _inproj_body,
        grid=(rows // tm, nt),
        in_specs=[
            pl.BlockSpec((tm, d), lambda m, n: (m, 0)),
            pl.BlockSpec((1, d), lambda m, n: (0, 0)),
            _mod_spec(st, tm, 0, d),
            _mod_spec(st, tm, 1, d),
            pl.BlockSpec((d, tn), lambda m, n: (0, n)),
            pl.BlockSpec((d, tn), lambda m, n: (0, nt + n)),
            pl.BlockSpec((d, tn), lambda m, n: (0, 2 * nt + n)),
        ],
        out_specs=[pl.BlockSpec((tm, tn), lambda m, n: (m, n))] * 2,
        out_shape=[jax.ShapeDtypeStruct((rows, d), BF16)] * 2,
        scratch_shapes=[pltpu.VMEM((tm, d), BF16)],
        compiler_params=_params("arbitrary", "arbitrary"),
        name="conv_inproj",
    )(h, gain.reshape(1, d), mods, mods, w_in, w_in, w_in)


def _mixout_body(*refs, conv, seq, tm, experts, cw):
    if conv:
        gb_ref, u_ref, up_ref, un_ref, cw_ref = refs[:5]
        refs = refs[5:]
        v_scr = refs[-1]
        halo = up_ref.shape[0]
        row = lax.broadcasted_iota(I32, (tm, 1), 0)
        pos = (pl.program_id(0) * tm + row) & (seq - 1)
        for c0 in range(0, u_ref.shape[1], cw):
            cols = slice(c0, c0 + cw)
            u = u_ref[:, cols].astype(F32)
            u_dn = pltpu.roll(u, 1, 0)
            u_dn = jnp.where(row == 0, up_ref[halo - 1:halo, cols].astype(F32), u_dn)
            u_dn = jnp.where(pos == 0, 0.0, u_dn)
            u_up = pltpu.roll(u, tm - 1, 0)
            u_up = jnp.where(row == tm - 1, un_ref[0:1, cols].astype(F32), u_up)
            u_up = jnp.where(pos == seq - 1, 0.0, u_up)
            y = cw_ref[0:1, cols] * u_dn + cw_ref[1:2, cols] * u + cw_ref[2:3, cols] * u_up
            v_scr[:, cols] = (gb_ref[:, cols].astype(F32) * y).astype(BF16)
        v_ref = v_scr
    else:
        v_ref = refs[0]
        refs = refs[1:]
    w_ref, h_ref, g1_ref, gf_ref, sh_ref, sc_ref, wr_ref, hn_ref, h2_ref, aff_ref = refs[:10]
    half = h_ref.shape[1] // 2
    nck = half // 128
    th = tm // 2
    for r0 in (0, th):
        rows = slice(r0, r0 + th)
        out = jnp.dot(v_ref[rows, :], w_ref[...], preferred_element_type=F32)
        hn = h_ref[rows, :] + g1_ref[0] * out
        hn_ref[rows, :] = hn
        a = _modulate(hn, gf_ref[...], sh_ref[0], sc_ref[0])
        packed = _pack_bf16_pair(a[:, :half], a[:, half:])
        for j in range(nck):
            h2_ref[pl.ds(r0 * nck + j, th, stride=nck), :] = packed[:, j * 128:(j + 1) * 128]
        logits = jnp.dot(a.astype(BF16), wr_ref[...], preferred_element_type=F32)
        lt = logits.T[0:experts, :]
        ex = jnp.exp(lt - jnp.max(lt, axis=0, keepdims=True))
        aff_ref[:, rows] = ex / jnp.sum(ex, axis=0, keepdims=True)


def _mixout(st, v_inputs, conv_w, w_out, h, mods, gain_ffn, w_router):
    rows, d = h.shape
    experts = w_router.shape[1]
    conv = conv_w is not None
    tm, halo = st.tm(512), 16
    nck = d // 256
    assert st.seq & (st.seq - 1) == 0 and tm % 256 == 0
    row = st.ada_row(tm)
    mod = lambda which: pl.BlockSpec((1, 1, d), lambda m: (row(m) * N_ADA + which, 0, 0))
    tile = pl.BlockSpec((tm, d), lambda m: (m, 0))
    const = lambda shape: pl.BlockSpec(shape, lambda m: (0,) * len(shape))
    wr = jnp.zeros((d, 128), BF16).at[:, :experts].set(w_router.astype(BF16))
    if conv:
        gb, u = v_inputs
        per, last = tm // halo, rows // halo - 1
        v_specs = [tile, tile,
                   pl.BlockSpec((halo, d), lambda m: (jnp.maximum(m * per - 1, 0), 0)),
                   pl.BlockSpec((halo, d), lambda m: (jnp.minimum((m + 1) * per, last), 0)),
                   const((3, d))]
        v_args = (gb, u, u, u, conv_w)
        scratch = [pltpu.VMEM((tm, d), BF16)]
    else:
        v_specs, v_args, scratch = [tile], tuple(v_inputs), []
    return pl.pallas_call(
        functools.partial(_mixout_body, conv=conv, seq=st.seq, tm=tm, experts=experts, cw=512),
        grid=(rows // tm,),
        in_specs=v_specs + [
            pl.BlockSpec((d, d), lambda m: (0, 0), pipeline_mode=pl.Buffered(1)),
            tile, mod(2), const((1, d)), mod(3), mod(4), const((d, 128)),
        ],
        out_specs=[
            tile,
            pl.BlockSpec((tm * nck, 128), lambda m: (m, 0)),
            pl.BlockSpec((experts, tm), lambda m: (0, m)),
        ],
        out_shape=[
            jax.ShapeDtypeStruct((rows, d), F32),
            jax.ShapeDtypeStruct((rows * nck, 128), I32),
            jax.ShapeDtypeStruct((experts, rows), F32),
        ],
        scratch_shapes=scratch,
        compiler_params=_params("arbitrary"),
        name="mix_out_router",
    )(*v_args, w_out, h, mods, gain_ffn.reshape(1, d), mods, mods, wr)


def _qkv_body(h_ref, g_ref, sh_ref, sc_ref, w_ref, cq_ref, sq_ref, ck_ref, sk_ref, o_ref, a_scr, *, n_q, tn):
    n = pl.program_id(1)
    tm = a_scr.shape[0]

    def tile(rows, kind):
        acc = jnp.dot(a_scr[rows, :], w_ref[...], preferred_element_type=F32)
        if kind == "v":
            o_ref[rows, :] = acc.astype(BF16)
            return
        cos_ref, sin_ref = (cq_ref, sq_ref) if kind == "q" else (ck_ref, sk_ref)
        cos = cos_ref[rows, :]
        sin = sin_ref[rows, :]
        for hd in range(tn // HEAD_DIM):
            lanes = slice(hd * HEAD_DIM, (hd + 1) * HEAD_DIM)
            x = acc[:, lanes]
            r = lax.rsqrt(jnp.mean(x * x, axis=-1, keepdims=True) + NORM_EPS)
            rot = pltpu.roll(x, HEAD_DIM // 2, 1)
            o_ref[rows, lanes] = ((x * cos + rot * sin) * r).astype(BF16)

    quarters = [slice(r0, r0 + tm // 4) for r0 in range(0, tm, tm // 4)]

    @pl.when(n == 0)
    def _():
        for rows in quarters:
            a_scr[rows, :] = _modulate(h_ref[rows, :], g_ref[...], sh_ref[0], sc_ref[0]).astype(BF16)
            tile(rows, "q" if n_q > 0 else "k")

    @pl.when((n > 0) & (n < n_q))
    def _():
        for rows in quarters:
            tile(rows, "q")

    @pl.when((n > 0) & (n == n_q))
    def _():
        for rows in quarters:
            tile(rows, "k")

    @pl.when(n > n_q)
    def _():
        tile(slice(0, tm), "v")


def _rope_tables(dims, tm, gain, scale):
    quarter = HEAD_DIM // 4
    inv_freq = ROPE_THETA ** (-jnp.arange(quarter, dtype=F32) * 2.0 / (HEAD_DIM // 2))
    t = jnp.arange(dims.seq)
    ang_row = (t // dims.grid_w).astype(F32)[:, None] * inv_freq
    ang_col = (t % dims.grid_w).astype(F32)[:, None] * inv_freq
    cos = jnp.concatenate([jnp.cos(ang_row), jnp.cos(ang_col)] * 2, axis=-1)
    sin = jnp.concatenate([-jnp.sin(ang_row), -jnp.sin(ang_col), jnp.sin(ang_row), jnp.sin(ang_col)], axis=-1)
    cos = jnp.concatenate([cos, jnp.ones((tm, HEAD_DIM), F32)], axis=0)
    sin = jnp.concatenate([sin, jnp.zeros((tm, HEAD_DIM), F32)], axis=0)
    g = gain[_head_order()]
    partner_gain = jnp.roll(g, HEAD_DIM // 2)
    return cos * (g * scale), sin * (partner_gain * scale)


def _head_order():
    quarter = HEAD_DIM // 4
    return jnp.arange(HEAD_DIM).reshape(2, 2, quarter).transpose(1, 0, 2).reshape(HEAD_DIM)


def _qkvproj(dims, st, h, mods, gain, w_qkv, q_gain, k_gain, with_q):
    rows, d = h.shape
    tm, tn = st.tm(), dims.dkv
    n_q = dims.dq // tn if with_q else 0
    col0 = 0 if with_q else dims.dq // tn
    n_total = n_q + 2
    cos_q, sin_q = _rope_tables(dims, tm, q_gain, HEAD_DIM ** -0.5)
    cos_k, sin_k = _rope_tables(dims, tm, k_gain, 1.0)
    per = dims.seq // tm
    pos_tile = (lambda m, n: (m % per, 0)) if with_q else (lambda m, n: (per, 0))
    return pl.pallas_call(
        functools.partial(_qkv_body, n_q=n_q, tn=tn),
        grid=(rows // tm, n_total),
        in_specs=[
            pl.BlockSpec((tm, d), lambda m, n: (m, 0)),
            pl.BlockSpec((1, d), lambda m, n: (0, 0)),
            _mod_spec(st, tm, 0, d),
            _mod_spec(st, tm, 1, d),
            pl.BlockSpec((d, tn), lambda m, n: (0, col0 + n)),
            pl.BlockSpec((tm, HEAD_DIM), pos_tile),
            pl.BlockSpec((tm, HEAD_DIM), pos_tile),
            pl.BlockSpec((tm, HEAD_DIM), pos_tile),
            pl.BlockSpec((tm, HEAD_DIM), pos_tile),
        ],
        out_specs=pl.BlockSpec((tm, tn), lambda m, n: (m, n)),
        out_shape=jax.ShapeDtypeStruct((rows, n_total * tn), BF16),
        scratch_shapes=[pltpu.VMEM((tm, d), BF16)],
        compiler_params=_params("arbitrary", "arbitrary"),
        name="attn_qkv",
    )(h, gain.reshape(1, d), mods, mods, w_qkv, cos_q, sin_q, cos_k, sin_k)


def _attn_body(sink_ref, q_ref, k_ref, v_ref, kc_ref, vc_ref, o_ref, *, dims):
    group = dims.heads // dims.kv_heads
    blk = ATTN_BLOCK
    band = 3 * blk
    n_ctx = dims.ctx
    head0 = pl.program_id(1) * group
    kc = kc_ref[...]
    vc = vc_ref[...]

    n_blk = dims.seq // blk
    assert n_blk >= 3
    sink = jnp.concatenate([jnp.full((blk, 1), sink_ref[head0 + g], F32) for g in range(group)], axis=0)

    def mask_bias(first_key):
        q_rel = first_key + (lax.broadcasted_iota(I32, (group * blk, 1), 0) & (blk - 1))
        col = lax.broadcasted_iota(I32, (1, band + n_ctx), 1)
        valid = (col >= band) | (jnp.abs(q_rel - col) <= blk)
        return jnp.where(valid, 0.0, MASK_VALUE)

    def block(n, start, bias):
        q0 = n * blk if isinstance(n, int) else pl.multiple_of(n * blk, blk)
        kcat = jnp.concatenate([k_ref[pl.ds(start, band), :], kc], axis=0)
        vcat = jnp.concatenate([v_ref[pl.ds(start, band), :], vc], axis=0)
        q = jnp.concatenate([q_ref[pl.ds(q0, blk), g * HEAD_DIM:(g + 1) * HEAD_DIM] for g in range(group)], axis=0)
        s = lax.dot_general(q, kcat, (((1,), (1,)), ((), ())), preferred_element_type=F32) + bias
        m = jnp.maximum(jnp.max(s, axis=-1, keepdims=True), sink)
        p = jnp.exp(s - m)
        den = jnp.sum(p, axis=-1, keepdims=True) + jnp.exp(sink - m)
        o = jnp.dot(p.astype(BF16), vcat, preferred_element_type=F32) / den
        for g in range(group):
            o_ref[pl.ds(q0, blk), g * HEAD_DIM:(g + 1) * HEAD_DIM] = o[g * blk:(g + 1) * blk].astype(BF16)

    block(0, 0, mask_bias(0))
    mid_bias = mask_bias(blk)

    def middle(n, carry):
        block(n, pl.multiple_of((n - 1) * blk, blk), mid_bias)
        return carry

    lax.fori_loop(1, n_blk - 1, middle, 0, unroll=2)
    block(n_blk - 1, dims.seq - band, mask_bias(2 * blk))


def _attention(dims, qkv, kv_ctx, sink):
    group = dims.heads // dims.kv_heads
    gw = group * HEAD_DIM
    k_col = dims.dq // HEAD_DIM
    v_col = (dims.dq + dims.dkv) // HEAD_DIM
    return pl.pallas_call(
        functools.partial(_attn_body, dims=dims),
        grid=(dims.batch, dims.kv_heads),
        in_specs=[
            pl.BlockSpec(memory_space=pltpu.SMEM),
            pl.BlockSpec((dims.seq, gw), lambda b, h: (b, h)),
            pl.BlockSpec((dims.seq, HEAD_DIM), lambda b, h: (b, k_col + h)),
            pl.BlockSpec((dims.seq, HEAD_DIM), lambda b, h: (b, v_col + h)),
            pl.BlockSpec((dims.ctx, HEAD_DIM), lambda b, h: (b, h)),
            pl.BlockSpec((dims.ctx, HEAD_DIM), lambda b, h: (b, dims.kv_heads + h)),
        ],
        out_specs=pl.BlockSpec((dims.seq, gw), lambda b, h: (b, h)),
        out_shape=jax.ShapeDtypeStruct((dims.batch * dims.seq, dims.dq), BF16),
        compiler_params=_params("arbitrary", "arbitrary"),
        name="attn_core",
    )(sink, qkv, qkv, qkv, kv_ctx, kv_ctx)


def _lane_prefix(x, tri):
    n = x.shape[1]
    off = jnp.zeros((x.shape[0], 1), F32)
    chunks = []
    for ch in range(n // 128):
        xc = x[:, ch * 128:(ch + 1) * 128]
        incl = jnp.dot(xc.astype(BF16), tri, preferred_element_type=F32)
        chunks.append(incl - xc + off)
        off = off + incl[:, 127:128]
    return chunks[0] if len(chunks) == 1 else jnp.concatenate(chunks, axis=1)


def _topk_body(aff_ref, idx_ref, dst_ref, gate_ref, off_ref, cnt_ref, w_scr, a_scr, q_scr, clo_scr, chi_scr, *,
               n_tok, cap, pair0, experts):
    grp = pl.program_id(0)
    a = aff_ref[...]
    bits = pltpu.bitcast(a, I32)
    lane = lax.broadcasted_iota(I32, (experts, n_tok), 1)

    def count(mask):
        return jnp.sum(mask.astype(I32), axis=1, keepdims=True)

    def thr_step(i, thr):
        hi = jnp.left_shift(jnp.int32(1), 29 - 2 * i)
        lo = jnp.left_shift(jnp.int32(1), 28 - 2 * i)
        best = thr
        for cand in (thr | lo, thr | hi, thr | hi | lo):
            best = jnp.where(count(bits >= cand) >= cap, cand, best)
        return best

    first = jnp.int32(1 << 30)
    thr = jnp.where(count(bits >= first) >= cap, first, jnp.zeros((experts, 1), I32))
    thr = lax.fori_loop(0, 15, thr_step, thr)
    above = bits > thr
    tie = bits == thr
    need = cap - count(above)

    top_bit = n_tok.bit_length() - 2

    def tie_try(lim, cands):
        for cand in cands:
            lim = jnp.where(count(tie & (lane < cand)) < need, cand, lim)
        return lim

    def tie_step(i, lim):
        hi = jnp.left_shift(jnp.int32(1), top_bit - odd - 2 * i)
        lo = jnp.left_shift(jnp.int32(1), top_bit - odd - 2 * i - 1)
        return tie_try(lim, (lim | lo, lim | hi, lim | hi | lo))

    odd = (top_bit + 1) % 2
    lim = jnp.zeros((experts, 1), I32)
    if odd:
        lim = tie_try(lim, (lim | (1 << top_bit),))
    lim = lax.fori_loop(0, (top_bit + 1) // 2, tie_step, lim)
    sel = above | (tie & (lane <= lim))
    sel_f = sel.astype(F32)
    sel_i = sel.astype(I32)

    r = lax.broadcasted_iota(I32, (128, 128), 0)
    c = lax.broadcasted_iota(I32, (128, 128), 1)
    tri = (r <= c).astype(F32).astype(BF16)

    level = jnp.zeros((1, n_tok), I32)
    levels = []
    for e in range(experts):
        levels.append(level)
        level = level + sel_i[e:e + 1, :]
    cnt = level
    off = _lane_prefix(cnt.astype(F32), tri).astype(I32) + (pair0 + grp * (experts * cap))
    off_ref[...] = off
    cnt_ref[...] = cnt
    pair_row = jnp.concatenate(levels, axis=0) + off

    n_ch = n_tok // 128
    pad = w_scr.shape[0] // experts
    lane_e = lax.broadcasted_iota(I32, (experts, 128), 1)
    w_scr[...] = jnp.zeros_like(w_scr)
    a_scr[...] = jnp.zeros_like(a_scr)
    q_scr[...] = jnp.zeros_like(q_scr)
    first = jnp.zeros((experts, 1), F32)
    c_lo = jnp.full((experts, 128), cap, I32)
    c_hi = jnp.full((experts, 128), cap, I32)
    for ch in range(n_ch):
        lanes = slice(ch * 128, (ch + 1) * 128)
        incl = jnp.dot(sel_f[:, lanes].astype(BF16), tri, preferred_element_type=F32)
        w_scr[pl.ds(ch, experts, stride=pad), :] = incl
        a_scr[pl.ds(ch, experts, stride=pad), :] = a[:, lanes]
        q_scr[pl.ds(ch, experts, stride=pad), :] = pair_row[:, lanes]
        nxt = first + incl[:, 127:128]
        c_lo = jnp.where(lane_e == ch, first.astype(I32), c_lo)
        c_hi = jnp.where(lane_e == ch, nxt.astype(I32), c_hi)
        first = nxt
    clo_scr[...] = c_lo
    chi_scr[...] = c_hi

    idx_ref[0] = jnp.zeros((cap, 128), I32)
    dst_ref[0] = jnp.zeros((cap, 128), I32)
    gate_ref[0] = jnp.zeros((cap, 128), F32)
    slot = lax.broadcasted_iota(I32, (cap, 1), 0)
    lane = lax.broadcasted_iota(I32, (1, 128), 1)
    zpad = jnp.zeros((128 - pad, 128), F32)

    def rhs(m):
        return jnp.concatenate([m, zpad], axis=0).astype(BF16)

    def per_expert(e, carry):
        base = pl.multiple_of(e * pad, pad)
        wm = w_scr[pl.ds(base, pad), :]
        am = a_scr[pl.ds(base, pad), :]
        qm = q_scr[pl.ds(base, pad), :]
        lo = clo_scr[pl.ds(e, 1), :]
        hi = chi_scr[pl.ds(e, 1), :]
        in_chunk = (lo <= slot) & (slot < hi)
        onehot = in_chunk.astype(F32).astype(BF16)

        def rows_of(m):
            return jnp.dot(onehot, rhs(m), preferred_element_type=F32)

        a1 = am.astype(BF16).astype(F32)
        a2 = (am - a1).astype(BF16).astype(F32)
        a3 = am - a1 - a2
        a_rows = rows_of(a1) + rows_of(a2) + rows_of(a3)
        q_rows = (rows_of(lax.shift_right_logical(qm, 8).astype(F32)) * 256.0
                  + rows_of((qm & 255).astype(F32)))
        w_rows = rows_of(wm)
        chunk_first = jnp.sum(jnp.where(in_chunk, lo, 0), axis=1, keepdims=True)
        chunk = jnp.sum(jnp.where(in_chunk, lane, 0), axis=1, keepdims=True)
        rank = (slot - chunk_first).astype(F32)
        pos = jnp.sum((w_rows <= rank).astype(I32), axis=1, keepdims=True)
        here = lane == pos
        gate = jnp.sum(jnp.where(here, a_rows, 0.0), axis=1, keepdims=True)
        dst = jnp.sum(jnp.where(here, q_rows, 0.0), axis=1, keepdims=True).astype(I32)
        token = chunk * 128 + pos + grp * n_tok
        mine = lane == e
        idx_ref[0] = jnp.where(mine, token, idx_ref[0])
        dst_ref[0] = jnp.where(mine, dst, dst_ref[0])
        gate_ref[0] = jnp.where(mine, gate, gate_ref[0])
        return carry

    lax.fori_loop(0, experts, per_expert, 0)


def _topk(st, aff, cap, pair0):
    e = aff.shape[0]
    n_tok, n_groups = st.seq, st.batch
    pad = max(8, n_tok // 128)
    assert e <= 128 and pad <= 128 and n_groups * e * cap < 2 ** 16 * 256
    tile_spec = pl.BlockSpec((1, cap, 128), lambda g: (g, 0, 0))
    return pl.pallas_call(
        functools.partial(_topk_body, n_tok=n_tok, cap=cap, pair0=pair0, experts=e),
        grid=(n_groups,),
        in_specs=[pl.BlockSpec((e, n_tok), lambda g: (0, g))],
        out_specs=[tile_spec, tile_spec, tile_spec,
                   pl.BlockSpec((1, n_tok), lambda g: (0, g)),
                   pl.BlockSpec((1, n_tok), lambda g: (0, g))],
        out_shape=[jax.ShapeDtypeStruct((n_groups, cap, 128), I32),
                   jax.ShapeDtypeStruct((n_groups, cap, 128), I32),
                   jax.ShapeDtypeStruct((n_groups, cap, 128), F32),
                   jax.ShapeDtypeStruct((1, n_groups * n_tok), I32),
                   jax.ShapeDtypeStruct((1, n_groups * n_tok), I32)],
        scratch_shapes=[pltpu.VMEM((e * pad, 128), F32), pltpu.VMEM((e * pad, 128), F32),
                        pltpu.VMEM((e * pad, 128), I32), pltpu.VMEM((e, 128), I32), pltpu.VMEM((e, 128), I32)],
        compiler_params=_params("arbitrary"),
        name="moe_topk",
    )(aff)


def _expert_body(*refs, n_src, src_slots, slots, rc, n_f, n_n, tf, n_e):
    idx_ref, idx_nxt_ref, dst_ref, dst_prv_ref = refs[0:4]
    srcs = refs[4:4 + n_src]
    gate_ref, wg_ref, wu_ref, wd_ref, pairs_hbm, x_scr, y_scr, mid_scr, gcol_scr, gsem, ssem = refs[4 + n_src:]
    e = pl.program_id(0)
    s = pl.program_id(1)
    nck = x_scr.shape[0] // slots
    half = nck * 128
    tnc = nck // n_n

    def tok(ref, r):
        return ref.at[pl.ds(pl.multiple_of(r * nck, nck), nck), :]

    def row_loop(lo, n, start_row):
        def eight(i, carry):
            for k in range(8):
                start_row(lo + i * 8 + k)
            return carry
        lax.fori_loop(0, n // 8, eight, 0)

    def gather_row(ref, r, slot):
        src = next(src for (s0, s1), src in zip(src_slots, srcs) if s0 <= slot < s1)
        pltpu.make_async_copy(tok(src, ref[0, 0, r]), tok(x_scr, r), gsem).start()

    def scatter_row(ref, r):
        pltpu.make_async_copy(tok(y_scr, r), tok(pairs_hbm, ref[0, 0, r]), ssem).start()

    n_chunks = slots // rc
    dma_chunks = max(n_chunks - 1, 1)

    def share(total, ci):
        base, extra = divmod(total, dma_chunks)
        if ci >= dma_chunks:
            return total, 0
        return ci * base + min(ci, extra), base + (1 if ci < extra else 0)

    def wait_gather():
        pltpu.make_async_copy(srcs[0].at[pl.ds(0, slots * nck), :], x_scr, gsem).wait()

    def wait_scatter():
        pltpu.make_async_copy(y_scr, pairs_hbm.at[pl.ds(0, slots * nck), :], ssem).wait()

    def x_rows(r0):
        parts = [_unpack_bf16_pair(x_scr[pl.ds(r0 * nck + c, rc, stride=nck), :]) for c in range(nck)]
        return (jnp.concatenate([p[0] for p in parts], axis=1), jnp.concatenate([p[1] for p in parts], axis=1))

    @pl.when((e == 0) & (s == 0))
    def _():
        y_scr[...] = jnp.zeros_like(y_scr)
        for (s0, s1), src in zip(src_slots, srcs):
            row_loop(s0, s1 - s0, lambda r, src=src: pltpu.make_async_copy(
                tok(src, idx_ref[0, 0, r]), tok(x_scr, r), gsem).start())

    @pl.when(s == 0)
    def _():
        wait_gather()
        eye = lax.broadcasted_iota(I32, (128, 128), 0) == lax.broadcasted_iota(I32, (128, 128), 1)
        for c in range(slots // 128):
            row = gate_ref[0, c:c + 1, :]
            gcol_scr[c * 128:(c + 1) * 128, :] = jnp.sum(jnp.where(eye, row, 0.0), axis=1, keepdims=True)

    @pl.when(s < n_f)
    def _():
        wg = wg_ref[0, 0].astype(BF16)
        wu = wu_ref[0, 0].astype(BF16)
        for r0 in range(0, slots, rc):
            lo, hi = x_rows(r0)
            a = (jnp.dot(lo, wg[:half], preferred_element_type=F32)
                 + jnp.dot(hi, wg[half:], preferred_element_type=F32))
            u = (jnp.dot(lo, wu[:half], preferred_element_type=F32)
                 + jnp.dot(hi, wu[half:], preferred_element_type=F32))
            mid_scr[s, r0:r0 + rc, :] = (a * jax.nn.sigmoid(a) * u).astype(BF16)
            off, cnt = share(slots // n_f, r0 // rc)
            for k in range(cnt):
                scatter_row(dst_prv_ref, s * (slots // n_f) + off + k)

    @pl.when(s == n_f)
    def _():
        wait_scatter()

    for j in range(n_n):
        @pl.when(s == n_f + j)
        def _():
            wd = wd_ref[0, 0].astype(BF16)
            for r0 in range(0, slots, rc):
                y = jnp.dot(mid_scr[0, r0:r0 + rc, :], wd[0:tf], preferred_element_type=F32)
                for f in range(1, n_f):
                    y += jnp.dot(mid_scr[f, r0:r0 + rc, :], wd[f * tf:(f + 1) * tf], preferred_element_type=F32)
                y = y * gcol_scr[r0:r0 + rc, :]
                tnw = tnc * 128
                packed = _pack_bf16_pair(y[:, :tnw], y[:, tnw:])
                for c in range(tnc):
                    y_scr[pl.ds(r0 * nck + j * tnc + c, rc, stride=nck), :] = packed[:, c * 128:(c + 1) * 128]
                g_steps = max(n_n - 1, 1)
                step_rows = -(-slots // g_steps)
                lo = min(j * step_rows, slots)
                off, cnt = share(min(step_rows, slots - lo), r0 // rc)
                for k in range(cnt):
                    slot = lo + off + k
                    gather_row(idx_nxt_ref, slot, slot)

    @pl.when((e == n_e - 1) & (s == n_f + n_n - 1))
    def _():
        wait_gather()
        row_loop(0, slots, lambda r: scatter_row(dst_ref, r))
        wait_scatter()


def _experts(idx, dst, gate, sources, src_slots, w_gate, w_up, w_down, layer, n_pairs):
    n_e, slots = idx.shape
    d, fdim = w_gate.shape[2], w_gate.shape[3]
    tf, tn = 256, EXPERT_DOWN_TILE
    n_f, n_n = fdim // tf, d // tn
    rc = slots // 4
    nck = d // 256
    assert slots % 128 == 0 and rc % 16 == 0 and nck % n_n == 0 and slots % n_f == 0 and slots % n_n == 0
    assert all(s0 % 8 == 0 and s1 % 8 == 0 for s0, s1 in src_slots)
    idx3 = idx.reshape(n_e, 1, slots)
    dst3 = dst.reshape(n_e, 1, slots)
    smem = lambda f: pl.BlockSpec((1, 1, slots), f, memory_space=pltpu.SMEM)
    up_chunk = lambda i, s: (layer, i, 0, jnp.minimum(s, n_f - 1))
    return pl.pallas_call(
        functools.partial(_expert_body, n_src=len(sources), src_slots=src_slots, slots=slots, rc=rc,
                          n_f=n_f, n_n=n_n, tf=tf, n_e=n_e),
        grid=(n_e, n_f + n_n),
        in_specs=[
            smem(lambda i, s: (i, 0, 0)),
            smem(lambda i, s: (jnp.minimum(i + 1, n_e - 1), 0, 0)),
            smem(lambda i, s: (i, 0, 0)),
            smem(lambda i, s: (jnp.maximum(i - 1, 0), 0, 0)),
        ] + [pl.BlockSpec(memory_space=pl.ANY)] * len(sources) + [
            pl.BlockSpec((1, slots // 128, 128), lambda i, s: (i, 0, 0)),
            pl.BlockSpec((1, 1, d, tf), up_chunk),
            pl.BlockSpec((1, 1, d, tf), up_chunk),
            pl.BlockSpec((1, 1, fdim, tn), lambda i, s: (layer, i, 0, jnp.maximum(s - n_f, 0))),
        ],
        out_specs=pl.BlockSpec(memory_space=pl.ANY),
        out_shape=jax.ShapeDtypeStruct((n_pairs * nck, 128), I32),
        scratch_shapes=[
            pltpu.VMEM((slots * nck, 128), I32),
            pltpu.VMEM((slots * nck, 128), I32),
            pltpu.VMEM((n_f, slots, tf), BF16),
            pltpu.VMEM((slots, 1), F32),
            pltpu.SemaphoreType.DMA(()),
            pltpu.SemaphoreType.DMA(()),
        ],
        compiler_params=_params("arbitrary", "arbitrary"),
        name="moe_experts",
    )(idx3, idx3, dst3, dst3, *sources, gate.reshape(n_e, slots // 128, 128), w_gate, w_up, w_down)


def _combine_body(lo_ref, hi_ref, off_ref, cnt_ref, p_ref, h_ref, g2_ref, o_ref, acc_lo, acc_hi, *,
                  tt, nck, tnw, pair0, pairs_per_group):
    base = pair0 + pl.program_id(0) * pairs_per_group
    i = pl.program_id(1)
    lane = lax.broadcasted_iota(I32, (1, PAIR_BLOCK), 1)
    eye = lax.broadcasted_iota(I32, (128, 128), 0) == lax.broadcasted_iota(I32, (128, 128), 1)

    def column(row):
        cols = [jnp.sum(jnp.where(eye, row[:, c0:c0 + 128], 0), axis=1, keepdims=True) for c0 in range(0, tt, 128)]
        return cols[0] if len(cols) == 1 else jnp.concatenate(cols, axis=0)

    first = column(off_ref[...]) - base
    last = first + column(cnt_ref[...])
    shift = PAIR_BLOCK.bit_length() - 1
    k0 = lax.shift_right_logical(lo_ref[0, 0, i] - base, shift)
    k1 = lax.shift_right_logical(hi_ref[0, 0, i] - base + (PAIR_BLOCK - 1), shift)
    acc_lo[...] = jnp.zeros_like(acc_lo)
    acc_hi[...] = jnp.zeros_like(acc_hi)

    def step(k, carry):
        p0 = pl.multiple_of(k * PAIR_BLOCK, PAIR_BLOCK)
        parts = [_unpack_bf16_pair(p_ref[pl.ds(p0 * nck + c, PAIR_BLOCK, stride=nck), :]) for c in range(nck)]
        lo = jnp.concatenate([p[0] for p in parts], axis=1)
        hi = jnp.concatenate([p[1] for p in parts], axis=1)
        pr = p0 + lane
        seg = ((pr >= first) & (pr < last)).astype(F32).astype(BF16)
        acc_lo[...] += jnp.dot(seg, lo, preferred_element_type=F32)
        acc_hi[...] += jnp.dot(seg, hi, preferred_element_type=F32)
        return carry

    lax.fori_loop(k0, k1, step, 0)
    for j in range(nck * 128 // tnw):
        words = slice(j * tnw, (j + 1) * tnw)
        c_lo = slice(2 * j * tnw, (2 * j + 1) * tnw)
        c_hi = slice((2 * j + 1) * tnw, (2 * j + 2) * tnw)
        o_ref[:, c_lo] = h_ref[:, c_lo] + g2_ref[0, :, c_lo] * acc_lo[:, words]
        o_ref[:, c_hi] = h_ref[:, c_hi] + g2_ref[0, :, c_hi] * acc_hi[:, words]


def _combine(st, off, cnt, pairs, h, mods, pair0, pairs_per_group):
    rows, d = h.shape
    n_tok = st.seq
    tt = min(256, n_tok)
    n_tiles = n_tok // tt
    nck = d // 256
    assert pairs_per_group % PAIR_BLOCK == 0 and pair0 % pairs_per_group == 0
    off_t = off.reshape(st.batch, 1, n_tiles, tt)
    cnt_t = cnt.reshape(st.batch, 1, n_tiles, tt)
    tile_lo = off_t[..., 0]
    tile_hi = off_t[..., tt - 1] + cnt_t[..., tt - 1]
    row = st.ada_row(n_tok)
    smem = pl.BlockSpec((1, 1, n_tiles), lambda b, i: (b, 0, 0), memory_space=pltpu.SMEM)
    tile = lambda b, i: (b * n_tiles + i, 0)
    return pl.pallas_call(
        functools.partial(_combine_body, tt=tt, nck=nck, tnw=EXPERT_DOWN_TILE // 2, pair0=pair0,
                          pairs_per_group=pairs_per_group),
        grid=(st.batch, n_tiles),
        in_specs=[
            smem, smem,
            pl.BlockSpec((1, tt), lambda b, i: (0, b * n_tiles + i)),
            pl.BlockSpec((1, tt), lambda b, i: (0, b * n_tiles + i)),
            pl.BlockSpec((pairs_per_group * nck, 128), lambda b, i: (pair0 // pairs_per_group + b, 0)),
            pl.BlockSpec((tt, d), tile),
            pl.BlockSpec((1, 1, d), lambda b, i: (row(b) * N_ADA + 5, 0, 0)),
        ],
        out_specs=pl.BlockSpec((tt, d), tile),
        out_shape=jax.ShapeDtypeStruct((rows, d), F32),
        scratch_shapes=[pltpu.VMEM((tt, d // 2), F32)] * 2,
        compiler_params=_params("arbitrary", "arbitrary"),
        name="moe_combine",
    )(tile_lo, tile_hi, off, cnt, pairs, h, mods)


def _moe(streams, mixed, mods, w_gate, w_up, w_down, layer):
    n_e = mixed[0][2].shape[0]
    routed = []
    pair0 = 0
    slot0 = 0
    src_slots = []
    for st, (h, h2, aff) in zip(streams, mixed):
        cap = CAP_FACTOR * st.seq // n_e
        idx, dst, gate, off, cnt = _topk(st, aff, cap, pair0)
        routed.append((h2, idx, dst, gate, off, cnt, pair0, n_e * cap))
        src_slots.append((slot0, slot0 + st.batch * cap))
        pair0 += st.batch * n_e * cap
        slot0 += st.batch * cap
    flat = lambda t: t[:, :, :n_e].transpose(2, 0, 1).reshape(n_e, -1)
    idx = jnp.concatenate([flat(r[1]) for r in routed], axis=1)
    dst = jnp.concatenate([flat(r[2]) for r in routed], axis=1)
    gate = jnp.concatenate([flat(r[3]) for r in routed], axis=1)
    pairs = _experts(idx, dst, gate, [r[0] for r in routed], tuple(src_slots), w_gate, w_up, w_down, layer, pair0)
    return [_combine(st, r[4], r[5], pairs, m[0], mods, r[6], r[7]) for st, m, r in zip(streams, mixed, routed)]


def _forward(dims, x, c, ctx, c_ctx, ada_w, ada_b, norm_mix_g, norm_ffn_g, conv_w_in, conv_w, conv_w_out,
             attn_w_qkv, attn_q_gain, attn_k_gain, attn_sink, attn_w_o, router_w,
             expert_w_gate, expert_w_up, expert_w_down):
    d = dims.d
    lat = Stream(dims.batch, dims.seq)
    con = Stream(dims.batch, dims.ctx, shared_row=dims.batch)
    cvec = jnp.zeros((ADA_ROWS, d), F32).at[:dims.batch].set(c).at[dims.batch].set(c_ctx)
    mods = _adaln(cvec, ada_w, ada_b)
    mods = mods.reshape(mods.shape[0], ADA_ROWS * N_ADA, 1, d)
    hs = [x.reshape(lat.rows, d), ctx.reshape(con.rows, d)]

    w_in, w_out = conv_w_in[0].astype(BF16), conv_w_out[0].astype(BF16)
    mixed = []
    for st, h in zip((lat, con), hs):
        gb_u = _inproj(st, h, mods[0], norm_mix_g[0], w_in)
        mixed.append(_mixout(st, gb_u, conv_w[0], w_out, h, mods[0], norm_ffn_g[0], router_w[0]))
    h_lat, h_ctx = _moe((lat, con), mixed, mods[0], expert_w_gate, expert_w_up, expert_w_down, 0)

    order = _head_order()
    w_qkv = attn_w_qkv[0].astype(BF16)
    w_q = w_qkv[:, :dims.dq].reshape(d, dims.heads, HEAD_DIM)[:, :, order].reshape(d, dims.dq)
    w_k = w_qkv[:, dims.dq:dims.dq + dims.dkv].reshape(d, dims.kv_heads, HEAD_DIM)[:, :, order].reshape(d, dims.dkv)
    w_qkv = jnp.concatenate([w_q, w_k, w_qkv[:, dims.dq + dims.dkv:]], axis=1)
    w_o = attn_w_o[0].astype(BF16)
    qkv = _qkvproj(dims, lat, h_lat, mods[1], norm_mix_g[1], w_qkv, attn_q_gain[0], attn_k_gain[0], True)
    kv_ctx = _qkvproj(dims, con, h_ctx, mods[1], norm_mix_g[1], w_qkv, attn_q_gain[0], attn_k_gain[0], False)
    o = _attention(dims, qkv, kv_ctx, attn_sink[0])
    mixed = [_mixout(lat, (o,), None, w_o, h_lat, mods[1], norm_ffn_g[1], router_w[1])]
    (h_lat,) = _moe((lat,), mixed, mods[1], expert_w_gate, expert_w_up, expert_w_down, 1)
    return h_lat.reshape(dims.batch, dims.seq, d)


def kernel(x, c, ctx, c_ctx, ada_w, ada_b, norm_mix_g, norm_ffn_g, conv_w_in, conv_w, conv_w_out, attn_w_qkv, attn_q_gain, attn_k_gain, attn_sink, attn_w_o, router_w, expert_w_gate, expert_w_up, expert_w_down):
    batch, seq, d = x.shape
    dims = Dims(d=d, batch=batch, seq=seq, grid_w=GRID_W, ctx=ctx.shape[1], heads=d // HEAD_DIM,
                kv_heads=(attn_w_qkv.shape[2] // HEAD_DIM - d // HEAD_DIM) // 2,
                experts=router_w.shape[2], d_expert=expert_w_gate.shape[3])
    return _forward(dims, x, c, ctx, c_ctx, ada_w, ada_b, norm_mix_g, norm_ffn_g, conv_w_in, conv_w, conv_w_out,
                    attn_w_qkv, attn_q_gain, attn_k_gain, attn_sink, attn_w_o, router_w,
                    expert_w_gate, expert_w_up, expert_w_down)
```

```python
import dataclasses
import functools

import jax
import jax.numpy as jnp
from jax import lax
from jax.experimental import pallas as pl
from jax.experimental.pallas import tpu as pltpu

F32 = jnp.float32
BF16 = jnp.bfloat16
I32 = jnp.int32

NORM_EPS = 1e-6
MASK_VALUE = -1e30
ROPE_THETA = 10000.0
GRID_W = 64
CAP_FACTOR = 2
HEAD_DIM = 128
ATTN_BLOCK = 128
N_ADA = 6
ADA_ROWS = 16
HI_MASK = -65536
PAIR_BLOCK = 256
EXPERT_DOWN_TILE = 512
VMEM_LIMIT = 58 * 1024 * 1024


@dataclasses.dataclass(frozen=True)
class Dims:
    d: int
    batch: int
    seq: int
    grid_w: int
    ctx: int
    heads: int
    kv_heads: int
    experts: int
    d_expert: int

    @property
    def dq(self):
        return self.heads * HEAD_DIM

    @property
    def dkv(self):
        return self.kv_heads * HEAD_DIM


@dataclasses.dataclass(frozen=True)
class Stream:
    batch: int
    seq: int
    shared_row: int = -1

    @property
    def rows(self):
        return self.batch * self.seq

    def tm(self, cap=1024):
        unit = self.rows if self.shared_row >= 0 else self.seq
        t = cap
        while unit % t:
            t //= 2
        return t

    def ada_row(self, tm):
        if self.shared_row >= 0:
            return lambda m: self.shared_row
        per = self.seq // tm
        assert per >= 1
        return lambda m: m // per


def _params(*sem):
    return pltpu.CompilerParams(dimension_semantics=sem, vmem_limit_bytes=VMEM_LIMIT)


def _modulate(h, gain, shift, scale):
    ms = jnp.mean(h * h, axis=-1, keepdims=True)
    xn = h * lax.rsqrt(ms + NORM_EPS)
    return (xn * gain) * (1.0 + scale) + shift


def _mod_spec(st, tm, which, d, width=None):
    row = st.ada_row(tm)
    if width is None:
        return pl.BlockSpec((1, 1, d), lambda m, n: (row(m) * N_ADA + which, 0, 0))
    return pl.BlockSpec((1, 1, width), lambda m, n: (row(m) * N_ADA + which, 0, n))


def _pack_bf16_pair(lo, hi):
    lo_bits = pltpu.bitcast(lo.astype(BF16).astype(F32), I32)
    hi_bits = pltpu.bitcast(hi.astype(BF16).astype(F32), I32)
    return (hi_bits & HI_MASK) | lax.shift_right_logical(lo_bits, 16)


def _unpack_bf16_pair(w):
    lo = pltpu.bitcast(lax.shift_left(w, 16), F32).astype(BF16)
    hi = pltpu.bitcast(w & HI_MASK, F32).astype(BF16)
    return lo, hi


def _adaln_body(c_ref, w_ref, b_ref, o_ref):
    c = c_ref[...]
    s = c * jax.nn.sigmoid(c)
    o_ref[0] = jnp.dot(s.astype(BF16), w_ref[0].astype(BF16), preferred_element_type=F32) + b_ref[0]


def _adaln(cvec, ada_w, ada_b):
    depth, d, n = ada_w.shape
    tn = min(1024, n)
    return pl.pallas_call(
        _adaln_body,
        grid=(depth, n // tn),
        in_specs=[
            pl.BlockSpec((ADA_ROWS, d), lambda l, j: (0, 0)),
            pl.BlockSpec((1, d, tn), lambda l, j: (l, 0, j)),
            pl.BlockSpec((1, 1, tn), lambda l, j: (l, 0, j)),
        ],
        out_specs=pl.BlockSpec((1, ADA_ROWS, tn), lambda l, j: (l, 0, j)),
        out_shape=jax.ShapeDtypeStruct((depth, ADA_ROWS, n), F32),
        compiler_params=_params("arbitrary", "arbitrary"),
        name="adaln",
    )(cvec, ada_w, ada_b.reshape(depth, 1, n))


def _inproj_body(h_ref, g_ref, sh_ref, sc_ref, wb_ref, wc_ref, wx_ref, gb_ref, u_ref, a_scr):
    def project(rows):
        a = a_scr[rows, :]
        gb = jnp.dot(a, wb_ref[...], preferred_element_type=F32)
        gc = jnp.dot(a, wc_ref[...], preferred_element_type=F32)
        xv = jnp.dot(a, wx_ref[...], preferred_element_type=F32)
        gb_ref[rows, :] = gb.astype(BF16)
        u_ref[rows, :] = (gc * xv).astype(BF16)

    tm = a_scr.shape[0]

    @pl.when(pl.program_id(1) == 0)
    def _():
        for r0 in range(0, tm, tm // 4):
            rows = slice(r0, r0 + tm // 4)
            a_scr[rows, :] = _modulate(h_ref[rows, :], g_ref[...], sh_ref[0], sc_ref[0]).astype(BF16)
            project(rows)

    @pl.when(pl.program_id(1) > 0)
    def _():
        project(slice(0, tm))


def _inproj(st, h, mods, gain, w_in):
    rows, d = h.shape
    tm, tn = st.tm(), 512
    nt = d // tn
    return pl.pallas_call(
        _inproj_body,
        grid=(rows // tm, nt),
        in_specs=[
            pl.BlockSpec((tm, d), lambda m, n: (m, 0)),
            pl.BlockSpec((1, d), lambda m, n: (0, 0)),
            _mod_spec(st, tm, 0, d),
            _mod_spec(st, tm, 1, d),
            pl.BlockSpec((d, tn), lambda m, n: (0, n)),
            pl.BlockSpec((d, tn), lambda m, n: (0, nt + n)),
            pl.BlockSpec((d, tn), lambda m, n: (0, 2 * nt + n)),
        ],
        out_specs=[pl.BlockSpec((tm, tn), lambda m, n: (m, n))] * 2,
        out_shape=[jax.ShapeDtypeStruct((rows, d), BF16)] * 2,
        scratch_shapes=[pltpu.VMEM((tm, d), BF16)],
        compiler_params=_params("arbitrary", "arbitrary"),
        name="conv_inproj",
    )(h, gain.reshape(1, d), mods, mods, w_in, w_in, w_in)


def _mixout_body(*refs, conv, seq, tm, experts, cw):
    if conv:
        gb_ref, u_ref, up_ref, un_ref, cw_ref = refs[:5]
        refs = refs[5:]
        v_scr = refs[-1]
        halo = up_ref.shape[0]
        row = lax.broadcasted_iota(I32, (tm, 1), 0)
        pos = (pl.program_id(0) * tm + row) & (seq - 1)
        for c0 in range(0, u_ref.shape[1], cw):
            cols = slice(c0, c0 + cw)
            u = u_ref[:, cols].astype(F32)
            u_dn = pltpu.roll(u, 1, 0)
            u_dn = jnp.where(row == 0, up_ref[halo - 1:halo, cols].astype(F32), u_dn)
            u_dn = jnp.where(pos == 0, 0.0, u_dn)
            u_up = pltpu.roll(u, tm - 1, 0)
            u_up = jnp.where(row == tm - 1, un_ref[0:1, cols].astype(F32), u_up)
            u_up = jnp.where(pos == seq - 1, 0.0, u_up)
            y = cw_ref[0:1, cols] * u_dn + cw_ref[1:2, cols] * u + cw_ref[2:3, cols] * u_up
            v_scr[:, cols] = (gb_ref[:, cols].astype(F32) * y).astype(BF16)
        v_ref = v_scr
    else:
        v_ref = refs[0]
        refs = refs[1:]
    w_ref, h_ref, g1_ref, gf_ref, sh_ref, sc_ref, wr_ref, hn_ref, h2_ref, aff_ref = refs[:10]
    half = h_ref.shape[1] // 2
    nck = half // 128
    th = tm // 2
    for r0 in (0, th):
        rows = slice(r0, r0 + th)
        out = jnp.dot(v_ref[rows, :], w_ref[...], preferred_element_type=F32)
        hn = h_ref[rows, :] + g1_ref[0] * out
        hn_ref[rows, :] = hn
        a = _modulate(hn, gf_ref[...], sh_ref[0], sc_ref[0])
        packed = _pack_bf16_pair(a[:, :half], a[:, half:])
        for j in range(nck):
            h2_ref[pl.ds(r0 * nck + j, th, stride=nck), :] = packed[:, j * 128:(j + 1) * 128]
        logits = jnp.dot(a.astype(BF16), wr_ref[...], preferred_element_type=F32)
        lt = logits.T[0:experts, :]
        ex = jnp.exp(lt - jnp.max(lt, axis=0, keepdims=True))
        aff_ref[:, rows] = ex / jnp.sum(ex, axis=0, keepdims=True)


def _mixout(st, v_inputs, conv_w, w_out, h, mods, gain_ffn, w_router):
    rows, d = h.shape
    experts = w_router.shape[1]
    conv = conv_w is not None
    tm, halo = st.tm(512), 16
    nck = d // 256
    assert st.seq & (st.seq - 1) == 0 and tm % 256 == 0
    row = st.ada_row(tm)
    mod = lambda which: pl.BlockSpec((1, 1, d), lambda m: (row(m) * N_ADA + which, 0, 0))
    tile = pl.BlockSpec((tm, d), lambda m: (m, 0))
    const = lambda shape: pl.BlockSpec(shape, lambda m: (0,) * len(shape))
    wr = jnp.zeros((d, 128), BF16).at[:, :experts].set(w_router.astype(BF16))
    if conv:
        gb, u = v_inputs
        per, last = tm // halo, rows // halo - 1
        v_specs = [tile, tile,
                   pl.BlockSpec((halo, d), lambda m: (jnp.maximum(m * per - 1, 0), 0)),
                   pl.BlockSpec((halo, d), lambda m: (jnp.minimum((m + 1) * per, last), 0)),
                   const((3, d))]
        v_args = (gb, u, u, u, conv_w)
        scratch = [pltpu.VMEM((tm, d), BF16)]
    else:
        v_specs, v_args, scratch = [tile], tuple(v_inputs), []
    return pl.pallas_call(
        functools.partial(_mixout_body, conv=conv, seq=st.seq, tm=tm, experts=experts, cw=512),
        grid=(rows // tm,),
        in_specs=v_specs + [
            pl.BlockSpec((d, d), lambda m: (0, 0), pipeline_mode=pl.Buffered(1)),
            tile, mod(2), const((1, d)), mod(3), mod(4), const((d, 128)),
        ],
        out_specs=[
            tile,
            pl.BlockSpec((tm * nck, 128), lambda m: (m, 0)),
            pl.BlockSpec((experts, tm), lambda m: (0, m)),
        ],
        out_shape=[
            jax.ShapeDtypeStruct((rows, d), F32),
            jax.ShapeDtypeStruct((rows * nck, 128), I32),
            jax.ShapeDtypeStruct((experts, rows), F32),
        ],
        scratch_shapes=scratch,
        compiler_params=_params("arbitrary"),
        name="mix_out_router",
    )(*v_args, w_out, h, mods, gain_ffn.reshape(1, d), mods, mods, wr)


def _qkv_body(h_ref, g_ref, sh_ref, sc_ref, w_ref, cq_ref, sq_ref, ck_ref, sk_ref, o_ref, a_scr, *, n_q, tn):
    n = pl.program_id(1)
    tm = a_scr.shape[0]

    def tile(rows, kind):
        acc = jnp.dot(a_scr[rows, :], w_ref[...], preferred_element_type=F32)
        if kind == "v":
            o_ref[rows, :] = acc.astype(BF16)
            return
        cos_ref, sin_ref = (cq_ref, sq_ref) if kind == "q" else (ck_ref, sk_ref)
        cos = cos_ref[rows, :]
        sin = sin_ref[rows, :]
        for hd in range(tn // HEAD_DIM):
            lanes = slice(hd * HEAD_DIM, (hd + 1) * HEAD_DIM)
            x = acc[:, lanes]
            r = lax.rsqrt(jnp.mean(x * x, axis=-1, keepdims=True) + NORM_EPS)
            rot = pltpu.roll(x, HEAD_DIM // 2, 1)
            o_ref[rows, lanes] = ((x * cos + rot * sin) * r).astype(BF16)

    quarters = [slice(r0, r0 + tm // 4) for r0 in range(0, tm, tm // 4)]

    @pl.when(n == 0)
    def _():
        for rows in quarters:
            a_scr[rows, :] = _modulate(h_ref[rows, :], g_ref[...], sh_ref[0], sc_ref[0]).astype(BF16)
            tile(rows, "q" if n_q > 0 else "k")

    @pl.when((n > 0) & (n < n_q))
    def _():
        for rows in quarters:
            tile(rows, "q")

    @pl.when((n > 0) & (n == n_q))
    def _():
        for rows in quarters:
            tile(rows, "k")

    @pl.when(n > n_q)
    def _():
        tile(slice(0, tm), "v")


def _rope_tables(dims, tm, gain, scale):
    quarter = HEAD_DIM // 4
    inv_freq = ROPE_THETA ** (-jnp.arange(quarter, dtype=F32) * 2.0 / (HEAD_DIM // 2))
    t = jnp.arange(dims.seq)
    ang_row = (t // dims.grid_w).astype(F32)[:, None] * inv_freq
    ang_col = (t % dims.grid_w).astype(F32)[:, None] * inv_freq
    cos = jnp.concatenate([jnp.cos(ang_row), jnp.cos(ang_col)] * 2, axis=-1)
    sin = jnp.concatenate([-jnp.sin(ang_row), -jnp.sin(ang_col), jnp.sin(ang_row), jnp.sin(ang_col)], axis=-1)
    cos = jnp.concatenate([cos, jnp.ones((tm, HEAD_DIM), F32)], axis=0)
    sin = jnp.concatenate([sin, jnp.zeros((tm, HEAD_DIM), F32)], axis=0)
    g = gain.reshape(2, 2, quarter).transpose(1, 0, 2).reshape(HEAD_DIM)
    partner_gain = jnp.concatenate([g[HEAD_DIM // 2:], g[:HEAD_DIM // 2]])
    return cos * (g * scale), sin * (partner_gain * scale)


def _qkvproj(dims, st, h, mods, gain, w_qkv, q_gain, k_gain, with_q):
    rows, d = h.shape
    tm, tn = st.tm(), dims.dkv
    n_q = dims.dq // tn if with_q else 0
    col0 = 0 if with_q else dims.dq // tn
    n_total = n_q + 2
    cos_q, sin_q = _rope_tables(dims, tm, q_gain, HEAD_DIM ** -0.5)
    cos_k, sin_k = _rope_tables(dims, tm, k_gain, 1.0)
    per = dims.seq // tm
    pos_tile = (lambda m, n: (m % per, 0)) if with_q else (lambda m, n: (per, 0))
    return pl.pallas_call(
        functools.partial(_qkv_body, n_q=n_q, tn=tn),
        grid=(rows // tm, n_total),
        in_specs=[
            pl.BlockSpec((tm, d), lambda m, n: (m, 0)),
            pl.BlockSpec((1, d), lambda m, n: (0, 0)),
            _mod_spec(st, tm, 0, d),
            _mod_spec(st, tm, 1, d),
            pl.BlockSpec((d, tn), lambda m, n: (0, col0 + n)),
            pl.BlockSpec((tm, HEAD_DIM), pos_tile),
            pl.BlockSpec((tm, HEAD_DIM), pos_tile),
            pl.BlockSpec((tm, HEAD_DIM), pos_tile),
            pl.BlockSpec((tm, HEAD_DIM), pos_tile),
        ],
        out_specs=pl.BlockSpec((tm, tn), lambda m, n: (m, n)),
        out_shape=jax.ShapeDtypeStruct((rows, n_total * tn), BF16),
        scratch_shapes=[pltpu.VMEM((tm, d), BF16)],
        compiler_params=_params("arbitrary", "arbitrary"),
        name="attn_qkv",
    )(h, gain.reshape(1, d), mods, mods, w_qkv, cos_q, sin_q, cos_k, sin_k)


def _attn_body(sink_ref, q_ref, k_ref, v_ref, kc_ref, vc_ref, o_ref, *, dims):
    group = dims.heads // dims.kv_heads
    blk = ATTN_BLOCK
    band = 3 * blk
    n_ctx = dims.ctx
    head0 = pl.program_id(1) * group
    kc = kc_ref[...]
    vc = vc_ref[...]

    n_blk = dims.seq // blk
    assert n_blk >= 3
    sink = jnp.concatenate([jnp.full((blk, 1), sink_ref[head0 + g], F32) for g in range(group)], axis=0)

    def mask_bias(first_key):
        q_rel = first_key + (lax.broadcasted_iota(I32, (group * blk, 1), 0) & (blk - 1))
        col = lax.broadcasted_iota(I32, (1, band + n_ctx), 1)
        valid = (col >= band) | (jnp.abs(q_rel - col) <= blk)
        return jnp.where(valid, 0.0, MASK_VALUE)

    def block(n, start, bias):
        q0 = n * blk if isinstance(n, int) else pl.multiple_of(n * blk, blk)
        kcat = jnp.concatenate([k_ref[pl.ds(start, band), :], kc], axis=0)
        vcat = jnp.concatenate([v_ref[pl.ds(start, band), :], vc], axis=0)
        q = jnp.concatenate([q_ref[pl.ds(q0, blk), g * HEAD_DIM:(g + 1) * HEAD_DIM] for g in range(group)], axis=0)
        s = lax.dot_general(q, kcat, (((1,), (1,)), ((), ())), preferred_element_type=F32) + bias
        m = jnp.maximum(jnp.max(s, axis=-1, keepdims=True), sink)
        p = jnp.exp(s - m)
        den = jnp.sum(p, axis=-1, keepdims=True) + jnp.exp(sink - m)
        o = jnp.dot(p.astype(BF16), vcat, preferred_element_type=F32) / den
        for g in range(group):
            o_ref[pl.ds(q0, blk), g * HEAD_DIM:(g + 1) * HEAD_DIM] = o[g * blk:(g + 1) * blk].astype(BF16)

    block(0, 0, mask_bias(0))
    mid_bias = mask_bias(blk)

    def middle(n, carry):
        block(n, pl.multiple_of((n - 1) * blk, blk), mid_bias)
        return carry

    lax.fori_loop(1, n_blk - 1, middle, 0, unroll=2)
    block(n_blk - 1, dims.seq - band, mask_bias(2 * blk))


def _attention(dims, qkv, kv_ctx, sink):
    group = dims.heads // dims.kv_heads
    gw = group * HEAD_DIM
    k_col = dims.dq // HEAD_DIM
    v_col = (dims.dq + dims.dkv) // HEAD_DIM
    return pl.pallas_call(
        functools.partial(_attn_body, dims=dims),
        grid=(dims.batch, dims.kv_heads),
        in_specs=[
            pl.BlockSpec(memory_space=pltpu.SMEM),
            pl.BlockSpec((dims.seq, gw), lambda b, h: (b, h)),
            pl.BlockSpec((dims.seq, HEAD_DIM), lambda b, h: (b, k_col + h)),
            pl.BlockSpec((dims.seq, HEAD_DIM), lambda b, h: (b, v_col + h)),
            pl.BlockSpec((dims.ctx, HEAD_DIM), lambda b, h: (b, h)),
            pl.BlockSpec((dims.ctx, HEAD_DIM), lambda b, h: (b, dims.kv_heads + h)),
        ],
        out_specs=pl.BlockSpec((dims.seq, gw), lambda b, h: (b, h)),
        out_shape=jax.ShapeDtypeStruct((dims.batch * dims.seq, dims.dq), BF16),
        compiler_params=_params("arbitrary", "arbitrary"),
        name="attn_core",
    )(sink, qkv, qkv, qkv, kv_ctx, kv_ctx)


def _lane_prefix(x, tri):
    n = x.shape[1]
    off = jnp.zeros((x.shape[0], 1), F32)
    chunks = []
    for ch in range(n // 128):
        xc = x[:, ch * 128:(ch + 1) * 128]
        incl = jnp.dot(xc.astype(BF16), tri, preferred_element_type=F32)
        chunks.append(incl - xc + off)
        off = off + incl[:, 127:128]
    return chunks[0] if len(chunks) == 1 else jnp.concatenate(chunks, axis=1)


def _topk_body(aff_ref, idx_ref, dst_ref, gate_ref, off_ref, cnt_ref, w_scr, a_scr, q_scr, clo_scr, chi_scr, *,
               n_tok, cap, pair0, experts):
    grp = pl.program_id(0)
    a = aff_ref[...]
    bits = pltpu.bitcast(a, I32)
    lane = lax.broadcasted_iota(I32, (experts, n_tok), 1)

    def count(mask):
        return jnp.sum(mask.astype(I32), axis=1, keepdims=True)

    def thr_step(i, thr):
        hi = jnp.left_shift(jnp.int32(1), 29 - 2 * i)
        lo = jnp.left_shift(jnp.int32(1), 28 - 2 * i)
        best = thr
        for cand in (thr | lo, thr | hi, thr | hi | lo):
            best = jnp.where(count(bits >= cand) >= cap, cand, best)
        return best

    first = jnp.int32(1 << 30)
    thr = jnp.where(count(bits >= first) >= cap, first, jnp.zeros((experts, 1), I32))
    thr = lax.fori_loop(0, 15, thr_step, thr)
    above = bits > thr
    tie = bits == thr
    need = cap - count(above)

    top_bit = n_tok.bit_length() - 2

    def tie_try(lim, cands):
        for cand in cands:
            lim = jnp.where(count(tie & (lane < cand)) < need, cand, lim)
        return lim

    def tie_step(i, lim):
        hi = jnp.left_shift(jnp.int32(1), top_bit - odd - 2 * i)
        lo = jnp.left_shift(jnp.int32(1), top_bit - odd - 2 * i - 1)
        return tie_try(lim, (lim | lo, lim | hi, lim | hi | lo))

    odd = (top_bit + 1) % 2
    lim = jnp.zeros((experts, 1), I32)
    if odd:
        lim = tie_try(lim, (lim | (1 << top_bit),))
    lim = lax.fori_loop(0, (top_bit + 1) // 2, tie_step, lim)
    sel = above | (tie & (lane <= lim))
    sel_f = sel.astype(F32)
    sel_i = sel.astype(I32)

    r = lax.broadcasted_iota(I32, (128, 128), 0)
    c = lax.broadcasted_iota(I32, (128, 128), 1)
    tri = (r <= c).astype(F32).astype(BF16)

    level = jnp.zeros((1, n_tok), I32)
    levels = []
    for e in range(experts):
        levels.append(level)
        level = level + sel_i[e:e + 1, :]
    cnt = level
    off = _lane_prefix(cnt.astype(F32), tri).astype(I32) + (pair0 + grp * (experts * cap))
    off_ref[...] = off
    cnt_ref[...] = cnt
    pair_row = jnp.concatenate(levels, axis=0) + off

    n_ch = n_tok // 128
    pad = w_scr.shape[0] // experts
    lane_e = lax.broadcasted_iota(I32, (experts, 128), 1)
    w_scr[...] = jnp.zeros_like(w_scr)
    a_scr[...] = jnp.zeros_like(a_scr)
    q_scr[...] = jnp.zeros_like(q_scr)
    first = jnp.zeros((experts, 1), F32)
    c_lo = jnp.full((experts, 128), cap, I32)
    c_hi = jnp.full((experts, 128), cap, I32)
    for ch in range(n_ch):
        lanes = slice(ch * 128, (ch + 1) * 128)
        incl = jnp.dot(sel_f[:, lanes].astype(BF16), tri, preferred_element_type=F32)
        w_scr[pl.ds(ch, experts, stride=pad), :] = incl
        a_scr[pl.ds(ch, experts, stride=pad), :] = a[:, lanes]
        q_scr[pl.ds(ch, experts, stride=pad), :] = pair_row[:, lanes]
        nxt = first + incl[:, 127:128]
        c_lo = jnp.where(lane_e == ch, first.astype(I32), c_lo)
        c_hi = jnp.where(lane_e == ch, nxt.astype(I32), c_hi)
        first = nxt
    clo_scr[...] = c_lo
    chi_scr[...] = c_hi

    idx_ref[0] = jnp.zeros((cap, 128), I32)
    dst_ref[0] = jnp.zeros((cap, 128), I32)
    gate_ref[0] = jnp.zeros((cap, 128), F32)
    slot = lax.broadcasted_iota(I32, (cap, 1), 0)
    lane = lax.broadcasted_iota(I32, (1, 128), 1)
    zpad = jnp.zeros((128 - pad, 128), F32)

    def rhs(m):
        return jnp.concatenate([m, zpad], axis=0).astype(BF16)

    def per_expert(e, carry):
        base = pl.multiple_of(e * pad, pad)
        wm = w_scr[pl.ds(base, pad), :]
        am = a_scr[pl.ds(base, pad), :]
        qm = q_scr[pl.ds(base, pad), :]
        lo = clo_scr[pl.ds(e, 1), :]
        hi = chi_scr[pl.ds(e, 1), :]
        in_chunk = (lo <= slot) & (slot < hi)
        onehot = in_chunk.astype(F32).astype(BF16)

        def rows_of(m):
            return jnp.dot(onehot, rhs(m), preferred_element_type=F32)

        a1 = am.astype(BF16).astype(F32)
        a2 = (am - a1).astype(BF16).astype(F32)
        a3 = am - a1 - a2
        a_rows = rows_of(a1) + rows_of(a2) + rows_of(a3)
        q_rows = (rows_of(lax.shift_right_logical(qm, 8).astype(F32)) * 256.0
                  + rows_of((qm & 255).astype(F32)))
        w_rows = rows_of(wm)
        chunk_first = jnp.sum(jnp.where(in_chunk, lo, 0), axis=1, keepdims=True)
        chunk = jnp.sum(jnp.where(in_chunk, lane, 0), axis=1, keepdims=True)
        rank = (slot - chunk_first).astype(F32)
        pos = jnp.sum((w_rows <= rank).astype(I32), axis=1, keepdims=True)
        here = lane == pos
        gate = jnp.sum(jnp.where(here, a_rows, 0.0), axis=1, keepdims=True)
        dst = jnp.sum(jnp.where(here, q_rows, 0.0), axis=1, keepdims=True).astype(I32)
        token = chunk * 128 + pos + grp * n_tok
        mine = lane == e
        idx_ref[0] = jnp.where(mine, token, idx_ref[0])
        dst_ref[0] = jnp.where(mine, dst, dst_ref[0])
        gate_ref[0] = jnp.where(mine, gate, gate_ref[0])
        return carry

    lax.fori_loop(0, experts, per_expert, 0)


def _topk(st, aff, cap, pair0):
    e = aff.shape[0]
    n_tok, n_groups = st.seq, st.batch
    pad = max(8, n_tok // 128)
    assert e <= 128 and pad <= 128 and n_groups * e * cap < 2 ** 16 * 256
    tile_spec = pl.BlockSpec((1, cap, 128), lambda g: (g, 0, 0))
    return pl.pallas_call(
        functools.partial(_topk_body, n_tok=n_tok, cap=cap, pair0=pair0, experts=e),
        grid=(n_groups,),
        in_specs=[pl.BlockSpec((e, n_tok), lambda g: (0, g))],
        out_specs=[tile_spec, tile_spec, tile_spec,
                   pl.BlockSpec((1, n_tok), lambda g: (0, g)),
                   pl.BlockSpec((1, n_tok), lambda g: (0, g))],
        out_shape=[jax.ShapeDtypeStruct((n_groups, cap, 128), I32),
                   jax.ShapeDtypeStruct((n_groups, cap, 128), I32),
                   jax.ShapeDtypeStruct((n_groups, cap, 128), F32),
                   jax.ShapeDtypeStruct((1, n_groups * n_tok), I32),
                   jax.ShapeDtypeStruct((1, n_groups * n_tok), I32)],
        scratch_shapes=[pltpu.VMEM((e * pad, 128), F32), pltpu.VMEM((e * pad, 128), F32),
                        pltpu.VMEM((e * pad, 128), I32), pltpu.VMEM((e, 128), I32), pltpu.VMEM((e, 128), I32)],
        compiler_params=_params("arbitrary"),
        name="moe_topk",
    )(aff)


def _expert_body(*refs, n_src, src_slots, slots, rc, n_f, n_n, tf, n_e):
    idx_ref, idx_nxt_ref, dst_ref, dst_prv_ref = refs[0:4]
    srcs = refs[4:4 + n_src]
    gate_ref, wg_ref, wu_ref, wd_ref, pairs_hbm, x_scr, y_scr, mid_scr, gcol_scr, gsem, ssem = refs[4 + n_src:]
    e = pl.program_id(0)
    s = pl.program_id(1)
    nck = x_scr.shape[0] // slots
    half = nck * 128
    tnc = nck // n_n

    def tok(ref, r):
        return ref.at[pl.ds(pl.multiple_of(r * nck, nck), nck), :]

    def row_loop(lo, n, start_row):
        def eight(i, carry):
            for k in range(8):
                start_row(lo + i * 8 + k)
            return carry
        lax.fori_loop(0, n // 8, eight, 0)

    def gather_row(ref, r, slot):
        src = next(src for (s0, s1), src in zip(src_slots, srcs) if s0 <= slot < s1)
        pltpu.make_async_copy(tok(src, ref[0, 0, r]), tok(x_scr, r), gsem).start()

    def scatter_row(ref, r):
        pltpu.make_async_copy(tok(y_scr, r), tok(pairs_hbm, ref[0, 0, r]), ssem).start()

    n_chunks = slots // rc
    dma_chunks = max(n_chunks - 1, 1)

    def share(total, ci):
        base, extra = divmod(total, dma_chunks)
        if ci >= dma_chunks:
            return total, 0
        return ci * base + min(ci, extra), base + (1 if ci < extra else 0)

    def wait_gather():
        pltpu.make_async_copy(srcs[0].at[pl.ds(0, slots * nck), :], x_scr, gsem).wait()

    def wait_scatter():
        pltpu.make_async_copy(y_scr, pairs_hbm.at[pl.ds(0, slots * nck), :], ssem).wait()

    def x_rows(r0):
        parts = [_unpack_bf16_pair(x_scr[pl.ds(r0 * nck + c, rc, stride=nck), :]) for c in range(nck)]
        return (jnp.concatenate([p[0] for p in parts], axis=1), jnp.concatenate([p[1] for p in parts], axis=1))

    @pl.when((e == 0) & (s == 0))
    def _():
        y_scr[...] = jnp.zeros_like(y_scr)
        for (s0, s1), src in zip(src_slots, srcs):
            row_loop(s0, s1 - s0, lambda r, src=src: pltpu.make_async_copy(
                tok(src, idx_ref[0, 0, r]), tok(x_scr, r), gsem).start())

    @pl.when(s == 0)
    def _():
        wait_gather()
        eye = lax.broadcasted_iota(I32, (128, 128), 0) == lax.broadcasted_iota(I32, (128, 128), 1)
        for c in range(slots // 128):
            row = gate_ref[0, c:c + 1, :]
            gcol_scr[c * 128:(c + 1) * 128, :] = jnp.sum(jnp.where(eye, row, 0.0), axis=1, keepdims=True)

    @pl.when(s < n_f)
    def _():
        wg = wg_ref[0, 0].astype(BF16)
        wu = wu_ref[0, 0].astype(BF16)
        for r0 in range(0, slots, rc):
            lo, hi = x_rows(r0)
            a = (jnp.dot(lo, wg[:half], preferred_element_type=F32)
                 + jnp.dot(hi, wg[half:], preferred_element_type=F32))
            u = (jnp.dot(lo, wu[:half], preferred_element_type=F32)
                 + jnp.dot(hi, wu[half:], preferred_element_type=F32))
            mid_scr[s, r0:r0 + rc, :] = (a * jax.nn.sigmoid(a) * u).astype(BF16)
            off, cnt = share(slots // n_f, r0 // rc)
            for k in range(cnt):
                scatter_row(dst_prv_ref, s * (slots // n_f) + off + k)

    @pl.when(s == n_f)
    def _():
        wait_scatter()

    for j in range(n_n):
        @pl.when(s == n_f + j)
        def _():
            wd = wd_ref[0, 0].astype(BF16)
            for r0 in range(0, slots, rc):
                y = jnp.dot(mid_scr[0, r0:r0 + rc, :], wd[0:tf], preferred_element_type=F32)
                for f in range(1, n_f):
                    y += jnp.dot(mid_scr[f, r0:r0 + rc, :], wd[f * tf:(f + 1) * tf], preferred_element_type=F32)
                y = y * gcol_scr[r0:r0 + rc, :]
                tnw = tnc * 128
                packed = _pack_bf16_pair(y[:, :tnw], y[:, tnw:])
                for c in range(tnc):
                    y_scr[pl.ds(r0 * nck + j * tnc + c, rc, stride=nck), :] = packed[:, c * 128:(c + 1) * 128]
                g_steps = max(n_n - 1, 1)
                step_rows = -(-slots // g_steps)
                lo = min(j * step_rows, slots)
                off, cnt = share(min(step_rows, slots - lo), r0 // rc)
                for k in range(cnt):
                    slot = lo + off + k
                    gather_row(idx_nxt_ref, slot, slot)

    @pl.when((e == n_e - 1) & (s == n_f + n_n - 1))
    def _():
        wait_gather()
        row_loop(0, slots, lambda r: scatter_row(dst_ref, r))
        wait_scatter()


def _experts(idx, dst, gate, sources, src_slots, w_gate, w_up, w_down, layer, n_pairs):
    n_e, slots = idx.shape
    d, fdim = w_gate.shape[2], w_gate.shape[3]
    tf, tn = 256, EXPERT_DOWN_TILE
    n_f, n_n = fdim // tf, d // tn
    rc = slots // 4
    nck = d // 256
    assert slots % 128 == 0 and rc % 16 == 0 and nck % n_n == 0 and slots % n_f == 0 and slots % n_n == 0
    assert all(s0 % 8 == 0 and s1 % 8 == 0 for s0, s1 in src_slots)
    idx3 = idx.reshape(n_e, 1, slots)
    dst3 = dst.reshape(n_e, 1, slots)
    smem = lambda f: pl.BlockSpec((1, 1, slots), f, memory_space=pltpu.SMEM)
    up_chunk = lambda i, s: (layer, i, 0, jnp.minimum(s, n_f - 1))
    return pl.pallas_call(
        functools.partial(_expert_body, n_src=len(sources), src_slots=src_slots, slots=slots, rc=rc,
                          n_f=n_f, n_n=n_n, tf=tf, n_e=n_e),
        grid=(n_e, n_f + n_n),
        in_specs=[
            smem(lambda i, s: (i, 0, 0)),
            smem(lambda i, s: (jnp.minimum(i + 1, n_e - 1), 0, 0)),
            smem(lambda i, s: (i, 0, 0)),
            smem(lambda i, s: (jnp.maximum(i - 1, 0), 0, 0)),
        ] + [pl.BlockSpec(memory_space=pl.ANY)] * len(sources) + [
            pl.BlockSpec((1, slots // 128, 128), lambda i, s: (i, 0, 0)),
            pl.BlockSpec((1, 1, d, tf), up_chunk),
            pl.BlockSpec((1, 1, d, tf), up_chunk),
            pl.BlockSpec((1, 1, fdim, tn), lambda i, s: (layer, i, 0, jnp.maximum(s - n_f, 0))),
        ],
        out_specs=pl.BlockSpec(memory_space=pl.ANY),
        out_shape=jax.ShapeDtypeStruct((n_pairs * nck, 128), I32),
        scratch_shapes=[
            pltpu.VMEM((slots * nck, 128), I32),
            pltpu.VMEM((slots * nck, 128), I32),
            pltpu.VMEM((n_f, slots, tf), BF16),
            pltpu.VMEM((slots, 1), F32),
            pltpu.SemaphoreType.DMA(()),
            pltpu.SemaphoreType.DMA(()),
        ],
        compiler_params=_params("arbitrary", "arbitrary"),
        name="moe_experts",
    )(idx3, idx3, dst3, dst3, *sources, gate.reshape(n_e, slots // 128, 128), w_gate, w_up, w_down)


def _combine_body(lo_ref, hi_ref, off_ref, cnt_ref, p_ref, h_ref, g2_ref, o_ref, acc_lo, acc_hi, *,
                  tt, nck, tnw, pair0, pairs_per_group):
    base = pair0 + pl.program_id(0) * pairs_per_group
    i = pl.program_id(1)
    lane = lax.broadcasted_iota(I32, (1, PAIR_BLOCK), 1)
    eye = lax.broadcasted_iota(I32, (128, 128), 0) == lax.broadcasted_iota(I32, (128, 128), 1)

    def column(row):
        cols = [jnp.sum(jnp.where(eye, row[:, c0:c0 + 128], 0), axis=1, keepdims=True) for c0 in range(0, tt, 128)]
        return cols[0] if len(cols) == 1 else jnp.concatenate(cols, axis=0)

    first = column(off_ref[...]) - base
    last = first + column(cnt_ref[...])
    shift = PAIR_BLOCK.bit_length() - 1
    k0 = lax.shift_right_logical(lo_ref[0, 0, i] - base, shift)
    k1 = lax.shift_right_logical(hi_ref[0, 0, i] - base + (PAIR_BLOCK - 1), shift)
    acc_lo[...] = jnp.zeros_like(acc_lo)
    acc_hi[...] = jnp.zeros_like(acc_hi)

    def step(k, carry):
        p0 = pl.multiple_of(k * PAIR_BLOCK, PAIR_BLOCK)
        parts = [_unpack_bf16_pair(p_ref[pl.ds(p0 * nck + c, PAIR_BLOCK, stride=nck), :]) for c in range(nck)]
        lo = jnp.concatenate([p[0] for p in parts], axis=1)
        hi = jnp.concatenate([p[1] for p in parts], axis=1)
        pr = p0 + lane
        seg = ((pr >= first) & (pr < last)).astype(F32).astype(BF16)
        acc_lo[...] += jnp.dot(seg, lo, preferred_element_type=F32)
        acc_hi[...] += jnp.dot(seg, hi, preferred_element_type=F32)
        return carry

    lax.fori_loop(k0, k1, step, 0)
    for j in range(nck * 128 // tnw):
        words = slice(j * tnw, (j + 1) * tnw)
        c_lo = slice(2 * j * tnw, (2 * j + 1) * tnw)
        c_hi = slice((2 * j + 1) * tnw, (2 * j + 2) * tnw)
        o_ref[:, c_lo] = h_ref[:, c_lo] + g2_ref[0, :, c_lo] * acc_lo[:, words]
        o_ref[:, c_hi] = h_ref[:, c_hi] + g2_ref[0, :, c_hi] * acc_hi[:, words]


def _combine(st, off, cnt, pairs, h, mods, pair0, pairs_per_group):
    rows, d = h.shape
    n_tok = st.seq
    tt = min(256, n_tok)
    n_tiles = n_tok // tt
    nck = d // 256
    assert pairs_per_group % PAIR_BLOCK == 0 and pair0 % pairs_per_group == 0
    off_t = off.reshape(st.batch, 1, n_tiles, tt)
    cnt_t = cnt.reshape(st.batch, 1, n_tiles, tt)
    tile_lo = off_t[..., 0]
    tile_hi = off_t[..., tt - 1] + cnt_t[..., tt - 1]
    row = st.ada_row(n_tok)
    smem = pl.BlockSpec((1, 1, n_tiles), lambda b, i: (b, 0, 0), memory_space=pltpu.SMEM)
    tile = lambda b, i: (b * n_tiles + i, 0)
    return pl.pallas_call(
        functools.partial(_combine_body, tt=tt, nck=nck, tnw=EXPERT_DOWN_TILE // 2, pair0=pair0,
                          pairs_per_group=pairs_per_group),
        grid=(st.batch, n_tiles),
        in_specs=[
            smem, smem,
            pl.BlockSpec((1, tt), lambda b, i: (0, b * n_tiles + i)),
            pl.BlockSpec((1, tt), lambda b, i: (0, b * n_tiles + i)),
            pl.BlockSpec((pairs_per_group * nck, 128), lambda b, i: (pair0 // pairs_per_group + b, 0)),
            pl.BlockSpec((tt, d), tile),
            pl.BlockSpec((1, 1, d), lambda b, i: (row(b) * N_ADA + 5, 0, 0)),
        ],
        out_specs=pl.BlockSpec((tt, d), tile),
        out_shape=jax.ShapeDtypeStruct((rows, d), F32),
        scratch_shapes=[pltpu.VMEM((tt, d // 2), F32)] * 2,
        compiler_params=_params("arbitrary", "arbitrary"),
        name="moe_combine",
    )(tile_lo, tile_hi, off, cnt, pairs, h, mods)


def _moe(streams, mixed, mods, w_gate, w_up, w_down, layer):
    n_e = mixed[0][2].shape[0]
    routed = []
    pair0 = 0
    slot0 = 0
    src_slots = []
    for st, (h, h2, aff) in zip(streams, mixed):
        cap = CAP_FACTOR * st.seq // n_e
        idx, dst, gate, off, cnt = _topk(st, aff, cap, pair0)
        routed.append((h2, idx, dst, gate, off, cnt, pair0, n_e * cap))
        src_slots.append((slot0, slot0 + st.batch * cap))
        pair0 += st.batch * n_e * cap
        slot0 += st.batch * cap
    flat = lambda t: t[:, :, :n_e].transpose(2, 0, 1).reshape(n_e, -1)
    idx = jnp.concatenate([flat(r[1]) for r in routed], axis=1)
    dst = jnp.concatenate([flat(r[2]) for r in routed], axis=1)
    gate = jnp.concatenate([flat(r[3]) for r in routed], axis=1)
    pairs = _experts(idx, dst, gate, [r[0] for r in routed], tuple(src_slots), w_gate, w_up, w_down, layer, pair0)
    return [_combine(st, r[4], r[5], pairs, m[0], mods, r[6], r[7]) for st, m, r in zip(streams, mixed, routed)]


def _forward(dims, x, c, ctx, c_ctx, ada_w, ada_b, norm_mix_g, norm_ffn_g, conv_w_in, conv_w, conv_w_out,
             attn_w_qkv, attn_q_gain, attn_k_gain, attn_sink, attn_w_o, router_w,
             expert_w_gate, expert_w_up, expert_w_down):
    d = dims.d
    lat = Stream(dims.batch, dims.seq)
    con = Stream(dims.batch, dims.ctx, shared_row=dims.batch)
    cvec = jnp.zeros((ADA_ROWS, d), F32).at[:dims.batch].set(c).at[dims.batch].set(c_ctx)
    mods = _adaln(cvec, ada_w, ada_b)
    mods = mods.reshape(mods.shape[0], ADA_ROWS * N_ADA, 1, d)
    hs = [x.reshape(lat.rows, d), ctx.reshape(con.rows, d)]

    w_in, w_out = conv_w_in[0].astype(BF16), conv_w_out[0].astype(BF16)
    mixed = []
    for st, h in zip((lat, con), hs):
        gb_u = _inproj(st, h, mods[0], norm_mix_g[0], w_in)
        mixed.append(_mixout(st, gb_u, conv_w[0], w_out, h, mods[0], norm_ffn_g[0], router_w[0]))
    h_lat, h_ctx = _moe((lat, con), mixed, mods[0], expert_w_gate, expert_w_up, expert_w_down, 0)

    n_qk = dims.heads + dims.kv_heads
    w_qk = attn_w_qkv[0, :, :n_qk * HEAD_DIM].reshape(d, n_qk, 2, 2, HEAD_DIM // 4)
    w_qk = w_qk.transpose(0, 1, 3, 2, 4).reshape(d, n_qk * HEAD_DIM)
    w_qkv = jnp.concatenate([w_qk, attn_w_qkv[0, :, n_qk * HEAD_DIM:]], axis=1).astype(BF16)
    w_o = attn_w_o[0].astype(BF16)
    qkv = _qkvproj(dims, lat, h_lat, mods[1], norm_mix_g[1], w_qkv, attn_q_gain[0], attn_k_gain[0], True)
    kv_ctx = _qkvproj(dims, con, h_ctx, mods[1], norm_mix_g[1], w_qkv, attn_q_gain[0], attn_k_gain[0], False)
    o = _attention(dims, qkv, kv_ctx, attn_sink[0])
    mixed = [_mixout(lat, (o,), None, w_o, h_lat, mods[1], norm_ffn_g[1], router_w[1])]
    (h_lat,) = _moe((lat,), mixed, mods[1], expert_w_gate, expert_w_up, expert_w_down, 1)
    return h_lat.reshape(dims.batch, dims.seq, d)


def kernel(x, c, ctx, c_ctx, ada_w, ada_b, norm_mix_g, norm_ffn_g, conv_w_in, conv_w, conv_w_out, attn_w_qkv, attn_q_gain, attn_k_gain, attn_sink, attn_w_o, router_w, expert_w_gate, expert_w_up, expert_w_down):
    batch, seq, d = x.shape
    dims = Dims(d=d, batch=batch, seq=seq, grid_w=GRID_W, ctx=ctx.shape[1], heads=d // HEAD_DIM,
                kv_heads=(attn_w_qkv.shape[2] // HEAD_DIM - d // HEAD_DIM) // 2,
                experts=router_w.shape[2], d_expert=expert_w_gate.shape[3])
    return _forward(dims, x, c, ctx, c_ctx, ada_w, ada_b, norm_mix_g, norm_ffn_g, conv_w_in, conv_w, conv_w_out,
                    attn_w_qkv, attn_q_gain, attn_k_gain, attn_sink, attn_w_o, router_w,
                    expert_w_gate, expert_w_up, expert_w_down)
```

```python
import dataclasses
import functools

import jax
import jax.numpy as jnp
from jax import lax
from jax.experimental import pallas as pl
from jax.experimental.pallas import tpu as pltpu

F32 = jnp.float32
BF16 = jnp.bfloat16
I32 = jnp.int32

NORM_EPS = 1e-6
MASK_VALUE = -1e30
ROPE_THETA = 10000.0
GRID_W = 64
CAP_FACTOR = 2
HEAD_DIM = 128
ATTN_BLOCK = 128
N_ADA = 6
ADA_ROWS = 16
HI_MASK = -65536
PAIR_BLOCK = 256
EXPERT_DOWN_TILE = 512
VMEM_LIMIT = 58 * 1024 * 1024


@dataclasses.dataclass(frozen=True)
class Dims:
    d: int
    batch: int
    seq: int
    grid_w: int
    ctx: int
    heads: int
    kv_heads: int
    experts: int
    d_expert: int

    @property
    def dq(self):
        return self.heads * HEAD_DIM

    @property
    def dkv(self):
        return self.kv_heads * HEAD_DIM


@dataclasses.dataclass(frozen=True)
class Stream:
    batch: int
    seq: int
    shared_row: int = -1

    @property
    def rows(self):
        return self.batch * self.seq

    def tm(self, cap=1024):
        unit = self.rows if self.shared_row >= 0 else self.seq
        t = cap
        while unit % t:
            t //= 2
        return t

    def ada_row(self, tm):
        if self.shared_row >= 0:
            return lambda m: self.shared_row
        per = self.seq // tm
        assert per >= 1
        return lambda m: m // per


def _params(*sem):
    return pltpu.CompilerParams(dimension_semantics=sem, vmem_limit_bytes=VMEM_LIMIT)


def _modulate(h, gain, shift, scale):
    ms = jnp.mean(h * h, axis=-1, keepdims=True)
    xn = h * lax.rsqrt(ms + NORM_EPS)
    return (xn * gain) * (1.0 + scale) + shift


def _mod_spec(st, tm, which, d, width=None):
    row = st.ada_row(tm)
    if width is None:
        return pl.BlockSpec((1, 1, d), lambda m, n: (row(m) * N_ADA + which, 0, 0))
    return pl.BlockSpec((1, 1, width), lambda m, n: (row(m) * N_ADA + which, 0, n))


def _pack_bf16_pair(lo, hi):
    lo_bits = pltpu.bitcast(lo.astype(BF16).astype(F32), I32)
    hi_bits = pltpu.bitcast(hi.astype(BF16).astype(F32), I32)
    return (hi_bits & HI_MASK) | lax.shift_right_logical(lo_bits, 16)


def _unpack_bf16_pair(w):
    lo = pltpu.bitcast(lax.shift_left(w, 16), F32).astype(BF16)
    hi = pltpu.bitcast(w & HI_MASK, F32).astype(BF16)
    return lo, hi


def _adaln_body(c_ref, w_ref, b_ref, o_ref):
    c = c_ref[...]
    s = c * jax.nn.sigmoid(c)
    o_ref[0] = jnp.dot(s.astype(BF16), w_ref[0].astype(BF16), preferred_element_type=F32) + b_ref[0]


def _adaln(cvec, ada_w, ada_b):
    depth, d, n = ada_w.shape
    tn = min(1024, n)
    return pl.pallas_call(
        _adaln_body,
        grid=(depth, n // tn),
        in_specs=[
            pl.BlockSpec((ADA_ROWS, d), lambda l, j: (0, 0)),
            pl.BlockSpec((1, d, tn), lambda l, j: (l, 0, j)),
            pl.BlockSpec((1, 1, tn), lambda l, j: (l, 0, j)),
        ],
        out_specs=pl.BlockSpec((1, ADA_ROWS, tn), lambda l, j: (l, 0, j)),
        out_shape=jax.ShapeDtypeStruct((depth, ADA_ROWS, n), F32),
        compiler_params=_params("arbitrary", "arbitrary"),
        name="adaln",
    )(cvec, ada_w, ada_b.reshape(depth, 1, n))


def _inproj_body(h_ref, g_ref, sh_ref, sc_ref, wb_ref, wc_ref, wx_ref, gb_ref, u_ref, a_scr):
    def project(rows):
        a = a_scr[rows, :]
        gb = jnp.dot(a, wb_ref[...], preferred_element_type=F32)
        gc = jnp.dot(a, wc_ref[...], preferred_element_type=F32)
        xv = jnp.dot(a, wx_ref[...], preferred_element_type=F32)
        gb_ref[rows, :] = gb.astype(BF16)
        u_ref[rows, :] = (gc * xv).astype(BF16)

    tm = a_scr.shape[0]

    @pl.when(pl.program_id(1) == 0)
    def _():
        for r0 in range(0, tm, tm // 4):
            rows = slice(r0, r0 + tm // 4)
            a_scr[rows, :] = _modulate(h_ref[rows, :], g_ref[...], sh_ref[0], sc_ref[0]).astype(BF16)
            project(rows)

    @pl.when(pl.program_id(1) > 0)
    def _():
        project(slice(0, tm))


def _inproj(st, h, mods, gain, w_in):
    rows, d = h.shape
    tm, tn = st.tm(), 512
    nt = d // tn
    return pl.pallas_call(
        _inproj_body,
        grid=(rows // tm, nt),
        in_specs=[
            pl.BlockSpec((tm, d), lambda m, n: (m, 0)),
            pl.BlockSpec((1, d), lambda m, n: (0, 0)),
            _mod_spec(st, tm, 0, d),
            _mod_spec(st, tm, 1, d),
            pl.BlockSpec((d, tn), lambda m, n: (0, n)),
            pl.BlockSpec((d, tn), lambda m, n: (0, nt + n)),
            pl.BlockSpec((d, tn), lambda m, n: (0, 2 * nt + n)),
        ],
        out_specs=[pl.BlockSpec((tm, tn), lambda m, n: (m, n))] * 2,
        out_shape=[jax.ShapeDtypeStruct((rows, d), BF16)] * 2,
        scratch_shapes=[pltpu.VMEM((tm, d), BF16)],
        compiler_params=_params("arbitrary", "arbitrary"),
        name="conv_inproj",
    )(h, gain.reshape(1, d), mods, mods, w_in, w_in, w_in)


def _mixout_body(*refs, conv, seq, tm, experts, cw):
    if conv:
        gb_ref, u_ref, up_ref, un_ref, cw_ref = refs[:5]
        refs = refs[5:]
        v_scr = refs[-1]
        halo = up_ref.shape[0]
        row = lax.broadcasted_iota(I32, (tm, 1), 0)
        pos = (pl.program_id(0) * tm + row) & (seq - 1)
        for c0 in range(0, u_ref.shape[1], cw):
            cols = slice(c0, c0 + cw)
            u = u_ref[:, cols].astype(F32)
            u_dn = pltpu.roll(u, 1, 0)
            u_dn = jnp.where(row == 0, up_ref[halo - 1:halo, cols].astype(F32), u_dn)
            u_dn = jnp.where(pos == 0, 0.0, u_dn)
            u_up = pltpu.roll(u, tm - 1, 0)
            u_up = jnp.where(row == tm - 1, un_ref[0:1, cols].astype(F32), u_up)
            u_up = jnp.where(pos == seq - 1, 0.0, u_up)
            y = cw_ref[0:1, cols] * u_dn + cw_ref[1:2, cols] * u + cw_ref[2:3, cols] * u_up
            v_scr[:, cols] = (gb_ref[:, cols].astype(F32) * y).astype(BF16)
        v_ref = v_scr
    else:
        v_ref = refs[0]
        refs = refs[1:]
    w_ref, h_ref, g1_ref, gf_ref, sh_ref, sc_ref, wr_ref, hn_ref, h2_ref, aff_ref = refs[:10]
    half = h_ref.shape[1] // 2
    nck = half // 128
    th = tm // 2
    for r0 in (0, th):
        rows = slice(r0, r0 + th)
        out = jnp.dot(v_ref[rows, :], w_ref[...], preferred_element_type=F32)
        hn = h_ref[rows, :] + g1_ref[0] * out
        hn_ref[rows, :] = hn
        a = _modulate(hn, gf_ref[...], sh_ref[0], sc_ref[0])
        packed = _pack_bf16_pair(a[:, :half], a[:, half:])
        for j in range(nck):
            h2_ref[pl.ds(r0 * nck + j, th, stride=nck), :] = packed[:, j * 128:(j + 1) * 128]
        logits = jnp.dot(a.astype(BF16), wr_ref[...], preferred_element_type=F32)
        lt = logits.T[0:experts, :]
        ex = jnp.exp(lt - jnp.max(lt, axis=0, keepdims=True))
        aff_ref[:, rows] = ex / jnp.sum(ex, axis=0, keepdims=True)


def _mixout(st, v_inputs, conv_w, w_out, h, mods, gain_ffn, w_router):
    rows, d = h.shape
    experts = w_router.shape[1]
    conv = conv_w is not None
    tm, halo = st.tm(512), 16
    nck = d // 256
    assert st.seq & (st.seq - 1) == 0 and tm % 256 == 0
    row = st.ada_row(tm)
    mod = lambda which: pl.BlockSpec((1, 1, d), lambda m: (row(m) * N_ADA + which, 0, 0))
    tile = pl.BlockSpec((tm, d), lambda m: (m, 0))
    const = lambda shape: pl.BlockSpec(shape, lambda m: (0,) * len(shape))
    wr = jnp.zeros((d, 128), BF16).at[:, :experts].set(w_router.astype(BF16))
    if conv:
        gb, u = v_inputs
        per, last = tm // halo, rows // halo - 1
        v_specs = [tile, tile,
                   pl.BlockSpec((halo, d), lambda m: (jnp.maximum(m * per - 1, 0), 0)),
                   pl.BlockSpec((halo, d), lambda m: (jnp.minimum((m + 1) * per, last), 0)),
                   const((3, d))]
        v_args = (gb, u, u, u, conv_w)
        scratch = [pltpu.VMEM((tm, d), BF16)]
    else:
        v_specs, v_args, scratch = [tile], tuple(v_inputs), []
    return pl.pallas_call(
        functools.partial(_mixout_body, conv=conv, seq=st.seq, tm=tm, experts=experts, cw=512),
        grid=(rows // tm,),
        in_specs=v_specs + [
            pl.BlockSpec((d, d), lambda m: (0, 0), pipeline_mode=pl.Buffered(1)),
            tile, mod(2), const((1, d)), mod(3), mod(4), const((d, 128)),
        ],
        out_specs=[
            tile,
            pl.BlockSpec((tm * nck, 128), lambda m: (m, 0)),
            pl.BlockSpec((experts, tm), lambda m: (0, m)),
        ],
        out_shape=[
            jax.ShapeDtypeStruct((rows, d), F32),
            jax.ShapeDtypeStruct((rows * nck, 128), I32),
            jax.ShapeDtypeStruct((experts, rows), F32),
        ],
        scratch_shapes=scratch,
        compiler_params=_params("arbitrary"),
        name="mix_out_router",
    )(*v_args, w_out, h, mods, gain_ffn.reshape(1, d), mods, mods, wr)


def _qkv_body(h_ref, g_ref, sh_ref, sc_ref, w_ref, cq_ref, sq_ref, ck_ref, sk_ref, o_ref, a_scr, *, n_q, tn):
    n = pl.program_id(1)
    tm = a_scr.shape[0]

    def tile(rows, kind):
        acc = jnp.dot(a_scr[rows, :], w_ref[...], preferred_element_type=F32)
        if kind == "v":
            o_ref[rows, :] = acc.astype(BF16)
            return
        cos_ref, sin_ref = (cq_ref, sq_ref) if kind == "q" else (ck_ref, sk_ref)
        cos = cos_ref[rows, :]
        sin = sin_ref[rows, :]
        for hd in range(tn // HEAD_DIM):
            lanes = slice(hd * HEAD_DIM, (hd + 1) * HEAD_DIM)
            x = acc[:, lanes]
            r = lax.rsqrt(jnp.mean(x * x, axis=-1, keepdims=True) + NORM_EPS)
            rot = pltpu.roll(x, HEAD_DIM // 2, 1)
            o_ref[rows, lanes] = ((x * cos + rot * sin) * r).astype(BF16)

    quarters = [slice(r0, r0 + tm // 4) for r0 in range(0, tm, tm // 4)]

    @pl.when(n == 0)
    def _():
        for rows in quarters:
            a_scr[rows, :] = _modulate(h_ref[rows, :], g_ref[...], sh_ref[0], sc_ref[0]).astype(BF16)
            tile(rows, "q" if n_q > 0 else "k")

    @pl.when((n > 0) & (n < n_q))
    def _():
        for rows in quarters:
            tile(rows, "q")

    @pl.when((n > 0) & (n == n_q))
    def _():
        for rows in quarters:
            tile(rows, "k")

    @pl.when(n > n_q)
    def _():
        tile(slice(0, tm), "v")


def _rope_tables(dims, tm, gain, scale):
    quarter = HEAD_DIM // 4
    inv_freq = ROPE_THETA ** (-jnp.arange(quarter, dtype=F32) * 2.0 / (HEAD_DIM // 2))
    t = jnp.arange(dims.seq)
    ang_row = (t // dims.grid_w).astype(F32)[:, None] * inv_freq
    ang_col = (t % dims.grid_w).astype(F32)[:, None] * inv_freq
    cos = jnp.concatenate([jnp.cos(ang_row), jnp.cos(ang_col)] * 2, axis=-1)
    sin = jnp.concatenate([-jnp.sin(ang_row), -jnp.sin(ang_col), jnp.sin(ang_row), jnp.sin(ang_col)], axis=-1)
    cos = jnp.concatenate([cos, jnp.ones((tm, HEAD_DIM), F32)], axis=0)
    sin = jnp.concatenate([sin, jnp.zeros((tm, HEAD_DIM), F32)], axis=0)
    g = gain.reshape(2, 2, quarter).transpose(1, 0, 2).reshape(HEAD_DIM)
    partner_gain = jnp.concatenate([g[HEAD_DIM // 2:], g[:HEAD_DIM // 2]])
    return cos * (g * scale), sin * (partner_gain * scale)


def _qkvproj(dims, st, h, mods, gain, w_qkv, q_gain, k_gain, with_q):
    rows, d = h.shape
    tm, tn = st.tm(), dims.dkv
    n_q = dims.dq // tn if with_q else 0
    col0 = 0 if with_q else dims.dq // tn
    n_total = n_q + 2
    cos_q, sin_q = _rope_tables(dims, tm, q_gain, HEAD_DIM ** -0.5)
    cos_k, sin_k = _rope_tables(dims, tm, k_gain, 1.0)
    per = dims.seq // tm
    pos_tile = (lambda m, n: (m % per, 0)) if with_q else (lambda m, n: (per, 0))
    return pl.pallas_call(
        functools.partial(_qkv_body, n_q=n_q, tn=tn),
        grid=(rows // tm, n_total),
        in_specs=[
            pl.BlockSpec((tm, d), lambda m, n: (m, 0)),
            pl.BlockSpec((1, d), lambda m, n: (0, 0)),
            _mod_spec(st, tm, 0, d),
            _mod_spec(st, tm, 1, d),
            pl.BlockSpec((d, tn), lambda m, n: (0, col0 + n)),
            pl.BlockSpec((tm, HEAD_DIM), pos_tile),
            pl.BlockSpec((tm, HEAD_DIM), pos_tile),
            pl.BlockSpec((tm, HEAD_DIM), pos_tile),
            pl.BlockSpec((tm, HEAD_DIM), pos_tile),
        ],
        out_specs=pl.BlockSpec((tm, tn), lambda m, n: (m, n)),
        out_shape=jax.ShapeDtypeStruct((rows, n_total * tn), BF16),
        scratch_shapes=[pltpu.VMEM((tm, d), BF16)],
        compiler_params=_params("arbitrary", "arbitrary"),
        name="attn_qkv",
    )(h, gain.reshape(1, d), mods, mods, w_qkv, cos_q, sin_q, cos_k, sin_k)


def _attn_body(sink_ref, q_ref, k_ref, v_ref, kc_ref, vc_ref, o_ref, *, dims):
    group = dims.heads // dims.kv_heads
    blk = ATTN_BLOCK
    band = 3 * blk
    n_ctx = dims.ctx
    head0 = pl.program_id(1) * group
    kc = kc_ref[...]
    vc = vc_ref[...]

    n_blk = dims.seq // blk
    assert n_blk >= 3
    sink = jnp.concatenate([jnp.full((blk, 1), sink_ref[head0 + g], F32) for g in range(group)], axis=0)

    def mask_bias(first_key):
        q_rel = first_key + (lax.broadcasted_iota(I32, (group * blk, 1), 0) & (blk - 1))
        col = lax.broadcasted_iota(I32, (1, band + n_ctx), 1)
        valid = (col >= band) | (jnp.abs(q_rel - col) <= blk)
        return jnp.where(valid, 0.0, MASK_VALUE)

    def block(n, start, bias):
        q0 = n * blk if isinstance(n, int) else pl.multiple_of(n * blk, blk)
        kcat = jnp.concatenate([k_ref[pl.ds(start, band), :], kc], axis=0)
        vcat = jnp.concatenate([v_ref[pl.ds(start, band), :], vc], axis=0)
        q = jnp.concatenate([q_ref[pl.ds(q0, blk), g * HEAD_DIM:(g + 1) * HEAD_DIM] for g in range(group)], axis=0)
        s = lax.dot_general(q, kcat, (((1,), (1,)), ((), ())), preferred_element_type=F32) + bias
        m = jnp.maximum(jnp.max(s, axis=-1, keepdims=True), sink)
        p = jnp.exp(s - m)
        den = jnp.sum(p, axis=-1, keepdims=True) + jnp.exp(sink - m)
        o = jnp.dot(p.astype(BF16), vcat, preferred_element_type=F32) / den
        for g in range(group):
            o_ref[pl.ds(q0, blk), g * HEAD_DIM:(g + 1) * HEAD_DIM] = o[g * blk:(g + 1) * blk].astype(BF16)

    block(0, 0, mask_bias(0))
    mid_bias = mask_bias(blk)

    def middle(n, carry):
        block(n, pl.multiple_of((n - 1) * blk, blk), mid_bias)
        return carry

    lax.fori_loop(1, n_blk - 1, middle, 0, unroll=2)
    block(n_blk - 1, dims.seq - band, mask_bias(2 * blk))


def _attention(dims, qkv, kv_ctx, sink):
    group = dims.heads // dims.kv_heads
    gw = group * HEAD_DIM
    k_col = dims.dq // HEAD_DIM
    v_col = (dims.dq + dims.dkv) // HEAD_DIM
    return pl.pallas_call(
        functools.partial(_attn_body, dims=dims),
        grid=(dims.batch, dims.kv_heads),
        in_specs=[
            pl.BlockSpec(memory_space=pltpu.SMEM),
            pl.BlockSpec((dims.seq, gw), lambda b, h: (b, h)),
            pl.BlockSpec((dims.seq, HEAD_DIM), lambda b, h: (b, k_col + h)),
            pl.BlockSpec((dims.seq, HEAD_DIM), lambda b, h: (b, v_col + h)),
            pl.BlockSpec((dims.ctx, HEAD_DIM), lambda b, h: (b, h)),
            pl.BlockSpec((dims.ctx, HEAD_DIM), lambda b, h: (b, dims.kv_heads + h)),
        ],
        out_specs=pl.BlockSpec((dims.seq, gw), lambda b, h: (b, h)),
        out_shape=jax.ShapeDtypeStruct((dims.batch * dims.seq, dims.dq), BF16),
        compiler_params=_params("arbitrary", "arbitrary"),
        name="attn_core",
    )(sink, qkv, qkv, qkv, kv_ctx, kv_ctx)


def _lane_prefix(x, tri):
    n = x.shape[1]
    off = jnp.zeros((x.shape[0], 1), F32)
    chunks = []
    for ch in range(n // 128):
        xc = x[:, ch * 128:(ch + 1) * 128]
        incl = jnp.dot(xc.astype(BF16), tri, preferred_element_type=F32)
        chunks.append(incl - xc + off)
        off = off + incl[:, 127:128]
    return chunks[0] if len(chunks) == 1 else jnp.concatenate(chunks, axis=1)


def _topk_body(aff_ref, idx_ref, dst_ref, gate_ref, off_ref, cnt_ref, w_scr, a_scr, q_scr, clo_scr, chi_scr, *,
               n_tok, cap, pair0, experts):
    grp = pl.program_id(0)
    a = aff_ref[...]
    bits = pltpu.bitcast(a, I32)
    lane = lax.broadcasted_iota(I32, (experts, n_tok), 1)

    def count(mask):
        return jnp.sum(mask.astype(I32), axis=1, keepdims=True)

    def thr_step(i, thr):
        hi = jnp.left_shift(jnp.int32(1), 29 - 2 * i)
        lo = jnp.left_shift(jnp.int32(1), 28 - 2 * i)
        best = thr
        for cand in (thr | lo, thr | hi, thr | hi | lo):
            best = jnp.where(count(bits >= cand) >= cap, cand, best)
        return best

    first = jnp.int32(1 << 30)
    thr = jnp.where(count(bits >= first) >= cap, first, jnp.zeros((experts, 1), I32))
    thr = lax.fori_loop(0, 15, thr_step, thr)
    above = bits > thr
    tie = bits == thr
    need = cap - count(above)

    top_bit = n_tok.bit_length() - 2

    def tie_try(lim, cands):
        for cand in cands:
            lim = jnp.where(count(tie & (lane < cand)) < need, cand, lim)
        return lim

    def tie_step(i, lim):
        hi = jnp.left_shift(jnp.int32(1), top_bit - odd - 2 * i)
        lo = jnp.left_shift(jnp.int32(1), top_bit - odd - 2 * i - 1)
        return tie_try(lim, (lim | lo, lim | hi, lim | hi | lo))

    odd = (top_bit + 1) % 2
    lim = jnp.zeros((experts, 1), I32)
    if odd:
        lim = tie_try(lim, (lim | (1 << top_bit),))
    lim = lax.fori_loop(0, (top_bit + 1) // 2, tie_step, lim)
    sel = above | (tie & (lane <= lim))
    sel_f = sel.astype(F32)
    sel_i = sel.astype(I32)

    r = lax.broadcasted_iota(I32, (128, 128), 0)
    c = lax.broadcasted_iota(I32, (128, 128), 1)
    tri = (r <= c).astype(F32).astype(BF16)

    level = jnp.zeros((1, n_tok), I32)
    levels = []
    for e in range(experts):
        levels.append(level)
        level = level + sel_i[e:e + 1, :]
    cnt = level
    off = _lane_prefix(cnt.astype(F32), tri).astype(I32) + (pair0 + grp * (experts * cap))
    off_ref[...] = off
    cnt_ref[...] = cnt
    pair_row = jnp.concatenate(levels, axis=0) + off

    n_ch = n_tok // 128
    pad = w_scr.shape[0] // experts
    lane_e = lax.broadcasted_iota(I32, (experts, 128), 1)
    w_scr[...] = jnp.zeros_like(w_scr)
    a_scr[...] = jnp.zeros_like(a_scr)
    q_scr[...] = jnp.zeros_like(q_scr)
    first = jnp.zeros((experts, 1), F32)
    c_lo = jnp.full((experts, 128), cap, I32)
    c_hi = jnp.full((experts, 128), cap, I32)
    for ch in range(n_ch):
        lanes = slice(ch * 128, (ch + 1) * 128)
        incl = jnp.dot(sel_f[:, lanes].astype(BF16), tri, preferred_element_type=F32)
        w_scr[pl.ds(ch, experts, stride=pad), :] = incl
        a_scr[pl.ds(ch, experts, stride=pad), :] = a[:, lanes]
        q_scr[pl.ds(ch, experts, stride=pad), :] = pair_row[:, lanes]
        nxt = first + incl[:, 127:128]
        c_lo = jnp.where(lane_e == ch, first.astype(I32), c_lo)
        c_hi = jnp.where(lane_e == ch, nxt.astype(I32), c_hi)
        first = nxt
    clo_scr[...] = c_lo
    chi_scr[...] = c_hi

    idx_ref[0] = jnp.zeros((cap, 128), I32)
    dst_ref[0] = jnp.zeros((cap, 128), I32)
    gate_ref[0] = jnp.zeros((cap, 128), F32)
    slot = lax.broadcasted_iota(I32, (cap, 1), 0)
    lane = lax.broadcasted_iota(I32, (1, 128), 1)
    zpad = jnp.zeros((128 - pad, 128), F32)

    def rhs(m):
        return jnp.concatenate([m, zpad], axis=0).astype(BF16)

    def per_expert(e, carry):
        base = pl.multiple_of(e * pad, pad)
        wm = w_scr[pl.ds(base, pad), :]
        am = a_scr[pl.ds(base, pad), :]
        qm = q_scr[pl.ds(base, pad), :]
        lo = clo_scr[pl.ds(e, 1), :]
        hi = chi_scr[pl.ds(e, 1), :]
        in_chunk = (lo <= slot) & (slot < hi)
        onehot = in_chunk.astype(F32).astype(BF16)

        def rows_of(m):
            return jnp.dot(onehot, rhs(m), preferred_element_type=F32)

        a1 = am.astype(BF16).astype(F32)
        a2 = (am - a1).astype(BF16).astype(F32)
        a3 = am - a1 - a2
        a_rows = rows_of(a1) + rows_of(a2) + rows_of(a3)
        q_rows = (rows_of(lax.shift_right_logical(qm, 8).astype(F32)) * 256.0
                  + rows_of((qm & 255).astype(F32)))
        w_rows = rows_of(wm)
        chunk_first = jnp.sum(jnp.where(in_chunk, lo, 0), axis=1, keepdims=True)
        chunk = jnp.sum(jnp.where(in_chunk, lane, 0), axis=1, keepdims=True)
        rank = (slot - chunk_first).astype(F32)
        pos = jnp.sum((w_rows <= rank).astype(I32), axis=1, keepdims=True)
        here = lane == pos
        gate = jnp.sum(jnp.where(here, a_rows, 0.0), axis=1, keepdims=True)
        dst = jnp.sum(jnp.where(here, q_rows, 0.0), axis=1, keepdims=True).astype(I32)
        token = chunk * 128 + pos + grp * n_tok
        mine = lane == e
        idx_ref[0] = jnp.where(mine, token, idx_ref[0])
        dst_ref[0] = jnp.where(mine, dst, dst_ref[0])
        gate_ref[0] = jnp.where(mine, gate, gate_ref[0])
        return carry

    lax.fori_loop(0, experts, per_expert, 0)


def _topk(st, aff, cap, pair0):
    e = aff.shape[0]
    n_tok, n_groups = st.seq, st.batch
    pad = max(8, n_tok // 128)
    assert e <= 128 and pad <= 128 and n_groups * e * cap < 2 ** 16 * 256
    tile_spec = pl.BlockSpec((1, cap, 128), lambda g: (g, 0, 0))
    return pl.pallas_call(
        functools.partial(_topk_body, n_tok=n_tok, cap=cap, pair0=pair0, experts=e),
        grid=(n_groups,),
        in_specs=[pl.BlockSpec((e, n_tok), lambda g: (0, g))],
        out_specs=[tile_spec, tile_spec, tile_spec,
                   pl.BlockSpec((1, n_tok), lambda g: (0, g)),
                   pl.BlockSpec((1, n_tok), lambda g: (0, g))],
        out_shape=[jax.ShapeDtypeStruct((n_groups, cap, 128), I32),
                   jax.ShapeDtypeStruct((n_groups, cap, 128), I32),
                   jax.ShapeDtypeStruct((n_groups, cap, 128), F32),
                   jax.ShapeDtypeStruct((1, n_groups * n_tok), I32),
                   jax.ShapeDtypeStruct((1, n_groups * n_tok), I32)],
        scratch_shapes=[pltpu.VMEM((e * pad, 128), F32), pltpu.VMEM((e * pad, 128), F32),
                        pltpu.VMEM((e * pad, 128), I32), pltpu.VMEM((e, 128), I32), pltpu.VMEM((e, 128), I32)],
        compiler_params=_params("arbitrary"),
        name="moe_topk",
    )(aff)


def _expert_body(*refs, n_src, src_slots, slots, rc, n_f, n_n, tf, n_e):
    idx_ref, idx_nxt_ref, dst_ref, dst_prv_ref = refs[0:4]
    srcs = refs[4:4 + n_src]
    gate_ref, wg_ref, wu_ref, wd_ref, pairs_hbm, x_scr, y_scr, mid_scr, gcol_scr, gsem, ssem = refs[4 + n_src:]
    e = pl.program_id(0)
    s = pl.program_id(1)
    nck = x_scr.shape[0] // slots
    half = nck * 128
    tnc = nck // n_n

    def tok(ref, r):
        return ref.at[pl.ds(pl.multiple_of(r * nck, nck), nck), :]

    def row_loop(lo, n, start_row):
        def eight(i, carry):
            for k in range(8):
                start_row(lo + i * 8 + k, k)
            return carry
        lax.fori_loop(0, n // 8, eight, 0)

    def gather_row(ref, r, slot):
        src = next(src for (s0, s1), src in zip(src_slots, srcs) if s0 <= slot < s1)
        pltpu.make_async_copy(tok(src, ref[0, 0, r]), tok(x_scr, r), gsem).start(priority=slot % 2)

    def scatter_row(ref, r, k):
        pltpu.make_async_copy(tok(y_scr, r), tok(pairs_hbm, ref[0, 0, r]), ssem).start(priority=k % 2)

    n_chunks = slots // rc
    dma_chunks = max(n_chunks - 1, 1)

    def share(total, ci):
        base, extra = divmod(total, dma_chunks)
        if ci >= dma_chunks:
            return total, 0
        return ci * base + min(ci, extra), base + (1 if ci < extra else 0)

    def wait_gather():
        pltpu.make_async_copy(srcs[0].at[pl.ds(0, slots * nck), :], x_scr, gsem).wait()

    def wait_scatter():
        pltpu.make_async_copy(y_scr, pairs_hbm.at[pl.ds(0, slots * nck), :], ssem).wait()

    def x_rows(r0):
        parts = [_unpack_bf16_pair(x_scr[pl.ds(r0 * nck + c, rc, stride=nck), :]) for c in range(nck)]
        return (jnp.concatenate([p[0] for p in parts], axis=1), jnp.concatenate([p[1] for p in parts], axis=1))

    @pl.when((e == 0) & (s == 0))
    def _():
        y_scr[...] = jnp.zeros_like(y_scr)
        for (s0, s1), src in zip(src_slots, srcs):
            row_loop(s0, s1 - s0, lambda r, k, src=src: pltpu.make_async_copy(
                tok(src, idx_ref[0, 0, r]), tok(x_scr, r), gsem).start(priority=k % 2))

    @pl.when(s == 0)
    def _():
        wait_gather()
        eye = lax.broadcasted_iota(I32, (128, 128), 0) == lax.broadcasted_iota(I32, (128, 128), 1)
        for c in range(slots // 128):
            row = gate_ref[0, c:c + 1, :]
            gcol_scr[c * 128:(c + 1) * 128, :] = jnp.sum(jnp.where(eye, row, 0.0), axis=1, keepdims=True)

    @pl.when(s < n_f)
    def _():
        wg = wg_ref[0, 0].astype(BF16)
        wu = wu_ref[0, 0].astype(BF16)
        for r0 in range(0, slots, rc):
            lo, hi = x_rows(r0)
            a = (jnp.dot(lo, wg[:half], preferred_element_type=F32)
                 + jnp.dot(hi, wg[half:], preferred_element_type=F32))
            u = (jnp.dot(lo, wu[:half], preferred_element_type=F32)
                 + jnp.dot(hi, wu[half:], preferred_element_type=F32))
            mid_scr[s, r0:r0 + rc, :] = (a * jax.nn.sigmoid(a) * u).astype(BF16)
            off, cnt = share(slots // n_f, r0 // rc)
            for k in range(cnt):
                scatter_row(dst_prv_ref, s * (slots // n_f) + off + k, k)

    @pl.when(s == n_f)
    def _():
        wait_scatter()

    for j in range(n_n):
        @pl.when(s == n_f + j)
        def _():
            wd = wd_ref[0, 0].astype(BF16)
            for r0 in range(0, slots, rc):
                y = jnp.dot(mid_scr[0, r0:r0 + rc, :], wd[0:tf], preferred_element_type=F32)
                for f in range(1, n_f):
                    y += jnp.dot(mid_scr[f, r0:r0 + rc, :], wd[f * tf:(f + 1) * tf], preferred_element_type=F32)
                y = y * gcol_scr[r0:r0 + rc, :]
                tnw = tnc * 128
                packed = _pack_bf16_pair(y[:, :tnw], y[:, tnw:])
                for c in range(tnc):
                    y_scr[pl.ds(r0 * nck + j * tnc + c, rc, stride=nck), :] = packed[:, c * 128:(c + 1) * 128]
                g_steps = max(n_n - 1, 1)
                step_rows = -(-slots // g_steps)
                lo = min(j * step_rows, slots)
                off, cnt = share(min(step_rows, slots - lo), r0 // rc)
                for k in range(cnt):
                    slot = lo + off + k
                    gather_row(idx_nxt_ref, slot, slot)

    @pl.when((e == n_e - 1) & (s == n_f + n_n - 1))
    def _():
        wait_gather()
        row_loop(0, slots, lambda r, k: scatter_row(dst_ref, r, k))
        wait_scatter()


def _experts(idx, dst, gate, sources, src_slots, w_gate, w_up, w_down, layer, n_pairs):
    n_e, slots = idx.shape
    d, fdim = w_gate.shape[2], w_gate.shape[3]
    tf, tn = 256, EXPERT_DOWN_TILE
    n_f, n_n = fdim // tf, d // tn
    rc = slots // 4
    nck = d // 256
    assert slots % 128 == 0 and rc % 16 == 0 and nck % n_n == 0 and slots % n_f == 0 and slots % n_n == 0
    assert all(s0 % 8 == 0 and s1 % 8 == 0 for s0, s1 in src_slots)
    idx3 = idx.reshape(n_e, 1, slots)
    dst3 = dst.reshape(n_e, 1, slots)
    smem = lambda f: pl.BlockSpec((1, 1, slots), f, memory_space=pltpu.SMEM)
    up_chunk = lambda i, s: (layer, i, 0, jnp.minimum(s, n_f - 1))
    return pl.pallas_call(
        functools.partial(_expert_body, n_src=len(sources), src_slots=src_slots, slots=slots, rc=rc,
                          n_f=n_f, n_n=n_n, tf=tf, n_e=n_e),
        grid=(n_e, n_f + n_n),
        in_specs=[
            smem(lambda i, s: (i, 0, 0)),
            smem(lambda i, s: (jnp.minimum(i + 1, n_e - 1), 0, 0)),
            smem(lambda i, s: (i, 0, 0)),
            smem(lambda i, s: (jnp.maximum(i - 1, 0), 0, 0)),
        ] + [pl.BlockSpec(memory_space=pl.ANY)] * len(sources) + [
            pl.BlockSpec((1, slots // 128, 128), lambda i, s: (i, 0, 0)),
            pl.BlockSpec((1, 1, d, tf), up_chunk),
            pl.BlockSpec((1, 1, d, tf), up_chunk),
            pl.BlockSpec((1, 1, fdim, tn), lambda i, s: (layer, i, 0, jnp.maximum(s - n_f, 0))),
        ],
        out_specs=pl.BlockSpec(memory_space=pl.ANY),
        out_shape=jax.ShapeDtypeStruct((n_pairs * nck, 128), I32),
        scratch_shapes=[
            pltpu.VMEM((slots * nck, 128), I32),
            pltpu.VMEM((slots * nck, 128), I32),
            pltpu.VMEM((n_f, slots, tf), BF16),
            pltpu.VMEM((slots, 1), F32),
            pltpu.SemaphoreType.DMA(()),
            pltpu.SemaphoreType.DMA(()),
        ],
        compiler_params=_params("arbitrary", "arbitrary"),
        name="moe_experts",
    )(idx3, idx3, dst3, dst3, *sources, gate.reshape(n_e, slots // 128, 128), w_gate, w_up, w_down)


def _combine_body(lo_ref, hi_ref, off_ref, cnt_ref, p_ref, h_ref, g2_ref, o_ref, acc_lo, acc_hi, *,
                  tt, nck, tnw, pair0, pairs_per_group):
    base = pair0 + pl.program_id(0) * pairs_per_group
    i = pl.program_id(1)
    lane = lax.broadcasted_iota(I32, (1, PAIR_BLOCK), 1)
    eye = lax.broadcasted_iota(I32, (128, 128), 0) == lax.broadcasted_iota(I32, (128, 128), 1)

    def column(row):
        cols = [jnp.sum(jnp.where(eye, row[:, c0:c0 + 128], 0), axis=1, keepdims=True) for c0 in range(0, tt, 128)]
        return cols[0] if len(cols) == 1 else jnp.concatenate(cols, axis=0)

    first = column(off_ref[...]) - base
    last = first + column(cnt_ref[...])
    shift = PAIR_BLOCK.bit_length() - 1
    k0 = lax.shift_right_logical(lo_ref[0, 0, i] - base, shift)
    k1 = lax.shift_right_logical(hi_ref[0, 0, i] - base + (PAIR_BLOCK - 1), shift)
    acc_lo[...] = jnp.zeros_like(acc_lo)
    acc_hi[...] = jnp.zeros_like(acc_hi)

    def step(k, carry):
        p0 = pl.multiple_of(k * PAIR_BLOCK, PAIR_BLOCK)
        parts = [_unpack_bf16_pair(p_ref[pl.ds(p0 * nck + c, PAIR_BLOCK, stride=nck), :]) for c in range(nck)]
        lo = jnp.concatenate([p[0] for p in parts], axis=1)
        hi = jnp.concatenate([p[1] for p in parts], axis=1)
        pr = p0 + lane
        seg = ((pr >= first) & (pr < last)).astype(F32).astype(BF16)
        acc_lo[...] += jnp.dot(seg, lo, preferred_element_type=F32)
        acc_hi[...] += jnp.dot(seg, hi, preferred_element_type=F32)
        return carry

    lax.fori_loop(k0, k1, step, 0)
    for j in range(nck * 128 // tnw):
        words = slice(j * tnw, (j + 1) * tnw)
        c_lo = slice(2 * j * tnw, (2 * j + 1) * tnw)
        c_hi = slice((2 * j + 1) * tnw, (2 * j + 2) * tnw)
        o_ref[:, c_lo] = h_ref[:, c_lo] + g2_ref[0, :, c_lo] * acc_lo[:, words]
        o_ref[:, c_hi] = h_ref[:, c_hi] + g2_ref[0, :, c_hi] * acc_hi[:, words]


def _combine(st, off, cnt, pairs, h, mods, pair0, pairs_per_group):
    rows, d = h.shape
    n_tok = st.seq
    tt = min(256, n_tok)
    n_tiles = n_tok // tt
    nck = d // 256
    assert pairs_per_group % PAIR_BLOCK == 0 and pair0 % pairs_per_group == 0
    off_t = off.reshape(st.batch, 1, n_tiles, tt)
    cnt_t = cnt.reshape(st.batch, 1, n_tiles, tt)
    tile_lo = off_t[..., 0]
    tile_hi = off_t[..., tt - 1] + cnt_t[..., tt - 1]
    row = st.ada_row(n_tok)
    smem = pl.BlockSpec((1, 1, n_tiles), lambda b, i: (b, 0, 0), memory_space=pltpu.SMEM)
    tile = lambda b, i: (b * n_tiles + i, 0)
    return pl.pallas_call(
        functools.partial(_combine_body, tt=tt, nck=nck, tnw=EXPERT_DOWN_TILE // 2, pair0=pair0,
                          pairs_per_group=pairs_per_group),
        grid=(st.batch, n_tiles),
        in_specs=[
            smem, smem,
            pl.BlockSpec((1, tt), lambda b, i: (0, b * n_tiles + i)),
            pl.BlockSpec((1, tt), lambda b, i: (0, b * n_tiles + i)),
            pl.BlockSpec((pairs_per_group * nck, 128), lambda b, i: (pair0 // pairs_per_group + b, 0)),
            pl.BlockSpec((tt, d), tile),
            pl.BlockSpec((1, 1, d), lambda b, i: (row(b) * N_ADA + 5, 0, 0)),
        ],
        out_specs=pl.BlockSpec((tt, d), tile),
        out_shape=jax.ShapeDtypeStruct((rows, d), F32),
        scratch_shapes=[pltpu.VMEM((tt, d // 2), F32)] * 2,
        compiler_params=_params("arbitrary", "arbitrary"),
        name="moe_combine",
    )(tile_lo, tile_hi, off, cnt, pairs, h, mods)


def _moe(streams, mixed, mods, w_gate, w_up, w_down, layer):
    n_e = mixed[0][2].shape[0]
    routed = []
    pair0 = 0
    slot0 = 0
    src_slots = []
    for st, (h, h2, aff) in zip(streams, mixed):
        cap = CAP_FACTOR * st.seq // n_e
        idx, dst, gate, off, cnt = _topk(st, aff, cap, pair0)
        routed.append((h2, idx, dst, gate, off, cnt, pair0, n_e * cap))
        src_slots.append((slot0, slot0 + st.batch * cap))
        pair0 += st.batch * n_e * cap
        slot0 += st.batch * cap
    flat = lambda t: t[:, :, :n_e].transpose(2, 0, 1).reshape(n_e, -1)
    idx = jnp.concatenate([flat(r[1]) for r in routed], axis=1)
    dst = jnp.concatenate([flat(r[2]) for r in routed], axis=1)
    gate = jnp.concatenate([flat(r[3]) for r in routed], axis=1)
    pairs = _experts(idx, dst, gate, [r[0] for r in routed], tuple(src_slots), w_gate, w_up, w_down, layer, pair0)
    return [_combine(st, r[4], r[5], pairs, m[0], mods, r[6], r[7]) for st, m, r in zip(streams, mixed, routed)]


def _forward(dims, x, c, ctx, c_ctx, ada_w, ada_b, norm_mix_g, norm_ffn_g, conv_w_in, conv_w, conv_w_out,
             attn_w_qkv, attn_q_gain, attn_k_gain, attn_sink, attn_w_o, router_w,
             expert_w_gate, expert_w_up, expert_w_down):
    d = dims.d
    lat = Stream(dims.batch, dims.seq)
    con = Stream(dims.batch, dims.ctx, shared_row=dims.batch)
    cvec = jnp.zeros((ADA_ROWS, d), F32).at[:dims.batch].set(c).at[dims.batch].set(c_ctx)
    mods = _adaln(cvec, ada_w, ada_b)
    mods = mods.reshape(mods.shape[0], ADA_ROWS * N_ADA, 1, d)
    hs = [x.reshape(lat.rows, d), ctx.reshape(con.rows, d)]

    w_in, w_out = conv_w_in[0].astype(BF16), conv_w_out[0].astype(BF16)
    mixed = []
    for st, h in zip((lat, con), hs):
        gb_u = _inproj(st, h, mods[0], norm_mix_g[0], w_in)
        mixed.append(_mixout(st, gb_u, conv_w[0], w_out, h, mods[0], norm_ffn_g[0], router_w[0]))
    h_lat, h_ctx = _moe((lat, con), mixed, mods[0], expert_w_gate, expert_w_up, expert_w_down, 0)

    n_qk = dims.heads + dims.kv_heads
    w_qk = attn_w_qkv[0, :, :n_qk * HEAD_DIM].reshape(d, n_qk, 2, 2, HEAD_DIM // 4)
    w_qk = w_qk.transpose(0, 1, 3, 2, 4).reshape(d, n_qk * HEAD_DIM)
    w_qkv = jnp.concatenate([w_qk, attn_w_qkv[0, :, n_qk * HEAD_DIM:]], axis=1).astype(BF16)
    w_o = attn_w_o[0].astype(BF16)
    qkv = _qkvproj(dims, lat, h_lat, mods[1], norm_mix_g[1], w_qkv, attn_q_gain[0], attn_k_gain[0], True)
    kv_ctx = _qkvproj(dims, con, h_ctx, mods[1], norm_mix_g[1], w_qkv, attn_q_gain[0], attn_k_gain[0], False)
    o = _attention(dims, qkv, kv_ctx, attn_sink[0])
    mixed = [_mixout(lat, (o,), None, w_o, h_lat, mods[1], norm_ffn_g[1], router_w[1])]
    (h_lat,) = _moe((lat,), mixed, mods[1], expert_w_gate, expert_w_up, expert_w_down, 1)
    return h_lat.reshape(dims.batch, dims.seq, d)


def kernel(x, c, ctx, c_ctx, ada_w, ada_b, norm_mix_g, norm_ffn_g, conv_w_in, conv_w, conv_w_out, attn_w_qkv, attn_q_gain, attn_k_gain, attn_sink, attn_w_o, router_w, expert_w_gate, expert_w_up, expert_w_down):
    batch, seq, d = x.shape
    dims = Dims(d=d, batch=batch, seq=seq, grid_w=GRID_W, ctx=ctx.shape[1], heads=d // HEAD_DIM,
                kv_heads=(attn_w_qkv.shape[2] // HEAD_DIM - d // HEAD_DIM) // 2,
                experts=router_w.shape[2], d_expert=expert_w_gate.shape[3])
    return _forward(dims, x, c, ctx, c_ctx, ada_w, ada_b, norm_mix_g, norm_ffn_g, conv_w_in, conv_w, conv_w_out,
                    attn_w_qkv, attn_q_gain, attn_k_gain, attn_sink, attn_w_o, router_w,
                    expert_w_gate, expert_w_up, expert_w_down)
```
